```python
import math
import jax
import jax.numpy as jnp
from jax import lax
import numpy as np

D_MODEL = 2048
BATCH = 2
SEQ = 4096
DEPTH = 4

GRID_W = 64
CTX_LEN = 256
EPS = 1e-6
N_MOD = 6

S5_WIDTH = 1024
S5_GROUP = 16
S5_GROUPS = S5_WIDTH // S5_GROUP
S5_STATE = 64

SSD_WIDTH = 1024
SSD_HEAD_DIM = 64
SSD_HEADS = SSD_WIDTH // SSD_HEAD_DIM
SSD_GROUPS = 2
SSD_STATE = 128
SSD_CONV = 5
SSD_CHUNK = 128
SSD_XBC = SSD_WIDTH + 2 * SSD_GROUPS * SSD_STATE

D_IN = S5_WIDTH + SSD_WIDTH + SSD_XBC + 2 * SSD_HEADS
D_MIX = S5_WIDTH + SSD_WIDTH

N_EXPERTS = 16
CAPACITY_FACTOR = 2
D_FF = 1536

kernel_name = 'hybrid_s5_ssd_ecmoe_diffusion_trunk'

F32 = jnp.float32


def rmsnorm(x, w):
    xf = x.astype(F32)
    y = xf * lax.rsqrt(jnp.mean(xf * xf, axis=-1, keepdims=True) + EPS)
    return (y * w.astype(F32)).astype(x.dtype)


def to_scan_order(x, col_major):
    if not col_major:
        return x
    b, n = x.shape[:2]
    rows = n // GRID_W
    return x.reshape((b, rows, GRID_W) + x.shape[2:]).swapaxes(1, 2).reshape(x.shape)


def from_scan_order(x, col_major):
    if not col_major:
        return x
    b, n = x.shape[:2]
    rows = n // GRID_W
    return x.reshape((b, GRID_W, rows) + x.shape[2:]).swapaxes(1, 2).reshape(x.shape)


def dir_seq(a, lc, reverse):
    if not reverse:
        return a
    return jnp.concatenate([jnp.flip(a[:, :lc], 1), jnp.flip(a[:, lc:], 1)], axis=1)


def dwconv_centred(x, w, b):
    pad = w.shape[0] // 2
    y = lax.conv_general_dilated(x, w[:, None, :].astype(x.dtype), window_strides=(1,),
                                 padding=((pad, pad),), dimension_numbers=('NWC', 'WIO', 'NWC'),
                                 feature_group_count=x.shape[-1])
    return y + b.astype(x.dtype)


def s5_discretise(lam_re, lam_im, log_step, b_re, b_im):
    step = jnp.exp(log_step)[:, None]
    mag = jnp.exp(lam_re * step)
    ang = lam_im * step
    lb_re, lb_im = mag * jnp.cos(ang), mag * jnp.sin(ang)
    den = lam_re * lam_re + lam_im * lam_im
    nr = lb_re - 1.0
    f_re = (nr * lam_re + lb_im * lam_im) / den
    f_im = (lb_im * lam_re - nr * lam_im) / den
    bb_re = f_re[..., None] * b_re - f_im[..., None] * b_im
    bb_im = f_re[..., None] * b_im + f_im[..., None] * b_re
    return lb_re, lb_im, bb_re, bb_im


def _complex_affine_combine(e1, e2):
    a1r, a1i, b1r, b1i = e1
    a2r, a2i, b2r, b2i = e2
    return (a2r * a1r - a2i * a1i, a2r * a1i + a2i * a1r,
            a2r * b1r - a2i * b1i + b2r, a2r * b1i + a2i * b1r + b2i)


def s5_states(u, lb_re, lb_im, bb_re, bb_im):
    bu_re = jnp.einsum('btgh,gph->btgp', u, bb_re)
    bu_im = jnp.einsum('btgh,gph->btgp', u, bb_im)
    a_re = jnp.broadcast_to(lb_re, bu_re.shape)
    a_im = jnp.broadcast_to(lb_im, bu_im.shape)
    _, _, h_re, h_im = lax.associative_scan(_complex_affine_combine, (a_re, a_im, bu_re, bu_im), axis=1)
    return h_re, h_im


def s5_mixer(u, lc, lam_re, lam_im, log_step, b_re, b_im, c_re, c_im, d_skip, w_glu, b_glu):
    bsz, t, _ = u.shape
    uf = u.astype(F32).reshape(bsz, t, S5_GROUPS, S5_GROUP)
    b_re, b_im = b_re.astype(F32), b_im.astype(F32)
    h_re, h_im = 0.0, 0.0
    for d in range(2):
        rev = d == 1
        lb_re, lb_im, bb_re, bb_im = s5_discretise(lam_re[d].astype(F32), lam_im[d].astype(F32),
                                                   log_step[d].astype(F32), b_re, b_im)
        s_re, s_im = s5_states(dir_seq(uf, lc, rev), lb_re, lb_im, bb_re, bb_im)
        h_re = h_re + dir_seq(s_re, lc, rev)
        h_im = h_im + dir_seq(s_im, lc, rev)
    y = (jnp.einsum('btgp,ghp->btgh', h_re, c_re.astype(F32))
         - jnp.einsum('btgp,ghp->btgh', h_im, c_im.astype(F32))
         + d_skip.astype(F32).reshape(S5_GROUPS, S5_GROUP) * uf)
    y = jax.nn.gelu(y.reshape(bsz, t, S5_WIDTH)).astype(u.dtype)
    gl = y @ w_glu + b_glu
    return gl[..., :S5_WIDTH] * jax.nn.sigmoid(gl[..., S5_WIDTH:])


def ssd_chunked(x, dt, a, bm, cm):
    b, t, h, p = x.shape
    g, n = bm.shape[2], bm.shape[3]
    j = h // g
    nc, l = t // SSD_CHUNK, SSD_CHUNK
    xd = (x * dt[..., None]).reshape(b, nc, l, g, j, p)
    bm = bm.reshape(b, nc, l, g, n)
    cm = cm.reshape(b, nc, l, g, n)
    cum = jnp.cumsum((dt * a).reshape(b, nc, l, g, j), axis=2)
    seg = cum[:, :, :, None] - cum[:, :, None, :]
    lower = jnp.tril(jnp.ones((l, l), dtype=bool))[:, :, None, None]
    decay = jnp.exp(jnp.where(lower, seg, -jnp.inf))
    scores = jnp.einsum('bclgn,bcsgn->bclsg', cm, bm)
    y_diag = jnp.einsum('bclsg,bclsgj,bcsgjp->bclgjp', scores, decay, xd)
    end_decay = jnp.exp(cum[:, :, -1:] - cum)
    chunk_states = jnp.einsum('bclgn,bclgj,bclgjp->bcgjpn', bm, end_decay, xd)
    chunk_decay = jnp.exp(cum[:, :, -1])

    def carry_step(state, inp):
        st, dec = inp
        return dec[..., None, None] * state + st, state

    h0 = jnp.zeros((b, g, j, p, n), x.dtype)
    _, h_in = lax.scan(carry_step, h0, (jnp.moveaxis(chunk_states, 1, 0), jnp.moveaxis(chunk_decay, 1, 0)))
    h_in = jnp.moveaxis(h_in, 0, 1)
    y_off = jnp.einsum('bclgn,bcgjpn,bclgj->bclgjp', cm, h_in, jnp.exp(cum))
    return (y_diag + y_off).reshape(b, t, h, p)


def ssd_mixer(z, xbc, dt_raw, lc, conv_w, conv_b, dt_bias, a_log, d_skip, norm_w):
    bsz, t, _ = xbc.shape
    xbc = jnp.concatenate([dwconv_centred(xbc[:, :lc], conv_w, conv_b),
                           dwconv_centred(xbc[:, lc:], conv_w, conv_b)], axis=1)
    xbc = jax.nn.silu(xbc).astype(F32)
    gn = SSD_GROUPS * SSD_STATE
    xs = xbc[..., :SSD_WIDTH].reshape(bsz, t, SSD_HEADS, SSD_HEAD_DIM)
    bm = xbc[..., SSD_WIDTH:SSD_WIDTH + gn].reshape(bsz, t, SSD_GROUPS, SSD_STATE)
    cm = xbc[..., SSD_WIDTH + gn:].reshape(bsz, t, SSD_GROUPS, SSD_STATE)
    dt_raw = dt_raw.astype(F32).reshape(bsz, t, 2, SSD_HEADS)
    y = d_skip.astype(F32)[:, None] * xs
    for d in range(2):
        rev = d == 1
        dt = jax.nn.softplus(dt_raw[:, :, d] + dt_bias[d].astype(F32))
        a = -jnp.exp(a_log[d].astype(F32))
        y_d = ssd_chunked(dir_seq(xs, lc, rev), dir_seq(dt, lc, rev), a,
                          dir_seq(bm, lc, rev), dir_seq(cm, lc, rev))
        y = y + dir_seq(y_d, lc, rev)
    y = y.reshape(bsz, t, SSD_WIDTH)
    return rmsnorm(y * jax.nn.silu(z.astype(F32)), norm_w).astype(z.dtype)


def expert_choice_ffn(x, w_router, w_gate, w_up, w_down):
    bsz, t, d = x.shape
    cap = CAPACITY_FACTOR * t // N_EXPERTS
    aff = jax.nn.softmax(jnp.einsum('btd,de->bte', x, w_router).astype(F32), axis=-1)
    gate, idx = lax.top_k(jnp.swapaxes(aff, 1, 2), cap)
    xs = jax.vmap(lambda xb, ib: xb[ib])(x, idx)
    hid = jax.nn.silu(jnp.einsum('becd,edf->becf', xs, w_gate)) * jnp.einsum('becd,edf->becf', xs, w_up)
    out = jnp.einsum('becf,efd->becd', hid, w_down) * gate[..., None].astype(x.dtype)
    return jax.vmap(lambda ob, ib: jnp.zeros((t, d), ob.dtype).at[ib.reshape(-1)].add(ob.reshape(-1, d)))(out, idx)


def setup_inputs(seed: int = 0) -> dict:
    key = jax.random.key(seed)
    ks = jax.random.split(key, 32)
    L, G, P, H = DEPTH, S5_GROUPS, S5_STATE, S5_GROUP

    def nrm(k, shape, scale):
        return jax.random.normal(k, shape, F32) * scale

    n_idx = jnp.arange(S5_STATE, dtype=F32)
    dt0 = jnp.exp(jax.random.uniform(ks[21], (L, 2, SSD_HEADS), F32, math.log(1e-3), math.log(1e-1)))
    return {
        'x': nrm(ks[0], (BATCH, SEQ, D_MODEL), 1.0),
        'c': nrm(ks[1], (BATCH, D_MODEL), 1.0),
        'ctx': nrm(ks[2], (BATCH, CTX_LEN, D_MODEL), 1.0),
        'c_ctx': nrm(ks[3], (D_MODEL,), 1.0),
        'ada_w': nrm(ks[4], (L, D_MODEL, N_MOD * D_MODEL), 0.2 * D_MODEL ** -0.5),
        'ada_b': nrm(ks[5], (L, N_MOD * D_MODEL), 0.02),
        'norm_g': 1.0 + nrm(ks[6], (L, 4, D_MODEL), 0.02),
        'w_in': nrm(ks[7], (L, D_MODEL, D_IN), D_MODEL ** -0.5),
        'w_out': nrm(ks[8], (L, D_MIX, D_MODEL), D_MIX ** -0.5),
        's5_lam_re': -0.5 + nrm(ks[9], (L, 2, G, P), 0.01),
        's5_lam_im': math.pi * n_idx + nrm(ks[10], (L, 2, G, P), 0.01),
        's5_log_step': jax.random.uniform(ks[11], (L, 2, G), F32, math.log(1e-3), math.log(1e-1)),
        's5_b_re': nrm(ks[12], (L, G, P, H), (2 * H) ** -0.5),
        's5_b_im': nrm(ks[13], (L, G, P, H), (2 * H) ** -0.5),
        's5_c_re': nrm(ks[14], (L, G, H, P), (2 * P) ** -0.5),
        's5_c_im': nrm(ks[15], (L, G, H, P), (2 * P) ** -0.5),
        's5_d': nrm(ks[16], (L, S5_WIDTH), 1.0),
        's5_w_glu': nrm(ks[17], (L, S5_WIDTH, 2 * S5_WIDTH), S5_WIDTH ** -0.5),
        's5_b_glu': nrm(ks[18], (L, 2 * S5_WIDTH), 0.02),
        'ssd_conv_w': nrm(ks[19], (L, SSD_CONV, SSD_XBC), SSD_CONV ** -0.5),
        'ssd_conv_b': nrm(ks[20], (L, SSD_XBC), 0.02),
        'ssd_dt_bias': dt0 + jnp.log(-jnp.expm1(-dt0)),
        'ssd_a_log': jnp.log(jax.random.uniform(ks[22], (L, 2, SSD_HEADS), F32, 1.0, 16.0)),
        'ssd_d': 1.0 + nrm(ks[23], (L, SSD_HEADS), 0.1),
        'ssd_norm': 1.0 + nrm(ks[24], (L, SSD_WIDTH), 0.02),
        'moe_router': nrm(ks[25], (L, D_MODEL, N_EXPERTS), D_MODEL ** -0.5),
        'moe_w_gate': nrm(ks[26], (L, N_EXPERTS, D_MODEL, D_FF), D_MODEL ** -0.5),
        'moe_w_up': nrm(ks[27], (L, N_EXPERTS, D_MODEL, D_FF), D_MODEL ** -0.5),
        'moe_w_down': nrm(ks[28], (L, N_EXPERTS, D_FF, D_MODEL), D_FF ** -0.5),
    }


def reference(x, c, ctx, c_ctx, ada_w, ada_b, norm_g, w_in, w_out, s5_lam_re, s5_lam_im, s5_log_step,
              s5_b_re, s5_b_im, s5_c_re, s5_c_im, s5_d, s5_w_glu, s5_b_glu, ssd_conv_w, ssd_conv_b,
              ssd_dt_bias, ssd_a_log, ssd_d, ssd_norm, moe_router, moe_w_gate, moe_w_up, moe_w_down):
    lc = ctx.shape[1]
    xl, xc = x, ctx
    s1 = S5_WIDTH
    s2 = s1 + SSD_WIDTH
    s3 = s2 + SSD_XBC
    for i in range(DEPTH):
        col_major = i % 2 == 1
        last = i == DEPTH - 1
        mod_l = (jax.nn.silu(c) @ ada_w[i] + ada_b[i])[:, None, :]
        mod_c = (jax.nn.silu(c_ctx) @ ada_w[i] + ada_b[i])[None, None, :]
        sh1_l, sc1_l, g1_l, sh2_l, sc2_l, g2_l = jnp.split(mod_l, N_MOD, axis=-1)
        sh1_c, sc1_c, g1_c, sh2_c, sc2_c, g2_c = jnp.split(mod_c, N_MOD, axis=-1)

        hl = to_scan_order(rmsnorm(xl, norm_g[i, 0]) * (1.0 + sc1_l) + sh1_l, col_major)
        hc = rmsnorm(xc, norm_g[i, 0]) * (1.0 + sc1_c) + sh1_c
        proj = jnp.concatenate([hc, hl], axis=1) @ w_in[i]
        u, z, xbc, dt_raw = proj[..., :s1], proj[..., s1:s2], proj[..., s2:s3], proj[..., s3:]
        y_s5 = s5_mixer(u, lc, s5_lam_re[i], s5_lam_im[i], s5_log_step[i], s5_b_re[i], s5_b_im[i],
                        s5_c_re[i], s5_c_im[i], s5_d[i], s5_w_glu[i], s5_b_glu[i])
        y_ssd = ssd_mixer(z, xbc, dt_raw, lc, ssd_conv_w[i], ssd_conv_b[i], ssd_dt_bias[i],
                          ssd_a_log[i], ssd_d[i], ssd_norm[i])
        mix = jnp.concatenate([y_s5, y_ssd], axis=-1)
        yl = from_scan_order(mix[:, lc:], col_major) @ w_out[i]
        xl = xl + g1_l * rmsnorm(yl, norm_g[i, 1])

        hl2 = rmsnorm(xl, norm_g[i, 2]) * (1.0 + sc2_l) + sh2_l
        yl2 = expert_choice_ffn(hl2, moe_router[i], moe_w_gate[i], moe_w_up[i], moe_w_down[i])
        xl = xl + g2_l * rmsnorm(yl2, norm_g[i, 3])

        if not last:
            yc = mix[:, :lc] @ w_out[i]
            xc = xc + g1_c * rmsnorm(yc, norm_g[i, 1])
            hc2 = rmsnorm(xc, norm_g[i, 2]) * (1.0 + sc2_c) + sh2_c
            yc2 = expert_choice_ffn(hc2, moe_router[i], moe_w_gate[i], moe_w_up[i], moe_w_down[i])
            xc = xc + g2_c * rmsnorm(yc2, norm_g[i, 3])
    return xl
```

```python
import functools
import math

import jax
import jax.numpy as jnp
from jax import lax
from jax.experimental import pallas as pl
from jax.experimental.pallas import tpu as pltpu

D_MODEL = 2048
DEPTH = 4
GRID_W = 64
EPS = 1e-6
N_MOD = 6

S5_WIDTH = 1024
S5_GROUP = 16
S5_GROUPS = S5_WIDTH // S5_GROUP
S5_STATE = 64

SSD_WIDTH = 1024
SSD_HEAD_DIM = 64
SSD_HEADS = SSD_WIDTH // SSD_HEAD_DIM
SSD_GROUPS = 2
SSD_STATE = 128
SSD_CONV = 5
SSD_CHUNK = 128
SSD_XBC = SSD_WIDTH + 2 * SSD_GROUPS * SSD_STATE

N_EXPERTS = 16
CAPACITY_FACTOR = 2
D_FF = 1536

F32 = jnp.float32
BF16 = jnp.bfloat16

ROW_TILE = 256
MOE_F_TILE = 256
MOE_ROW_BLOCK = 256
VMEM_LIMIT = 56 * 1024 * 1024


def _cparams(*sem):
    return pltpu.CompilerParams(dimension_semantics=sem, vmem_limit_bytes=VMEM_LIMIT)


def _adaln_kernel(c_ref, w_ref, b_ref, o_ref):
    c = c_ref[...]
    a = (c * jax.nn.sigmoid(c)).astype(BF16)
    o_ref[0] = jnp.dot(a, w_ref[0].astype(BF16), preferred_element_type=F32) + b_ref[0]


def adaln_all(cs, ada_w, ada_b):
    depth, d, n = ada_w.shape
    tn = 1024
    return pl.pallas_call(
        _adaln_kernel,
        grid=(depth, n // tn),
        in_specs=[pl.BlockSpec((8, d), lambda l, j: (0, 0)),
                  pl.BlockSpec((1, d, tn), lambda l, j: (l, 0, j)),
                  pl.BlockSpec((1, 1, tn), lambda l, j: (l, 0, j))],
        out_specs=pl.BlockSpec((1, 8, tn), lambda l, j: (l, 0, j)),
        out_shape=jax.ShapeDtypeStruct((depth, 8, n), F32),
        compiler_params=_cparams("arbitrary", "arbitrary"),
        name="adaln",
    )(cs, ada_w, ada_b.reshape(depth, 1, n))


def _mod_row(t, b):
    return jnp.where(t == 0, 2, b)


def _rms(x):
    return x * lax.rsqrt(jnp.mean(x * x, axis=-1, keepdims=True) + EPS)


def _inproj_kernel(xc_ref, xl_ref, g_ref, sc_ref, sh_ref, wu_ref, wz_ref, wx_ref, wd_ref,
                   u_ref, z_ref, xbc_ref, dt_ref):
    b, t = pl.program_id(0), pl.program_id(1)
    x = jnp.where(t == 0, xc_ref[0], xl_ref[0])
    row = _mod_row(t, b)
    h = _rms(x) * g_ref[...]
    h = (h * (1.0 + sc_ref[pl.ds(row, 1), :]) + sh_ref[pl.ds(row, 1), :]).astype(BF16)
    u_ref[0] = jnp.dot(h, wu_ref[...], preferred_element_type=F32)
    z_ref[0] = jnp.dot(h, wz_ref[...], preferred_element_type=F32)
    xbc_ref[0] = jnp.dot(h, wx_ref[...], preferred_element_type=F32)
    dt_ref[0] = jnp.dot(h, wd_ref[...], preferred_element_type=F32)


def _pair_specs(lc, d):
    assert lc == ROW_TILE
    return [pl.BlockSpec((1, ROW_TILE, d), lambda b, t: (b, 0, 0)),
            pl.BlockSpec((1, ROW_TILE, d), lambda b, t: (b, jnp.maximum(t - 1, 0), 0))]


def _const_spec(shape):
    return pl.BlockSpec(shape, lambda b, t: (0,) * len(shape), pipeline_mode=pl.Buffered(1))


def in_proj(xc, xl, g, sc, sh, w_in):
    bsz, lc, d = xc.shape
    n = xl.shape[1]
    nt = (lc + n) // ROW_TILE
    s1, s2, s3 = S5_WIDTH, S5_WIDTH + SSD_WIDTH, S5_WIDTH + SSD_WIDTH + SSD_XBC
    w = w_in.astype(BF16)
    ws = [w[:, :s1], w[:, s1:s2], w[:, s2:s3], w[:, s3:]]
    widths = [x.shape[1] for x in ws]
    return pl.pallas_call(
        _inproj_kernel,
        grid=(bsz, nt),
        in_specs=_pair_specs(lc, d) + [_const_spec((1, d)), _const_spec((8, d)), _const_spec((8, d))]
        + [_const_spec((d, wd)) for wd in widths],
        out_specs=[pl.BlockSpec((1, ROW_TILE, wd), lambda b, t: (b, t, 0)) for wd in widths],
        out_shape=[jax.ShapeDtypeStruct((bsz, lc + n, wd), F32) for wd in widths],
        compiler_params=_cparams("arbitrary", "arbitrary"),
        name="in_proj",
    )(xc, xl, g.reshape(1, d), sc, sh, *ws)


def _glu_kernel(y_ref, wa_ref, wb_ref, ba_ref, bb_ref, o_ref):
    y = y_ref[0].astype(BF16)
    a = jnp.dot(y, wa_ref[...], preferred_element_type=F32) + ba_ref[...]
    g = jnp.dot(y, wb_ref[...], preferred_element_type=F32) + bb_ref[...]
    o_ref[0] = a * jax.nn.sigmoid(g)


def s5_glu(y, w_glu, b_glu):
    bsz, t, w = y.shape
    wb = w_glu.astype(BF16)
    return pl.pallas_call(
        _glu_kernel,
        grid=(bsz, t // ROW_TILE),
        in_specs=[pl.BlockSpec((1, ROW_TILE, w), lambda b, i: (b, i, 0)),
                  _const_spec((w, w)), _const_spec((w, w)), _const_spec((1, w)), _const_spec((1, w))],
        out_specs=pl.BlockSpec((1, ROW_TILE, w), lambda b, i: (b, i, 0)),
        out_shape=jax.ShapeDtypeStruct((bsz, t, w), F32),
        compiler_params=_cparams("arbitrary", "arbitrary"),
        name="s5_glu",
    )(y, wb[:, :w], wb[:, w:], b_glu[:w].reshape(1, w), b_glu[w:].reshape(1, w))


def _outproj_kernel(ys_ref, yd_ref, xc_ref, xl_ref, wt_ref, wb_ref, g1_ref, gate_ref, g2_ref, sc_ref, sh_ref,
                    wr_ref, xlo_ref, xco_ref, h2_ref, lg_ref):
    b, t = pl.program_id(0), pl.program_id(1)
    row = _mod_row(t, b)
    y = (jnp.dot(ys_ref[0].astype(BF16), wt_ref[...], preferred_element_type=F32)
         + jnp.dot(yd_ref[0].astype(BF16), wb_ref[...], preferred_element_type=F32))
    x = jnp.where(t == 0, xc_ref[0], xl_ref[0])
    xn = x + gate_ref[pl.ds(row, 1), :] * (_rms(y) * g1_ref[...])

    @pl.when(t == 0)
    def _():
        xco_ref[0] = xn

    @pl.when(t > 0)
    def _():
        xlo_ref[0] = xn

    h2 = _rms(xn) * g2_ref[...]
    h2 = h2 * (1.0 + sc_ref[pl.ds(row, 1), :]) + sh_ref[pl.ds(row, 1), :]
    h2_ref[0] = h2
    lg_ref[0] = jnp.dot(h2, wr_ref[...], preferred_element_type=F32, precision=lax.Precision.HIGHEST)


def out_proj(y_s5, y_ssd, xc, xl, w_out, g1, gate1, g2, sc2, sh2, w_router):
    bsz, lc, d = xc.shape
    n = xl.shape[1]
    nt = (lc + n) // ROW_TILE
    w = w_out.astype(BF16)
    hw = S5_WIDTH
    tile = lambda wd: pl.BlockSpec((1, ROW_TILE, wd), lambda b, t: (b, t, 0))
    pair_out = [pl.BlockSpec((1, ROW_TILE, d), lambda b, t: (b, jnp.maximum(t - 1, 0), 0)),
                pl.BlockSpec((1, ROW_TILE, d), lambda b, t: (b, 0, 0))]
    return pl.pallas_call(
        _outproj_kernel,
        grid=(bsz, nt),
        in_specs=[tile(hw), tile(SSD_WIDTH)] + _pair_specs(lc, d)
        + [_const_spec((hw, d)), _const_spec((SSD_WIDTH, d)), _const_spec((1, d)), _const_spec((8, d)),
           _const_spec((1, d)), _const_spec((8, d)), _const_spec((8, d)), _const_spec((d, N_EXPERTS))],
        out_specs=pair_out + [tile(d), tile(N_EXPERTS)],
        out_shape=[jax.ShapeDtypeStruct((bsz, n, d), F32), jax.ShapeDtypeStruct((bsz, lc, d), F32),
                   jax.ShapeDtypeStruct((bsz, lc + n, d), F32),
                   jax.ShapeDtypeStruct((bsz, lc + n, N_EXPERTS), F32)],
        compiler_params=_cparams("arbitrary", "arbitrary"),
        name="out_proj",
    )(y_s5, y_ssd, xc, xl, w[:hw], w[hw:], g1.reshape(1, d), gate1, g2.reshape(1, d), sc2, sh2, w_router)


def _moe_kernel(x_ref, gate_ref, wg_ref, wu_ref, wd_ref, o_ref, wg_s, wu_s, wd_s):
    f = pl.program_id(1)
    nf = pl.num_programs(1)
    wg_s[...] = wg_ref[0].astype(BF16)
    wu_s[...] = wu_ref[0].astype(BF16)
    wd_s[...] = wd_ref[0].astype(BF16)
    rows = x_ref.shape[1]
    for r0 in range(0, rows, MOE_ROW_BLOCK):
        rb = min(MOE_ROW_BLOCK, rows - r0)
        x = x_ref[0, r0:r0 + rb, :]
        g = jnp.dot(x, wg_s[...], preferred_element_type=F32)
        u = jnp.dot(x, wu_s[...], preferred_element_type=F32)
        h = ((g * jax.nn.sigmoid(g)) * u).astype(BF16)
        part = jnp.dot(h, wd_s[...], preferred_element_type=F32)

        @pl.when(f == 0)
        def _():
            o_ref[0, r0:r0 + rb, :] = part

        @pl.when(jnp.logical_and(f > 0, f < nf - 1))
        def _():
            o_ref[0, r0:r0 + rb, :] += part

        @pl.when(f == nf - 1)
        def _():
            o_ref[0, r0:r0 + rb, :] = (o_ref[0, r0:r0 + rb, :] + part) * gate_ref[0, r0:r0 + rb, :]


def moe_ffn(xs, gate, w_gate, w_up, w_down):
    e, r, d = xs.shape
    ff = w_gate.shape[2]
    tf = MOE_F_TILE
    return pl.pallas_call(
        _moe_kernel,
        grid=(e, ff // tf),
        in_specs=[pl.BlockSpec((1, r, d), lambda i, f: (i, 0, 0)),
                  pl.BlockSpec((1, r, 1), lambda i, f: (i, 0, 0)),
                  pl.BlockSpec((1, d, tf), lambda i, f: (i, 0, f)),
                  pl.BlockSpec((1, d, tf), lambda i, f: (i, 0, f)),
                  pl.BlockSpec((1, tf, d), lambda i, f: (i, f, 0))],
        out_specs=pl.BlockSpec((1, r, d), lambda i, f: (i, 0, 0)),
        out_shape=jax.ShapeDtypeStruct((e, r, d), F32),
        scratch_shapes=[pltpu.VMEM((d, tf), BF16), pltpu.VMEM((d, tf), BF16), pltpu.VMEM((tf, d), BF16)],
        compiler_params=_cparams("arbitrary", "arbitrary"),
        name="moe_ffn",
    )(xs, gate, w_gate, w_up, w_down)


def rmsnorm(x, w):
    xf = x.astype(F32)
    y = xf * lax.rsqrt(jnp.mean(xf * xf, axis=-1, keepdims=True) + EPS)
    return (y * w.astype(F32)).astype(x.dtype)


def grid_transpose(x):
    b, n = x.shape[:2]
    rows = n // GRID_W
    return x.reshape((b, rows, GRID_W) + x.shape[2:]).swapaxes(1, 2).reshape(x.shape)


def dir_seq(a, lc, reverse):
    if not reverse:
        return a
    return jnp.concatenate([jnp.flip(a[:, :lc], 1), jnp.flip(a[:, lc:], 1)], axis=1)


def dwconv_centred(x, w, b):
    pad = w.shape[0] // 2
    y = lax.conv_general_dilated(x, w[:, None, :].astype(x.dtype), window_strides=(1,),
                                 padding=((pad, pad),), dimension_numbers=('NWC', 'WIO', 'NWC'),
                                 feature_group_count=x.shape[-1])
    return y + b.astype(x.dtype)


def s5_discretise(lam_re, lam_im, log_step, b_re, b_im):
    step = jnp.exp(log_step)[:, None]
    mag = jnp.exp(lam_re * step)
    ang = lam_im * step
    lb_re, lb_im = mag * jnp.cos(ang), mag * jnp.sin(ang)
    den = lam_re * lam_re + lam_im * lam_im
    nr = lb_re - 1.0
    f_re = (nr * lam_re + lb_im * lam_im) / den
    f_im = (lb_im * lam_re - nr * lam_im) / den
    bb_re = f_re[..., None] * b_re - f_im[..., None] * b_im
    bb_im = f_re[..., None] * b_im + f_im[..., None] * b_re
    return lb_re, lb_im, bb_re, bb_im


def _complex_affine_combine(e1, e2):
    a1r, a1i, b1r, b1i = e1
    a2r, a2i, b2r, b2i = e2
    return (a2r * a1r - a2i * a1i, a2r * a1i + a2i * a1r,
            a2r * b1r - a2i * b1i + b2r, a2r * b1i + a2i * b1r + b2i)


def s5_states(u, lb_re, lb_im, bb_re, bb_im):
    bu_re = jnp.einsum('btgh,gph->btgp', u, bb_re)
    bu_im = jnp.einsum('btgh,gph->btgp', u, bb_im)
    a_re = jnp.broadcast_to(lb_re, bu_re.shape)
    a_im = jnp.broadcast_to(lb_im, bu_im.shape)
    _, _, h_re, h_im = lax.associative_scan(_complex_affine_combine, (a_re, a_im, bu_re, bu_im), axis=1)
    return h_re, h_im


def s5_core(u, lc, lam_re, lam_im, log_step, b_re, b_im, c_re, c_im, d_skip):
    bsz, t, _ = u.shape
    uf = u.reshape(bsz, t, S5_GROUPS, S5_GROUP)
    h_re, h_im = 0.0, 0.0
    for d in range(2):
        rev = d == 1
        lb_re, lb_im, bb_re, bb_im = s5_discretise(lam_re[d], lam_im[d], log_step[d], b_re, b_im)
        s_re, s_im = s5_states(dir_seq(uf, lc, rev), lb_re, lb_im, bb_re, bb_im)
        h_re = h_re + dir_seq(s_re, lc, rev)
        h_im = h_im + dir_seq(s_im, lc, rev)
    y = (jnp.einsum('btgp,ghp->btgh', h_re, c_re) - jnp.einsum('btgp,ghp->btgh', h_im, c_im)
         + d_skip.reshape(S5_GROUPS, S5_GROUP) * uf)
    return jax.nn.gelu(y.reshape(bsz, t, S5_WIDTH))


def ssd_chunked(x, dt, a, bm, cm):
    b, t, h, p = x.shape
    g, n = bm.shape[2], bm.shape[3]
    j = h // g
    nc, l = t // SSD_CHUNK, SSD_CHUNK
    xd = (x * dt[..., None]).reshape(b, nc, l, g, j, p)
    bm = bm.reshape(b, nc, l, g, n)
    cm = cm.reshape(b, nc, l, g, n)
    cum = jnp.cumsum((dt * a).reshape(b, nc, l, g, j), axis=2)
    seg = cum[:, :, :, None] - cum[:, :, None, :]
    lower = jnp.tril(jnp.ones((l, l), dtype=bool))[:, :, None, None]
    decay = jnp.exp(jnp.where(lower, seg, -jnp.inf))
    scores = jnp.einsum('bclgn,bcsgn->bclsg', cm, bm)
    y_diag = jnp.einsum('bclsg,bclsgj,bcsgjp->bclgjp', scores, decay, xd)
    end_decay = jnp.exp(cum[:, :, -1:] - cum)
    chunk_states = jnp.einsum('bclgn,bclgj,bclgjp->bcgjpn', bm, end_decay, xd)
    chunk_decay = jnp.exp(cum[:, :, -1])

    def carry_step(state, inp):
        st, dec = inp
        return dec[..., None, None] * state + st, state

    h0 = jnp.zeros((b, g, j, p, n), x.dtype)
    _, h_in = lax.scan(carry_step, h0, (jnp.moveaxis(chunk_states, 1, 0), jnp.moveaxis(chunk_decay, 1, 0)))
    h_in = jnp.moveaxis(h_in, 0, 1)
    y_off = jnp.einsum('bclgn,bcgjpn,bclgj->bclgjp', cm, h_in, jnp.exp(cum))
    return (y_diag + y_off).reshape(b, t, h, p)


def ssd_mixer(z, xbc, dt_raw, lc, conv_w, conv_b, dt_bias, a_log, d_skip, norm_w):
    bsz, t, _ = xbc.shape
    xbc = jnp.concatenate([dwconv_centred(xbc[:, :lc], conv_w, conv_b),
                           dwconv_centred(xbc[:, lc:], conv_w, conv_b)], axis=1)
    xbc = jax.nn.silu(xbc)
    gn = SSD_GROUPS * SSD_STATE
    xs = xbc[..., :SSD_WIDTH].reshape(bsz, t, SSD_HEADS, SSD_HEAD_DIM)
    bm = xbc[..., SSD_WIDTH:SSD_WIDTH + gn].reshape(bsz, t, SSD_GROUPS, SSD_STATE)
    cm = xbc[..., SSD_WIDTH + gn:].reshape(bsz, t, SSD_GROUPS, SSD_STATE)
    dt_raw = dt_raw.reshape(bsz, t, 2, SSD_HEADS)
    y = d_skip[:, None] * xs
    for d in range(2):
        rev = d == 1
        dt = jax.nn.softplus(dt_raw[:, :, d] + dt_bias[d])
        a = -jnp.exp(a_log[d])
        y_d = ssd_chunked(dir_seq(xs, lc, rev), dir_seq(dt, lc, rev), a,
                          dir_seq(bm, lc, rev), dir_seq(cm, lc, rev))
        y = y + dir_seq(y_d, lc, rev)
    y = y.reshape(bsz, t, SSD_WIDTH)
    return rmsnorm(y * jax.nn.silu(z), norm_w)


def expert_choice(h2, logits, lc, w_gate, w_up, w_down, with_ctx):
    bsz, t, d = h2.shape
    e = N_EXPERTS
    parts = [(lc, t)] + ([(0, lc)] if with_ctx else [])
    xs_l, gate_l, idx_l = [], [], []
    for lo, hi in parts:
        n = hi - lo
        cap = CAPACITY_FACTOR * n // e
        aff = jax.nn.softmax(logits[:, lo:hi], axis=-1)
        gate, idx = lax.top_k(jnp.swapaxes(aff, 1, 2), cap)
        xs = jax.vmap(lambda xb, ib: xb[ib])(h2[:, lo:hi], idx)
        xs_l.append(jnp.swapaxes(xs, 0, 1).reshape(e, bsz * cap, d))
        gate_l.append(jnp.swapaxes(gate, 0, 1).reshape(e, bsz * cap))
        idx_l.append(idx)
    xs = jnp.concatenate(xs_l, axis=1).astype(BF16)
    gate = jnp.concatenate(gate_l, axis=1)[..., None]
    out = moe_ffn(xs, gate, w_gate, w_up, w_down)
    res, r0 = [], 0
    for (lo, hi), idx in zip(parts, idx_l):
        n = hi - lo
        cap = idx.shape[-1]
        ob = jnp.swapaxes(out[:, r0:r0 + bsz * cap].reshape(e, bsz, cap, d), 0, 1)
        r0 += bsz * cap
        res.append(jax.vmap(lambda o, ib: jnp.zeros((n, d), o.dtype).at[ib.reshape(-1)].add(o.reshape(-1, d)))(ob, idx))
    return res


def kernel(x, c, ctx, c_ctx, ada_w, ada_b, norm_g, w_in, w_out, s5_lam_re, s5_lam_im, s5_log_step,
           s5_b_re, s5_b_im, s5_c_re, s5_c_im, s5_d, s5_w_glu, s5_b_glu, ssd_conv_w, ssd_conv_b,
           ssd_dt_bias, ssd_a_log, ssd_d, ssd_norm, moe_router, moe_w_gate, moe_w_up, moe_w_down):
    bsz, n, d = x.shape
    lc = ctx.shape[1]
    cs = jnp.zeros((8, d), F32).at[:bsz].set(c).at[2].set(c_ctx)
    mods = adaln_all(cs, ada_w, ada_b)
    xl, xc = x, ctx
    for i in range(DEPTH):
        col_major = i % 2 == 1
        last = i == DEPTH - 1
        sh1, sc1, g1, sh2, sc2, g2 = [mods[i, :, k * d:(k + 1) * d] for k in range(N_MOD)]
        if col_major:
            xl = grid_transpose(xl)
        u, z, xbc, dt_raw = in_proj(xc, xl, norm_g[i, 0], sc1, sh1, w_in[i])
        y5 = s5_core(u, lc, s5_lam_re[i], s5_lam_im[i], s5_log_step[i], s5_b_re[i], s5_b_im[i],
                     s5_c_re[i], s5_c_im[i], s5_d[i])
        y_s5 = s5_glu(y5, s5_w_glu[i], s5_b_glu[i])
        y_ssd = ssd_mixer(z, xbc, dt_raw, lc, ssd_conv_w[i], ssd_conv_b[i], ssd_dt_bias[i],
                          ssd_a_log[i], ssd_d[i], ssd_norm[i])
        xl, xc, h2, logits = out_proj(y_s5, y_ssd, xc, xl, w_out[i], norm_g[i, 1], g1, norm_g[i, 2],
                                      sc2, sh2, moe_router[i])
        res = expert_choice(h2, logits, lc, moe_w_gate[i], moe_w_up[i], moe_w_down[i], not last)
        xl = xl + g2[:bsz, None, :] * rmsnorm(res[0], norm_g[i, 3])
        if not last:
            xc = xc + g2[2][None, None, :] * rmsnorm(res[1], norm_g[i, 3])
        if col_major:
            xl = grid_transpose(xl)
    return xl
```

```python
import functools
import math

import jax
import jax.numpy as jnp
from jax import lax
from jax.experimental import pallas as pl
from jax.experimental.pallas import tpu as pltpu

D_MODEL = 2048
DEPTH = 4
GRID_W = 64
EPS = 1e-6
N_MOD = 6

S5_WIDTH = 1024
S5_GROUP = 16
S5_GROUPS = S5_WIDTH // S5_GROUP
S5_STATE = 64

SSD_WIDTH = 1024
SSD_HEAD_DIM = 64
SSD_HEADS = SSD_WIDTH // SSD_HEAD_DIM
SSD_GROUPS = 2
SSD_STATE = 128
SSD_CONV = 5
SSD_XBC = SSD_WIDTH + 2 * SSD_GROUPS * SSD_STATE

N_EXPERTS = 16
CAPACITY_FACTOR = 2
D_FF = 1536

F32 = jnp.float32
BF16 = jnp.bfloat16
HIGHEST = lax.Precision.HIGHEST

LANES = 128
SUBLANES = 8
ROW_TILE = 256
MOE_F_TILE = 256
MOE_ROW_BLOCK = 256
S5_BLOCK = 16
S5_PAIRS = S5_GROUPS // 2
SSD_CHUNK = 128
VMEM_LIMIT = 56 * 1024 * 1024


def _cparams(*sem):
    return pltpu.CompilerParams(dimension_semantics=sem, vmem_limit_bytes=VMEM_LIMIT)


def _adaln_kernel(c_ref, w_ref, b_ref, o_ref):
    c = c_ref[...]
    a = (c * jax.nn.sigmoid(c)).astype(BF16)
    o_ref[0] = jnp.dot(a, w_ref[0].astype(BF16), preferred_element_type=F32) + b_ref[0]


def adaln_all(cs, ada_w, ada_b):
    depth, d, n = ada_w.shape
    tn = 1024
    return pl.pallas_call(
        _adaln_kernel,
        grid=(depth, n // tn),
        in_specs=[pl.BlockSpec((8, d), lambda l, j: (0, 0)),
                  pl.BlockSpec((1, d, tn), lambda l, j: (l, 0, j)),
                  pl.BlockSpec((1, 1, tn), lambda l, j: (l, 0, j))],
        out_specs=pl.BlockSpec((1, 8, tn), lambda l, j: (l, 0, j)),
        out_shape=jax.ShapeDtypeStruct((depth, 8, n), F32),
        compiler_params=_cparams("arbitrary", "arbitrary"),
        name="adaln",
    )(cs, ada_w, ada_b.reshape(depth, 1, n))


def _mod_row(t, b):
    return jnp.where(t == 0, 2, b)


def _rms(x):
    return x * lax.rsqrt(jnp.mean(x * x, axis=-1, keepdims=True) + EPS)


def _inproj_kernel(xc_ref, xl_ref, g_ref, sc_ref, sh_ref, wu_ref, wz_ref, wx_ref, wf_ref, wb_ref,
                   u_ref, z_ref, xbc_ref, dtf_ref, dtb_ref):
    b, t = pl.program_id(0), pl.program_id(1)
    x = jnp.where(t == 0, xc_ref[0], xl_ref[0])
    row = _mod_row(t, b)
    h = _rms(x) * g_ref[...]
    h = (h * (1.0 + sc_ref[pl.ds(row, 1), :]) + sh_ref[pl.ds(row, 1), :]).astype(BF16)
    u_ref[0] = jnp.dot(h, wu_ref[...], preferred_element_type=F32).astype(BF16)
    z_ref[0] = jnp.dot(h, wz_ref[...], preferred_element_type=F32)
    xbc_ref[0] = jnp.dot(h, wx_ref[...], preferred_element_type=F32)
    dtf_ref[0] = jnp.dot(h, wf_ref[...], preferred_element_type=F32)
    dtb_ref[0] = jnp.dot(h, wb_ref[...], preferred_element_type=F32)


def _pair_specs(lc, d):
    assert lc == ROW_TILE
    return [pl.BlockSpec((1, ROW_TILE, d), lambda b, t: (b, 0, 0)),
            pl.BlockSpec((1, ROW_TILE, d), lambda b, t: (b, jnp.maximum(t - 1, 0), 0))]


def _const_spec(shape):
    return pl.BlockSpec(shape, lambda *_: (0,) * len(shape), pipeline_mode=pl.Buffered(1))


def in_proj(xc, xl, g, sc, sh, w_in):
    bsz, lc, d = xc.shape
    n = xl.shape[1]
    nt = (lc + n) // ROW_TILE
    s1, s2, s3 = S5_WIDTH, S5_WIDTH + SSD_WIDTH, S5_WIDTH + SSD_WIDTH + SSD_XBC
    w = w_in.astype(BF16)
    ws = [w[:, :s1], w[:, s1:s2], w[:, s2:s3], w[:, s3:s3 + SSD_HEADS], w[:, s3 + SSD_HEADS:]]
    widths = [x.shape[1] for x in ws]
    dtypes = [BF16, F32, F32, F32, F32]
    return pl.pallas_call(
        _inproj_kernel,
        grid=(bsz, nt),
        in_specs=_pair_specs(lc, d) + [_const_spec((1, d)), _const_spec((8, d)), _const_spec((8, d))]
        + [_const_spec((d, wd)) for wd in widths],
        out_specs=[pl.BlockSpec((1, ROW_TILE, wd), lambda b, t: (b, t, 0)) for wd in widths],
        out_shape=[jax.ShapeDtypeStruct((bsz, lc + n, wd), dt) for wd, dt in zip(widths, dtypes)],
        compiler_params=_cparams("arbitrary", "arbitrary"),
        name="in_proj",
    )(xc, xl, g.reshape(1, d), sc, sh, *ws)


def s5_operators(lam_re, lam_im, log_step, b_re, b_im, c_re, c_im, d_skip, seg_ctx, seg_lat):
    g_, p_, h_, lk = S5_GROUPS, S5_STATE, S5_GROUP, S5_BLOCK
    j = jnp.arange(lk + 1, dtype=F32)[:, None, None]
    eye2 = jnp.eye(2, dtype=F32)
    k_dir, bst_dir, coff_dir, lam_rows = [], [], [], []
    for d in range(2):
        step = jnp.exp(log_step[d])[:, None]
        e_re, ang = lam_re[d] * step, lam_im[d] * step
        pr = jnp.exp(j * e_re) * jnp.cos(j * ang)
        pi = jnp.exp(j * e_re) * jnp.sin(j * ang)
        den = lam_re[d] * lam_re[d] + lam_im[d] * lam_im[d]
        nr = pr[1] - 1.0
        f_re = (nr * lam_re[d] + pi[1] * lam_im[d]) / den
        f_im = (pi[1] * lam_re[d] - nr * lam_im[d]) / den
        bb_re = f_re[..., None] * b_re - f_im[..., None] * b_im
        bb_im = f_re[..., None] * b_im + f_im[..., None] * b_re
        w_re = pr[:lk, :, :, None] * bb_re - pi[:lk, :, :, None] * bb_im
        w_im = pr[:lk, :, :, None] * bb_im + pi[:lk, :, :, None] * bb_re
        k_dir.append(jnp.einsum('gop,jgph->jgoh', c_re, w_re, precision=HIGHEST)
                     - jnp.einsum('gop,jgph->jgoh', c_im, w_im, precision=HIGHEST))
        order = slice(None, None, -1) if d == 0 else slice(None)
        bst_dir.append(jnp.stack([w_re[order], w_im[order]], axis=0).transpose(2, 1, 4, 0, 3))
        kk = jnp.arange(1, lk + 1) if d == 0 else jnp.arange(lk, 0, -1)
        ar, ai = pr[kk], pi[kk]
        cr = jnp.einsum('gop,tgp->gpto', c_re, ar) - jnp.einsum('gop,tgp->gpto', c_im, ai)
        ci = -(jnp.einsum('gop,tgp->gpto', c_re, ai) + jnp.einsum('gop,tgp->gpto', c_im, ar))
        coff_dir.append(jnp.stack([cr, ci], axis=1))
        for n_pow in (lk, seg_ctx, seg_lat):
            lam_rows += [jnp.exp(n_pow * e_re) * jnp.cos(n_pow * ang), jnp.exp(n_pow * e_re) * jnp.sin(n_pow * ang)]
    s_i, t_i = jnp.arange(lk)[:, None], jnp.arange(lk)[None, :]
    kf = jnp.where((t_i >= s_i)[..., None, None, None], k_dir[0][jnp.clip(t_i - s_i, 0, lk - 1)], 0.0)
    kb = jnp.where((s_i >= t_i)[..., None, None, None], k_dir[1][jnp.clip(s_i - t_i, 0, lk - 1)], 0.0)
    dk = (jnp.eye(lk, dtype=F32)[:, :, None, None, None] * jnp.eye(h_, dtype=F32)[None, None, None]
          * d_skip.reshape(g_, h_)[None, None, :, :, None])
    tz = (kf + kb + dk).transpose(2, 0, 4, 1, 3).reshape(S5_PAIRS, 2, lk * h_, lk * h_)
    bst = jnp.stack(bst_dir, axis=3).reshape(S5_PAIRS, 2, lk, h_, 2, 2, p_)
    bst = jnp.einsum('aishdqp,ij->aishdqjp', bst, eye2).reshape(S5_PAIRS, 2 * lk * h_, 8 * p_)
    coff = jnp.stack(coff_dir, axis=1).reshape(S5_PAIRS, 2, 2, 2, p_, lk, h_)
    coff = jnp.einsum('aidqpto,ij->adqipjto', coff, eye2).reshape(S5_PAIRS, 8 * p_, 2 * lk * h_)
    lam = jnp.stack([r.reshape(S5_PAIRS, 2 * p_) for r in lam_rows], axis=1)
    order12 = jnp.array([0, 1, 6, 7, 2, 3, 8, 9, 4, 5, 10, 11])
    lam = jnp.concatenate([lam[:, order12], jnp.zeros((S5_PAIRS, 4, 2 * p_), F32)], axis=1)
    return tz.astype(BF16), bst.astype(BF16), coff.astype(BF16), lam


def _s5_kernel(u_ref, tz_ref, bst_ref, coff_ref, lam_ref, y_ref, x_s, h_s, t_s, g_s, *, n_lat, n_ctx, bsz):
    nseg = SUBLANES // bsz
    u = u_ref[0]
    x_s[...] = jnp.dot(u, bst_ref[0], preferred_element_type=F32)
    lam = lam_ref[0]

    def scan(base, nsteps, rev, co, lr, li, init, store):
        def body(k, carry):
            hr, hi = carry
            q = nsteps - 1 - k if rev else k
            r0 = pl.multiple_of(base + q * SUBLANES, SUBLANES)
            if store:
                h_s[pl.ds(r0, SUBLANES), co:co + LANES] = hr
                h_s[pl.ds(r0, SUBLANES), co + LANES:co + 2 * LANES] = hi
            xr = x_s[pl.ds(r0, SUBLANES), co:co + LANES]
            xi = x_s[pl.ds(r0, SUBLANES), co + LANES:co + 2 * LANES]
            return lr * hr - li * hi + xr, lr * hi + li * hr + xi
        return lax.fori_loop(0, nsteps, body, init, unroll=4)

    def seg_carry(tot, lr, li, inits, rev):
        t_s[:, :LANES], t_s[:, LANES:] = tot
        finals = []
        for b in range(bsz):
            gr, gi = inits[b]
            for seg in (range(nseg - 1, -1, -1) if rev else range(nseg)):
                s = b * nseg + seg
                g_s[s:s + 1, :LANES], g_s[s:s + 1, LANES:] = gr, gi
                tr, ti = t_s[s:s + 1, :LANES], t_s[s:s + 1, LANES:]
                gr, gi = lr * gr - li * gi + tr, lr * gi + li * gr + ti
            finals.append((gr, gi))
        return (g_s[:, :LANES], g_s[:, LANES:]), finals

    zero8 = (jnp.zeros((SUBLANES, LANES), F32), jnp.zeros((SUBLANES, LANES), F32))
    zero1 = (jnp.zeros((1, LANES), F32), jnp.zeros((1, LANES), F32))
    for d in range(2):
        rev, co = d == 1, d * 2 * LANES
        lr = jnp.broadcast_to(lam[2 * d:2 * d + 1], (SUBLANES, LANES))
        li = jnp.broadcast_to(lam[2 * d + 1:2 * d + 2], (SUBLANES, LANES))
        init1 = [zero1] * bsz
        for base, nsteps, row in ((n_lat * SUBLANES, n_ctx, 4 + 2 * d), (0, n_lat, 8 + 2 * d)):
            tot = scan(base, nsteps, rev, co, lr, li, zero8, False)
            g0, init1 = seg_carry(tot, lam[row:row + 1], lam[row + 1:row + 2], init1, rev)
            scan(base, nsteps, rev, co, lr, li, g0, True)
    half = S5_BLOCK * S5_GROUP
    y = jnp.dot(h_s[...].astype(BF16), coff_ref[0], preferred_element_type=F32)
    y = y + jnp.concatenate([jnp.dot(u[:, :half], tz_ref[0, 0], preferred_element_type=F32),
                             jnp.dot(u[:, half:], tz_ref[0, 1], preferred_element_type=F32)], axis=1)
    y_ref[0] = jax.nn.gelu(y).astype(BF16)


def _s5_pack(part, nseg):
    bsz, n, _ = part.shape
    steps = n // (nseg * S5_BLOCK)
    p = part.reshape(bsz, nseg, steps, S5_BLOCK, S5_PAIRS, 2, S5_GROUP)
    return p.transpose(4, 2, 0, 1, 5, 3, 6).reshape(S5_PAIRS, steps * bsz * nseg, 2 * S5_BLOCK * S5_GROUP)


def _s5_unpack(rows, bsz, nseg):
    steps = rows.shape[1] // (bsz * nseg)
    p = rows.reshape(S5_PAIRS, steps, bsz, nseg, 2, S5_BLOCK, S5_GROUP)
    return p.transpose(2, 3, 1, 5, 0, 4, 6).reshape(bsz, nseg * steps * S5_BLOCK, S5_WIDTH)


def s5_mix(u, lc, ops):
    bsz, t, _ = u.shape
    assert SUBLANES % bsz == 0
    nseg = SUBLANES // bsz
    tz, bst, coff, lam = ops
    n_lat, n_ctx = (t - lc) // (nseg * S5_BLOCK), lc // (nseg * S5_BLOCK)
    rows = (n_lat + n_ctx) * SUBLANES
    width = 2 * S5_BLOCK * S5_GROUP
    up = jnp.concatenate([_s5_pack(u[:, lc:], nseg), _s5_pack(u[:, :lc], nseg)], axis=1)
    blk = lambda *shape: pl.BlockSpec((1,) + shape, lambda i: (i,) + (0,) * len(shape))
    y = pl.pallas_call(
        functools.partial(_s5_kernel, n_lat=n_lat, n_ctx=n_ctx, bsz=bsz),
        grid=(S5_PAIRS,),
        in_specs=[blk(rows, width), blk(2, width // 2, width // 2), blk(width, 4 * LANES), blk(4 * LANES, width),
                  blk(16, LANES)],
        out_specs=blk(rows, width),
        out_shape=jax.ShapeDtypeStruct((S5_PAIRS, rows, width), BF16),
        scratch_shapes=[pltpu.VMEM((rows, 4 * LANES), F32), pltpu.VMEM((rows, 4 * LANES), F32),
                        pltpu.VMEM((SUBLANES, 2 * LANES), F32), pltpu.VMEM((SUBLANES, 2 * LANES), F32)],
        compiler_params=_cparams("arbitrary"),
        name="s5_mix",
    )(up, tz, bst, coff, lam)
    nl = n_lat * SUBLANES
    return jnp.concatenate([_s5_unpack(y[:, nl:], bsz, nseg), _s5_unpack(y[:, :nl], bsz, nseg)], axis=1)


def _glu_kernel(y_ref, wa_ref, wb_ref, ba_ref, bb_ref, o_ref):
    y = y_ref[0]
    a = jnp.dot(y, wa_ref[...], preferred_element_type=F32) + ba_ref[...]
    g = jnp.dot(y, wb_ref[...], preferred_element_type=F32) + bb_ref[...]
    o_ref[0] = (a * jax.nn.sigmoid(g)).astype(BF16)


def s5_glu(y, w_glu, b_glu):
    bsz, t, w = y.shape
    wb = w_glu.astype(BF16)
    return pl.pallas_call(
        _glu_kernel,
        grid=(bsz, t // ROW_TILE),
        in_specs=[pl.BlockSpec((1, ROW_TILE, w), lambda b, i: (b, i, 0)),
                  _const_spec((w, w)), _const_spec((w, w)), _const_spec((1, w)), _const_spec((1, w))],
        out_specs=pl.BlockSpec((1, ROW_TILE, w), lambda b, i: (b, i, 0)),
        out_shape=jax.ShapeDtypeStruct((bsz, t, w), BF16),
        compiler_params=_cparams("arbitrary", "arbitrary"),
        name="s5_glu",
    )(y, wb[:, :w], wb[:, w:], b_glu[:w].reshape(1, w), b_glu[w:].reshape(1, w))


def _softplus(x):
    return jnp.maximum(x, 0.0) + jnp.log(1.0 + jnp.exp(-jnp.abs(x)))


def _ssd_chunk_id(k, rev, nc, ncc):
    if not rev:
        return k
    return jnp.where(k < ncc, ncc - 1 - k, nc - 1 - (k - ncc))


def _ssd_kernel(*refs, rev, nc, ncc):
    if rev:
        (x_ref, xp_ref, xn_ref, dt_ref, dtt_ref, cw_ref, cb_ref, bias_ref, biast_ref, alog_ref, alogt_ref,
         z_ref, yf_ref, dsk_ref, nw_ref, o_ref, st_s, xp_s, y_s) = refs
    else:
        (x_ref, xp_ref, xn_ref, dt_ref, dtt_ref, cw_ref, cb_ref, bias_ref, biast_ref, alog_ref, alogt_ref,
         o_ref, st_s, xp_s) = refs
    lch = SSD_CHUNK
    k = pl.program_id(1)
    c = _ssd_chunk_id(k, rev, nc, ncc)

    @pl.when(k == 0)
    def _():
        st_s[...] = jnp.zeros(st_s.shape, F32)

    first = jnp.logical_or(c == 0, c == ncc)
    last = jnp.logical_or(c == ncc - 1, c == nc - 1)
    xp_s[0:SUBLANES, :] = jnp.where(first, 0.0, xp_ref[0])
    xp_s[SUBLANES:SUBLANES + lch, :] = x_ref[0]
    xp_s[SUBLANES + lch:2 * SUBLANES + lch, :] = jnp.where(last, 0.0, xn_ref[0])
    acc = cb_ref[...] + cw_ref[0:1, :] * xp_s[SUBLANES - 2:SUBLANES - 2 + lch, :]
    for tap in range(1, SSD_CONV):
        acc = acc + cw_ref[tap:tap + 1, :] * xp_s[SUBLANES - 2 + tap:SUBLANES - 2 + tap + lch, :]
    xc = acc * jax.nn.sigmoid(acc)
    gn = SSD_GROUPS * SSD_STATE
    xs = xc[:, :SSD_WIDTH]
    bm = xc[:, SSD_WIDTH:SSD_WIDTH + gn]
    cm = xc[:, SSD_WIDTH + gn:]

    dt = _softplus(dt_ref[0] + bias_ref[...])
    dtt = _softplus(dtt_ref[0] + biast_ref[...])
    da = dt * -jnp.exp(alog_ref[...])
    dat = dtt * -jnp.exp(alogt_ref[...])
    row_i = lax.broadcasted_iota(jnp.int32, (lch, lch), 0)
    col_i = lax.broadcasted_iota(jnp.int32, (lch, lch), 1)
    tri = (col_i >= row_i) if rev else (col_i <= row_i)
    trit = (row_i >= col_i) if rev else (row_i <= col_i)
    cum = jnp.dot(tri.astype(F32), da, preferred_element_type=F32, precision=HIGHEST)
    cumt = jnp.dot(dat, trit.astype(F32), preferred_element_type=F32, precision=HIGHEST)
    end = 0 if rev else lch - 1
    tot = cum[end:end + 1, :]
    wt = dtt * jnp.exp(cumt[:, end:end + 1] - cumt)
    lane = lax.broadcasted_iota(jnp.int32, (1, LANES), 1)
    low = lane < SSD_HEAD_DIM
    heads_per_group = SSD_HEADS // SSD_GROUPS
    for g in range(SSD_GROUPS):
        bg = bm[:, g * SSD_STATE:(g + 1) * SSD_STATE]
        cg = cm[:, g * SSD_STATE:(g + 1) * SSD_STATE].astype(BF16)
        scores = lax.dot_general(cg, bg.astype(BF16), (((1,), (1,)), ((), ())), preferred_element_type=F32)
        bgt = bg.T
        st = st_s[g]
        yoff = jnp.dot(cg, st.astype(BF16), preferred_element_type=F32)
        for jp in range(heads_per_group // 2):
            ha = g * heads_per_group + 2 * jp
            lo = (ha // 2) * LANES
            sl = jp * LANES
            xsp = xs[:, lo:lo + LANES]
            xblk = jnp.concatenate([jnp.where(low, xsp, 0.0), jnp.where(low, 0.0, xsp)], axis=0).astype(BF16)
            ms, ecols, lhs2 = [], [], []
            for h in (ha, ha + 1):
                col = cum[:, h:h + 1]
                seg = col - cumt[h:h + 1, :]
                dec = jnp.exp(jnp.where(tri, seg, -jnp.inf))
                ms.append((scores * dec * dtt[h:h + 1, :]).astype(BF16))
                ecols.append(jnp.exp(col))
                lhs2.append((bgt * wt[h:h + 1, :]).astype(BF16))
            ydiag = jnp.dot(jnp.concatenate(ms, axis=1), xblk, preferred_element_type=F32)
            y_pair = ydiag + jnp.where(low, ecols[0], ecols[1]) * yoff[:, sl:sl + LANES]
            upd = jnp.dot(jnp.concatenate(lhs2, axis=1), xblk, preferred_element_type=F32)
            cd = jnp.where(low, jnp.exp(tot[:, ha:ha + 1]), jnp.exp(tot[:, ha + 1:ha + 2]))
            st_s[g, :, sl:sl + LANES] = cd * st[:, sl:sl + LANES] + upd
            if rev:
                y_s[:, lo:lo + LANES] = y_pair
            else:
                o_ref[0, :, lo:lo + LANES] = y_pair
    if rev:
        y = yf_ref[0] + y_s[...] + dsk_ref[...] * xs
        z = z_ref[0]
        o_ref[0] = (_rms(y * (z * jax.nn.sigmoid(z))) * nw_ref[...]).astype(BF16)


def ssd_mix(z, xbc, dtf, dtb, lc, conv_w, conv_b, dt_bias, a_log, d_skip, norm_w):
    bsz, t, wx = xbc.shape
    lch = SSD_CHUNK
    nc, ncc = t // lch, lc // lch
    hb = lch // SUBLANES
    dsk = jnp.repeat(d_skip, SSD_HEAD_DIM).reshape(1, SSD_WIDTH)
    yf = None
    for rev in (False, True):
        cid = functools.partial(_ssd_chunk_id, rev=rev, nc=nc, ncc=ncc)
        d = int(rev)
        dt = dtb if rev else dtf
        row = lambda wd: pl.BlockSpec((1, lch, wd), lambda b, k: (b, cid(k), 0))
        in_specs = [row(wx),
                    pl.BlockSpec((1, SUBLANES, wx), lambda b, k: (b, jnp.maximum(cid(k) * hb - 1, 0), 0)),
                    pl.BlockSpec((1, SUBLANES, wx), lambda b, k: (b, jnp.minimum((cid(k) + 1) * hb, nc * hb - 1), 0)),
                    row(SSD_HEADS),
                    pl.BlockSpec((1, SSD_HEADS, lch), lambda b, k: (b, 0, cid(k))),
                    _const_spec((SSD_CONV, wx)), _const_spec((1, wx)),
                    _const_spec((1, SSD_HEADS)), _const_spec((SSD_HEADS, 1)),
                    _const_spec((1, SSD_HEADS)), _const_spec((SSD_HEADS, 1))]
        args = [xbc, xbc, xbc, dt, jnp.swapaxes(dt, 1, 2), conv_w, conv_b.reshape(1, wx),
                dt_bias[d].reshape(1, SSD_HEADS), dt_bias[d].reshape(SSD_HEADS, 1),
                a_log[d].reshape(1, SSD_HEADS), a_log[d].reshape(SSD_HEADS, 1)]
        scratch = [pltpu.VMEM((SSD_GROUPS, SSD_STATE, SSD_WIDTH // SSD_GROUPS), F32),
                   pltpu.VMEM((lch + 2 * SUBLANES, wx), F32)]
        if rev:
            in_specs += [row(SSD_WIDTH), row(SSD_WIDTH), _const_spec((1, SSD_WIDTH)), _const_spec((1, SSD_WIDTH))]
            args += [z, yf, dsk, norm_w.reshape(1, SSD_WIDTH)]
            scratch += [pltpu.VMEM((lch, SSD_WIDTH), F32)]
        out = pl.pallas_call(
            functools.partial(_ssd_kernel, rev=rev, nc=nc, ncc=ncc),
            grid=(bsz, nc),
            in_specs=in_specs,
            out_specs=row(SSD_WIDTH),
            out_shape=jax.ShapeDtypeStruct((bsz, t, SSD_WIDTH), BF16 if rev else F32),
            scratch_shapes=scratch,
            compiler_params=_cparams("arbitrary", "arbitrary"),
            name="ssd_bwd" if rev else "ssd_fwd",
        )(*args)
        yf = out
    return out


def _outproj_kernel(ys_ref, yd_ref, xc_ref, xl_ref, wt_ref, wb_ref, g1_ref, gate_ref, g2_ref, sc_ref, sh_ref,
                    wr_ref, xlo_ref, xco_ref, h2_ref, lg_ref):
    b, t = pl.program_id(0), pl.program_id(1)
    row = _mod_row(t, b)
    y = (jnp.dot(ys_ref[0], wt_ref[...], preferred_element_type=F32)
         + jnp.dot(yd_ref[0], wb_ref[...], preferred_element_type=F32))
    x = jnp.where(t == 0, xc_ref[0], xl_ref[0])
    xn = x + gate_ref[pl.ds(row, 1), :] * (_rms(y) * g1_ref[...])

    @pl.when(t == 0)
    def _():
        xco_ref[0] = xn

    @pl.when(t > 0)
    def _():
        xlo_ref[0] = xn

    h2 = _rms(xn) * g2_ref[...]
    h2 = h2 * (1.0 + sc_ref[pl.ds(row, 1), :]) + sh_ref[pl.ds(row, 1), :]
    h2_ref[0] = h2
    lg_ref[0] = jnp.dot(h2.astype(BF16), wr_ref[...], preferred_element_type=F32)


def out_proj(y_s5, y_ssd, xc, xl, w_out, g1, gate1, g2, sc2, sh2, w_router):
    bsz, lc, d = xc.shape
    n = xl.shape[1]
    nt = (lc + n) // ROW_TILE
    w = w_out.astype(BF16)
    hw = S5_WIDTH
    tile = lambda wd: pl.BlockSpec((1, ROW_TILE, wd), lambda b, t: (b, t, 0))
    pair_out = [pl.BlockSpec((1, ROW_TILE, d), lambda b, t: (b, jnp.maximum(t - 1, 0), 0)),
                pl.BlockSpec((1, ROW_TILE, d), lambda b, t: (b, 0, 0))]
    return pl.pallas_call(
        _outproj_kernel,
        grid=(bsz, nt),
        in_specs=[tile(hw), tile(SSD_WIDTH)] + _pair_specs(lc, d)
        + [_const_spec((hw, d)), _const_spec((SSD_WIDTH, d)), _const_spec((1, d)), _const_spec((8, d)),
           _const_spec((1, d)), _const_spec((8, d)), _const_spec((8, d)), _const_spec((d, N_EXPERTS))],
        out_specs=pair_out + [tile(d), tile(N_EXPERTS)],
        out_shape=[jax.ShapeDtypeStruct((bsz, n, d), F32), jax.ShapeDtypeStruct((bsz, lc, d), F32),
                   jax.ShapeDtypeStruct((bsz, lc + n, d), F32),
                   jax.ShapeDtypeStruct((bsz, lc + n, N_EXPERTS), F32)],
        compiler_params=_cparams("arbitrary", "arbitrary"),
        name="out_proj",
    )(y_s5, y_ssd, xc, xl, w[:hw], w[hw:], g1.reshape(1, d), gate1, g2.reshape(1, d), sc2, sh2,
      w_router.astype(BF16))


def _moe_kernel(x_ref, gate_ref, wg_ref, wu_ref, wd_ref, o_ref, wg_s, wu_s, wd_s):
    f = pl.program_id(1)
    nf = pl.num_programs(1)
    wg_s[...] = wg_ref[0].astype(BF16)
    wu_s[...] = wu_ref[0].astype(BF16)
    wd_s[...] = wd_ref[0].astype(BF16)
    rows = x_ref.shape[1]
    for r0 in range(0, rows, MOE_ROW_BLOCK):
        rb = min(MOE_ROW_BLOCK, rows - r0)
        x = x_ref[0, r0:r0 + rb, :]
        g = jnp.dot(x, wg_s[...], preferred_element_type=F32)
        u = jnp.dot(x, wu_s[...], preferred_element_type=F32)
        h = ((g * jax.nn.sigmoid(g)) * u).astype(BF16)
        part = jnp.dot(h, wd_s[...], preferred_element_type=F32)

        @pl.when(f == 0)
        def _():
            o_ref[0, r0:r0 + rb, :] = part

        @pl.when(jnp.logical_and(f > 0, f < nf - 1))
        def _():
            o_ref[0, r0:r0 + rb, :] += part

        @pl.when(f == nf - 1)
        def _():
            o_ref[0, r0:r0 + rb, :] = (o_ref[0, r0:r0 + rb, :] + part) * gate_ref[0, r0:r0 + rb, :]


def moe_ffn(xs, gate, w_gate, w_up, w_down):
    e, r, d = xs.shape
    ff = w_gate.shape[2]
    tf = MOE_F_TILE
    return pl.pallas_call(
        _moe_kernel,
        grid=(e, ff // tf),
        in_specs=[pl.BlockSpec((1, r, d), lambda i, f: (i, 0, 0)),
                  pl.BlockSpec((1, r, 1), lambda i, f: (i, 0, 0)),
                  pl.BlockSpec((1, d, tf), lambda i, f: (i, 0, f)),
                  pl.BlockSpec((1, d, tf), lambda i, f: (i, 0, f)),
                  pl.BlockSpec((1, tf, d), lambda i, f: (i, f, 0))],
        out_specs=pl.BlockSpec((1, r, d), lambda i, f: (i, 0, 0)),
        out_shape=jax.ShapeDtypeStruct((e, r, d), F32),
        scratch_shapes=[pltpu.VMEM((d, tf), BF16), pltpu.VMEM((d, tf), BF16), pltpu.VMEM((tf, d), BF16)],
        compiler_params=_cparams("arbitrary", "arbitrary"),
        name="moe_ffn",
    )(xs, gate, w_gate, w_up, w_down)


def rmsnorm(x, w):
    xf = x.astype(F32)
    y = xf * lax.rsqrt(jnp.mean(xf * xf, axis=-1, keepdims=True) + EPS)
    return (y * w.astype(F32)).astype(x.dtype)


def grid_transpose(x):
    b, n = x.shape[:2]
    rows = n // GRID_W
    return x.reshape((b, rows, GRID_W) + x.shape[2:]).swapaxes(1, 2).reshape(x.shape)


def expert_choice(h2, logits, lc, w_gate, w_up, w_down, with_ctx):
    bsz, t, d = h2.shape
    e = N_EXPERTS
    parts = [(lc, t)] + ([(0, lc)] if with_ctx else [])
    xs_l, gate_l, idx_l = [], [], []
    for lo, hi in parts:
        n = hi - lo
        cap = CAPACITY_FACTOR * n // e
        aff = jax.nn.softmax(logits[:, lo:hi], axis=-1)
        gate, idx = lax.top_k(jnp.swapaxes(aff, 1, 2), cap)
        xs = jax.vmap(lambda xb, ib: xb[ib])(h2[:, lo:hi], idx)
        xs_l.append(jnp.swapaxes(xs, 0, 1).reshape(e, bsz * cap, d))
        gate_l.append(jnp.swapaxes(gate, 0, 1).reshape(e, bsz * cap))
        idx_l.append(idx)
    xs = jnp.concatenate(xs_l, axis=1).astype(BF16)
    gate = jnp.concatenate(gate_l, axis=1)[..., None]
    out = moe_ffn(xs, gate, w_gate, w_up, w_down)
    res, r0 = [], 0
    for (lo, hi), idx in zip(parts, idx_l):
        n = hi - lo
        cap = idx.shape[-1]
        ob = jnp.swapaxes(out[:, r0:r0 + bsz * cap].reshape(e, bsz, cap, d), 0, 1)
        r0 += bsz * cap
        res.append(jax.vmap(lambda o, ib: jnp.zeros((n, d), o.dtype).at[ib.reshape(-1)].add(o.reshape(-1, d)))(ob, idx))
    return res


def kernel(x, c, ctx, c_ctx, ada_w, ada_b, norm_g, w_in, w_out, s5_lam_re, s5_lam_im, s5_log_step,
           s5_b_re, s5_b_im, s5_c_re, s5_c_im, s5_d, s5_w_glu, s5_b_glu, ssd_conv_w, ssd_conv_b,
           ssd_dt_bias, ssd_a_log, ssd_d, ssd_norm, moe_router, moe_w_gate, moe_w_up, moe_w_down):
    bsz, n, d = x.shape
    lc = ctx.shape[1]
    nseg = SUBLANES // bsz
    cs = jnp.zeros((8, d), F32).at[:bsz].set(c).at[2].set(c_ctx)
    mods = adaln_all(cs, ada_w, ada_b)
    xl, xc = x, ctx
    for i in range(DEPTH):
        col_major = i % 2 == 1
        last = i == DEPTH - 1
        sh1, sc1, g1, sh2, sc2, g2 = [mods[i, :, k * d:(k + 1) * d] for k in range(N_MOD)]
        if col_major:
            xl = grid_transpose(xl)
        u, z, xbc, dtf, dtb = in_proj(xc, xl, norm_g[i, 0], sc1, sh1, w_in[i])
        ops = s5_operators(s5_lam_re[i], s5_lam_im[i], s5_log_step[i], s5_b_re[i], s5_b_im[i],
                           s5_c_re[i], s5_c_im[i], s5_d[i], lc // nseg, n // nseg)
        y_s5 = s5_glu(s5_mix(u, lc, ops), s5_w_glu[i], s5_b_glu[i])
        y_ssd = ssd_mix(z, xbc, dtf, dtb, lc, ssd_conv_w[i], ssd_conv_b[i], ssd_dt_bias[i],
                        ssd_a_log[i], ssd_d[i], ssd_norm[i])
        xl, xc, h2, logits = out_proj(y_s5, y_ssd, xc, xl, w_out[i], norm_g[i, 1], g1, norm_g[i, 2],
                                      sc2, sh2, moe_router[i])
        res = expert_choice(h2, logits, lc, moe_w_gate[i], moe_w_up[i], moe_w_down[i], not last)
        xl = xl + g2[:bsz, None, :] * rmsnorm(res[0], norm_g[i, 3])
        if not last:
            xc = xc + g2[2][None, None, :] * rmsnorm(res[1], norm_g[i, 3])
        if col_major:
            xl = grid_transpose(xl)
    return xl
```

```python
import functools
import math

import jax
import jax.numpy as jnp
from jax import lax
from jax.experimental import pallas as pl
from jax.experimental.pallas import tpu as pltpu

D_MODEL = 2048
DEPTH = 4
GRID_W = 64
EPS = 1e-6
N_MOD = 6

S5_WIDTH = 1024
S5_GROUP = 16
S5_GROUPS = S5_WIDTH // S5_GROUP
S5_STATE = 64

SSD_WIDTH = 1024
SSD_HEAD_DIM = 64
SSD_HEADS = SSD_WIDTH // SSD_HEAD_DIM
SSD_GROUPS = 2
SSD_STATE = 128
SSD_CONV = 5
SSD_XBC = SSD_WIDTH + 2 * SSD_GROUPS * SSD_STATE

N_EXPERTS = 16
CAPACITY_FACTOR = 2
D_FF = 1536

F32 = jnp.float32
BF16 = jnp.bfloat16
HIGHEST = lax.Precision.HIGHEST

LANES = 128
SUBLANES = 8
ROW_TILE = 256
MOE_F_TILE = 256
MOE_ROW_BLOCK = 256
S5_BLOCK = 16
S5_PAIRS = S5_GROUPS // 2
SSD_CHUNK = 128
VMEM_LIMIT = 56 * 1024 * 1024


def _cparams(*sem):
    return pltpu.CompilerParams(dimension_semantics=sem, vmem_limit_bytes=VMEM_LIMIT)


def _adaln_kernel(c_ref, w_ref, b_ref, o_ref):
    c = c_ref[...]
    a = (c * jax.nn.sigmoid(c)).astype(BF16)
    o_ref[0] = jnp.dot(a, w_ref[0].astype(BF16), preferred_element_type=F32) + b_ref[0]


def adaln_all(cs, ada_w, ada_b):
    depth, d, n = ada_w.shape
    tn = 1024
    return pl.pallas_call(
        _adaln_kernel,
        grid=(depth, n // tn),
        in_specs=[pl.BlockSpec((8, d), lambda l, j: (0, 0)),
                  pl.BlockSpec((1, d, tn), lambda l, j: (l, 0, j)),
                  pl.BlockSpec((1, 1, tn), lambda l, j: (l, 0, j))],
        out_specs=pl.BlockSpec((1, 8, tn), lambda l, j: (l, 0, j)),
        out_shape=jax.ShapeDtypeStruct((depth, 8, n), F32),
        compiler_params=_cparams("arbitrary", "arbitrary"),
        name="adaln",
    )(cs, ada_w, ada_b.reshape(depth, 1, n))


def _mod_row(t, b):
    return jnp.where(t == 0, 2, b)


def _rms(x):
    return x * lax.rsqrt(jnp.mean(x * x, axis=-1, keepdims=True) + EPS)


def _s5_block_rows(b, t, bsz, lat_tiles):
    return jnp.where(t == 0, bsz * lat_tiles + b, b * lat_tiles + t - 1)


def _inproj_kernel(xc_ref, xl_ref, g_ref, sc_ref, sh_ref, wu_ref, wz_ref, wx_ref, wf_ref, wb_ref,
                   u_ref, z_ref, xbc_ref, dtf_ref, dtb_ref, r_s):
    b, t = pl.program_id(0), pl.program_id(1)
    x = jnp.where(t == 0, xc_ref[0], xl_ref[0])
    row = _mod_row(t, b)
    h = _rms(x) * g_ref[...]
    h = (h * (1.0 + sc_ref[pl.ds(row, 1), :]) + sh_ref[pl.ds(row, 1), :]).astype(BF16)
    u = jnp.dot(h, wu_ref[...], preferred_element_type=F32)
    nk = S5_WIDTH // LANES
    blocks = ROW_TILE // S5_BLOCK
    pw = 2 * S5_GROUP
    for k in range(nk):
        r_s[k] = u[:, k * LANES:(k + 1) * LANES]
    for k in range(nk):
        xts = [r_s[k, pl.ds(tt, blocks, stride=S5_BLOCK), :] for tt in range(S5_BLOCK)]
        for j in range(LANES // pw):
            u_ref[k * (LANES // pw) + j] = jnp.concatenate([xt[:, j * pw:(j + 1) * pw] for xt in xts], axis=1)
    z_ref[0] = jnp.dot(h, wz_ref[...], preferred_element_type=F32)
    xbc_ref[0] = jnp.dot(h, wx_ref[...], preferred_element_type=F32)
    dtf_ref[0] = jnp.dot(h, wf_ref[...], preferred_element_type=F32)
    dtb_ref[0] = jnp.dot(h, wb_ref[...], preferred_element_type=F32)


def _pair_specs(lc, d):
    assert lc == ROW_TILE
    return [pl.BlockSpec((1, ROW_TILE, d), lambda b, t: (b, 0, 0)),
            pl.BlockSpec((1, ROW_TILE, d), lambda b, t: (b, jnp.maximum(t - 1, 0), 0))]


def _const_spec(shape):
    return pl.BlockSpec(shape, lambda *_: (0,) * len(shape), pipeline_mode=pl.Buffered(1))


def in_proj(xc, xl, g, sc, sh, w_in):
    bsz, lc, d = xc.shape
    n = xl.shape[1]
    nt = (lc + n) // ROW_TILE
    s1, s2, s3 = S5_WIDTH, S5_WIDTH + SSD_WIDTH, S5_WIDTH + SSD_WIDTH + SSD_XBC
    w = w_in.astype(BF16)
    ws = [w[:, :s1], w[:, s1:s2], w[:, s2:s3], w[:, s3:s3 + SSD_HEADS], w[:, s3 + SSD_HEADS:]]
    widths = [x.shape[1] for x in ws]
    blocks = ROW_TILE // S5_BLOCK
    s5_rows = bsz * (lc + n) // S5_BLOCK
    s5_lanes = 2 * S5_GROUP * S5_BLOCK
    u_spec = pl.BlockSpec((S5_PAIRS, blocks, s5_lanes),
                          lambda b, t: (0, _s5_block_rows(b, t, bsz, n // ROW_TILE), 0))
    return pl.pallas_call(
        _inproj_kernel,
        grid=(bsz, nt),
        in_specs=_pair_specs(lc, d) + [_const_spec((1, d)), _const_spec((8, d)), _const_spec((8, d))]
        + [_const_spec((d, wd)) for wd in widths],
        out_specs=[u_spec] + [pl.BlockSpec((1, ROW_TILE, wd), lambda b, t: (b, t, 0)) for wd in widths[1:]],
        out_shape=[jax.ShapeDtypeStruct((S5_PAIRS, s5_rows, s5_lanes), F32)]
        + [jax.ShapeDtypeStruct((bsz, lc + n, wd), F32) for wd in widths[1:]],
        scratch_shapes=[pltpu.VMEM((S5_WIDTH // LANES, ROW_TILE, LANES), F32)],
        compiler_params=_cparams("arbitrary", "arbitrary"),
        name="in_proj",
    )(xc, xl, g.reshape(1, d), sc, sh, *ws)


def s5_operators(lam_re, lam_im, log_step, b_re, b_im, c_re, c_im, d_skip, seg_ctx, seg_lat):
    g_, p_, h_, lk = S5_GROUPS, S5_STATE, S5_GROUP, S5_BLOCK
    j = jnp.arange(lk + 1, dtype=F32)[:, None, None]
    eye2 = jnp.eye(2, dtype=F32)
    k_dir, bst_dir, coff_dir, lam_rows = [], [], [], []
    for d in range(2):
        step = jnp.exp(log_step[d])[:, None]
        e_re, ang = lam_re[d] * step, lam_im[d] * step
        pr = jnp.exp(j * e_re) * jnp.cos(j * ang)
        pi = jnp.exp(j * e_re) * jnp.sin(j * ang)
        den = lam_re[d] * lam_re[d] + lam_im[d] * lam_im[d]
        nr = pr[1] - 1.0
        f_re = (nr * lam_re[d] + pi[1] * lam_im[d]) / den
        f_im = (pi[1] * lam_re[d] - nr * lam_im[d]) / den
        bb_re = f_re[..., None] * b_re - f_im[..., None] * b_im
        bb_im = f_re[..., None] * b_im + f_im[..., None] * b_re
        w_re = pr[:lk, :, :, None] * bb_re - pi[:lk, :, :, None] * bb_im
        w_im = pr[:lk, :, :, None] * bb_im + pi[:lk, :, :, None] * bb_re
        k_dir.append(jnp.einsum('gop,jgph->jgoh', c_re, w_re, precision=HIGHEST)
                     - jnp.einsum('gop,jgph->jgoh', c_im, w_im, precision=HIGHEST))
        order = slice(None, None, -1) if d == 0 else slice(None)
        bst_dir.append(jnp.stack([w_re[order], w_im[order]], axis=0).transpose(2, 1, 4, 0, 3))
        kk = jnp.arange(1, lk + 1) if d == 0 else jnp.arange(lk, 0, -1)
        ar, ai = pr[kk], pi[kk]
        cr = jnp.einsum('gop,tgp->gpto', c_re, ar) - jnp.einsum('gop,tgp->gpto', c_im, ai)
        ci = -(jnp.einsum('gop,tgp->gpto', c_re, ai) + jnp.einsum('gop,tgp->gpto', c_im, ar))
        coff_dir.append(jnp.stack([cr, ci], axis=1))
        for n_pow in (lk, seg_ctx, seg_lat):
            lam_rows += [jnp.exp(n_pow * e_re) * jnp.cos(n_pow * ang), jnp.exp(n_pow * e_re) * jnp.sin(n_pow * ang)]
    s_i, t_i = jnp.arange(lk)[:, None], jnp.arange(lk)[None, :]
    kf = jnp.where((t_i >= s_i)[..., None, None, None], k_dir[0][jnp.clip(t_i - s_i, 0, lk - 1)], 0.0)
    kb = jnp.where((s_i >= t_i)[..., None, None, None], k_dir[1][jnp.clip(s_i - t_i, 0, lk - 1)], 0.0)
    dk = (jnp.eye(lk, dtype=F32)[:, :, None, None, None] * jnp.eye(h_, dtype=F32)[None, None, None]
          * d_skip.reshape(g_, h_)[None, None, :, :, None])
    tz = (kf + kb + dk).transpose(2, 0, 4, 1, 3).reshape(S5_PAIRS, 2, lk * h_, lk * h_)
    bst = jnp.stack(bst_dir, axis=3).reshape(S5_PAIRS, 2, lk, h_, 2, 2, p_)
    bst = jnp.einsum('aishdqp,ij->aishdqjp', bst, eye2).reshape(S5_PAIRS, 2 * lk * h_, 8 * p_)
    coff = jnp.stack(coff_dir, axis=1).reshape(S5_PAIRS, 2, 2, 2, p_, lk, h_)
    coff = jnp.einsum('aidqpto,ij->adqipjto', coff, eye2).reshape(S5_PAIRS, 8 * p_, 2 * lk * h_)
    lam = jnp.stack([r.reshape(S5_PAIRS, 2 * p_) for r in lam_rows], axis=1)
    order12 = jnp.array([0, 1, 6, 7, 2, 3, 8, 9, 4, 5, 10, 11])
    lam = jnp.concatenate([lam[:, order12], jnp.zeros((S5_PAIRS, 4, 2 * p_), F32)], axis=1)
    return tz.astype(BF16), bst.astype(BF16), coff.astype(BF16), lam


def _s5_kernel(u_ref, tz_ref, bst_ref, coff_ref, lam_ref, y_ref, x_s, h_s, t_s, g_s, *, n_lat, n_ctx, bsz):
    nseg = SUBLANES // bsz
    u = u_ref[0]
    x_s[...] = jnp.dot(u, bst_ref[0], preferred_element_type=F32)
    lam = lam_ref[0]

    def scan(base, nsteps, rev, co, lr, li, init, store):
        def body(k, carry):
            hr, hi = carry
            q = nsteps - 1 - k if rev else k
            r0 = pl.multiple_of(base + q * SUBLANES, SUBLANES)
            if store:
                h_s[pl.ds(r0, SUBLANES), co:co + LANES] = hr
                h_s[pl.ds(r0, SUBLANES), co + LANES:co + 2 * LANES] = hi
            xr = x_s[pl.ds(r0, SUBLANES), co:co + LANES]
            xi = x_s[pl.ds(r0, SUBLANES), co + LANES:co + 2 * LANES]
            return lr * hr - li * hi + xr, lr * hi + li * hr + xi
        return lax.fori_loop(0, nsteps, body, init, unroll=4)

    def seg_carry(tot, lr, li, inits, rev):
        t_s[:, :LANES], t_s[:, LANES:] = tot
        finals = []
        for b in range(bsz):
            gr, gi = inits[b]
            for seg in (range(nseg - 1, -1, -1) if rev else range(nseg)):
                s = b * nseg + seg
                g_s[s:s + 1, :LANES], g_s[s:s + 1, LANES:] = gr, gi
                tr, ti = t_s[s:s + 1, :LANES], t_s[s:s + 1, LANES:]
                gr, gi = lr * gr - li * gi + tr, lr * gi + li * gr + ti
            finals.append((gr, gi))
        return (g_s[:, :LANES], g_s[:, LANES:]), finals

    zero8 = (jnp.zeros((SUBLANES, LANES), F32), jnp.zeros((SUBLANES, LANES), F32))
    zero1 = (jnp.zeros((1, LANES), F32), jnp.zeros((1, LANES), F32))
    for d in range(2):
        rev, co = d == 1, d * 2 * LANES
        lr = jnp.broadcast_to(lam[2 * d:2 * d + 1], (SUBLANES, LANES))
        li = jnp.broadcast_to(lam[2 * d + 1:2 * d + 2], (SUBLANES, LANES))
        init1 = [zero1] * bsz
        for base, nsteps, row in ((n_lat * SUBLANES, n_ctx, 4 + 2 * d), (0, n_lat, 8 + 2 * d)):
            tot = scan(base, nsteps, rev, co, lr, li, zero8, False)
            g0, init1 = seg_carry(tot, lam[row:row + 1], lam[row + 1:row + 2], init1, rev)
            scan(base, nsteps, rev, co, lr, li, g0, True)
    half = S5_BLOCK * S5_GROUP
    y = jnp.dot(h_s[...].astype(BF16), coff_ref[0], preferred_element_type=F32)
    y = y + jnp.concatenate([jnp.dot(u[:, :half], tz_ref[0, 0], preferred_element_type=F32),
                             jnp.dot(u[:, half:], tz_ref[0, 1], preferred_element_type=F32)], axis=1)
    y_ref[0] = jax.nn.gelu(y).astype(BF16)


def _s5_pack(part, nseg):
    bsz, n, _ = part.shape
    steps = n // (nseg * S5_BLOCK)
    p = part.reshape(bsz, nseg, steps, S5_BLOCK, S5_PAIRS, 2, S5_GROUP)
    return p.transpose(4, 2, 0, 1, 5, 3, 6).reshape(S5_PAIRS, steps * bsz * nseg, 2 * S5_BLOCK * S5_GROUP)


def _s5_unpack(rows, bsz, nseg):
    steps = rows.shape[1] // (bsz * nseg)
    p = rows.reshape(S5_PAIRS, steps, bsz, nseg, 2, S5_BLOCK, S5_GROUP)
    return p.transpose(2, 3, 1, 5, 0, 4, 6).reshape(bsz, nseg * steps * S5_BLOCK, S5_WIDTH)


def s5_mix(u, lc, ops):
    bsz, t, _ = u.shape
    assert SUBLANES % bsz == 0
    nseg = SUBLANES // bsz
    tz, bst, coff, lam = ops
    n_lat, n_ctx = (t - lc) // (nseg * S5_BLOCK), lc // (nseg * S5_BLOCK)
    rows = (n_lat + n_ctx) * SUBLANES
    width = 2 * S5_BLOCK * S5_GROUP
    up = jnp.concatenate([_s5_pack(u[:, lc:], nseg), _s5_pack(u[:, :lc], nseg)], axis=1)
    blk = lambda *shape: pl.BlockSpec((1,) + shape, lambda i: (i,) + (0,) * len(shape))
    y = pl.pallas_call(
        functools.partial(_s5_kernel, n_lat=n_lat, n_ctx=n_ctx, bsz=bsz),
        grid=(S5_PAIRS,),
        in_specs=[blk(rows, width), blk(2, width // 2, width // 2), blk(width, 4 * LANES), blk(4 * LANES, width),
                  blk(16, LANES)],
        out_specs=blk(rows, width),
        out_shape=jax.ShapeDtypeStruct((S5_PAIRS, rows, width), BF16),
        scratch_shapes=[pltpu.VMEM((rows, 4 * LANES), F32), pltpu.VMEM((rows, 4 * LANES), F32),
                        pltpu.VMEM((SUBLANES, 2 * LANES), F32), pltpu.VMEM((SUBLANES, 2 * LANES), F32)],
        compiler_params=_cparams("arbitrary"),
        name="s5_mix",
    )(up, tz, bst, coff, lam)
    nl = n_lat * SUBLANES
    return jnp.concatenate([_s5_unpack(y[:, nl:], bsz, nseg), _s5_unpack(y[:, :nl], bsz, nseg)], axis=1)


def s5_compact_operators(lam_re, lam_im, log_step, b_re, b_im, c_re, c_im, d_skip, seg_ctx, seg_lat):
    g_, p_, h_, lk = S5_GROUPS, S5_STATE, S5_GROUP, S5_BLOCK
    j = jnp.arange(lk, dtype=F32)[:, None, None]
    eye2 = jnp.eye(2, dtype=F32)
    k_dir, bb_dir, lam_rows = [], [], []
    for d in range(2):
        step = jnp.exp(log_step[d])[:, None]
        e_re, ang = lam_re[d] * step, lam_im[d] * step
        pr = jnp.exp(j * e_re) * jnp.cos(j * ang)
        pi = jnp.exp(j * e_re) * jnp.sin(j * ang)
        den = lam_re[d] * lam_re[d] + lam_im[d] * lam_im[d]
        nr = pr[1] - 1.0
        f_re = (nr * lam_re[d] + pi[1] * lam_im[d]) / den
        f_im = (pi[1] * lam_re[d] - nr * lam_im[d]) / den
        bb_re = f_re[..., None] * b_re - f_im[..., None] * b_im
        bb_im = f_re[..., None] * b_im + f_im[..., None] * b_re
        w_re = pr[:, :, :, None] * bb_re - pi[:, :, :, None] * bb_im
        w_im = pr[:, :, :, None] * bb_im + pi[:, :, :, None] * bb_re
        k_dir.append(jnp.einsum('gop,jgph->jgho', c_re, w_re, precision=HIGHEST)
                     - jnp.einsum('gop,jgph->jgho', c_im, w_im, precision=HIGHEST))
        bb_dir.append(jnp.stack([bb_re, bb_im], axis=0).transpose(1, 3, 0, 2))
        for n_pow in (lk, seg_ctx, seg_lat, 1):
            lam_rows += [jnp.exp(n_pow * e_re) * jnp.cos(n_pow * ang), jnp.exp(n_pow * e_re) * jnp.sin(n_pow * ang)]
    mid = k_dir[0][0] + k_dir[1][0] + jnp.eye(h_, dtype=F32)[None] * d_skip.reshape(g_, h_, 1)
    taps = jnp.concatenate([k_dir[1][1:][::-1], mid[None], k_dir[0][1:]], axis=0)
    taps = taps.transpose(1, 2, 0, 3).reshape(S5_PAIRS, 2, h_, 2 * lk - 1, h_)
    kfull = jnp.einsum('aihmo,ij->aihmjo', taps, eye2).reshape(S5_PAIRS, 2 * h_, (2 * lk - 1) * 2 * h_)
    kfull = jnp.pad(kfull, ((0, 0), (0, 0), (0, 2 * lk * 2 * h_ - kfull.shape[2])))
    bbp = jnp.stack(bb_dir, axis=1).reshape(S5_PAIRS, 2, 2, h_, 2, p_)
    bbp = jnp.einsum('aidhqp,ij->adihqjp', bbp, eye2).reshape(S5_PAIRS, 2, 2 * h_, 4 * p_)
    cpt = jnp.stack([c_re, -c_im], axis=2).reshape(S5_PAIRS, 2, h_, 2, p_)
    cpt = jnp.einsum('ajoqp,ij->ajoqip', cpt, eye2).reshape(S5_PAIRS, 2 * h_, 4 * p_)
    order16 = [0, 1, 8, 9, 2, 3, 10, 11, 4, 5, 12, 13, 6, 7, 14, 15]
    lam = jnp.stack([lam_rows[r].reshape(S5_PAIRS, 2 * p_) for r in order16], axis=1)
    return kfull, bbp, cpt, lam


def _s5_fused_kernel(z_ref, kf_ref, bbp_ref, cpt_ref, lam_ref, y_ref, tz_s, bst_s, cft_s, x_s, h_s, t_s, g_s,
                     *, n_lat, n_ctx, bsz):
    nseg = SUBLANES // bsz
    pw = 2 * S5_GROUP
    lam = lam_ref[0]
    kf = kf_ref[0]
    for tt in range(S5_BLOCK):
        off = (S5_BLOCK - 1 - tt) * pw
        tz_s[tt * pw:(tt + 1) * pw, :] = kf[:, off:off + S5_BLOCK * pw].astype(BF16)
    for d in range(2):
        co = d * 2 * LANES
        lr = jnp.broadcast_to(lam[12 + 2 * d:13 + 2 * d], (pw, LANES))
        li = jnp.broadcast_to(lam[13 + 2 * d:14 + 2 * d], (pw, LANES))
        order = range(S5_BLOCK - 1, -1, -1) if d == 0 else range(S5_BLOCK)
        wr, wi = bbp_ref[0, d, :, :LANES], bbp_ref[0, d, :, LANES:]
        for tt in order:
            bst_s[tt * pw:(tt + 1) * pw, co:co + LANES] = wr.astype(BF16)
            bst_s[tt * pw:(tt + 1) * pw, co + LANES:co + 2 * LANES] = wi.astype(BF16)
            wr, wi = lr * wr - li * wi, lr * wi + li * wr
        xr, xi = cpt_ref[0, :, :LANES], cpt_ref[0, :, LANES:]
        for tt in (range(S5_BLOCK) if d == 0 else range(S5_BLOCK - 1, -1, -1)):
            xr, xi = lr * xr + li * xi, lr * xi - li * xr
            cft_s[tt * pw:(tt + 1) * pw, co:co + LANES] = xr.astype(BF16)
            cft_s[tt * pw:(tt + 1) * pw, co + LANES:co + 2 * LANES] = xi.astype(BF16)

    u = z_ref[0].astype(BF16)
    x = jnp.dot(u, bst_s[...], preferred_element_type=F32)
    for k in range(4):
        x_s[k] = x[:, k * LANES:(k + 1) * LANES]

    def scan(base, nsteps, stride, rev, cr, lr, li, init, store):
        def body(k, carry):
            hr, hi = carry
            q = nsteps - 1 - k if rev else k
            rows = pl.ds(base + q, SUBLANES, stride=stride)
            if store:
                h_s[cr, rows, :] = hr
                h_s[cr + 1, rows, :] = hi
            return lr * hr - li * hi + x_s[cr, rows, :], lr * hi + li * hr + x_s[cr + 1, rows, :]
        return lax.fori_loop(0, nsteps, body, init, unroll=4)

    def seg_carry(tot, lr, li, inits, rev):
        t_s[:, :LANES], t_s[:, LANES:] = tot
        finals = []
        for b in range(bsz):
            gr, gi = inits[b]
            for seg in (range(nseg - 1, -1, -1) if rev else range(nseg)):
                s = b * nseg + seg
                g_s[s:s + 1, :LANES], g_s[s:s + 1, LANES:] = gr, gi
                tr, ti = t_s[s:s + 1, :LANES], t_s[s:s + 1, LANES:]
                gr, gi = lr * gr - li * gi + tr, lr * gi + li * gr + ti
            finals.append((gr, gi))
        return (g_s[:, :LANES], g_s[:, LANES:]), finals

    zero8 = (jnp.zeros((SUBLANES, LANES), F32), jnp.zeros((SUBLANES, LANES), F32))
    zero1 = (jnp.zeros((1, LANES), F32), jnp.zeros((1, LANES), F32))
    for d in range(2):
        rev = d == 1
        lr = jnp.broadcast_to(lam[2 * d:2 * d + 1], (SUBLANES, LANES))
        li = jnp.broadcast_to(lam[2 * d + 1:2 * d + 2], (SUBLANES, LANES))
        init1 = [zero1] * bsz
        for base, nsteps, row in ((n_lat * SUBLANES, n_ctx, 4 + 2 * d), (0, n_lat, 8 + 2 * d)):
            tot = scan(base, nsteps, nsteps, rev, 2 * d, lr, li, zero8, False)
            g0, init1 = seg_carry(tot, lam[row:row + 1], lam[row + 1:row + 2], init1, rev)
            scan(base, nsteps, nsteps, rev, 2 * d, lr, li, g0, True)
    hcat = jnp.concatenate([h_s[k] for k in range(4)], axis=1).astype(BF16)
    y = lax.dot_general(hcat, cft_s[...], (((1,), (1,)), ((), ())), preferred_element_type=F32)
    y = y + jnp.dot(u, tz_s[...], preferred_element_type=F32)
    y_ref[0] = jax.nn.gelu(y)


def s5_mix_packed(z, bsz, lc, n, ops):
    assert SUBLANES % bsz == 0
    nseg = SUBLANES // bsz
    kfull, bbp, cpt, lam = ops
    n_lat, n_ctx = n // (nseg * S5_BLOCK), lc // (nseg * S5_BLOCK)
    rows, width = z.shape[1], z.shape[2]
    assert rows == (n_lat + n_ctx) * SUBLANES
    blk = lambda *shape: pl.BlockSpec((1,) + shape, lambda i: (i,) + (0,) * len(shape))
    return pl.pallas_call(
        functools.partial(_s5_fused_kernel, n_lat=n_lat, n_ctx=n_ctx, bsz=bsz),
        grid=(S5_PAIRS,),
        in_specs=[blk(rows, width), blk(*kfull.shape[1:]), blk(*bbp.shape[1:]), blk(*cpt.shape[1:]), blk(16, LANES)],
        out_specs=blk(rows, width),
        out_shape=jax.ShapeDtypeStruct((S5_PAIRS, rows, width), F32),
        scratch_shapes=[pltpu.VMEM((width, width), BF16), pltpu.VMEM((width, 4 * LANES), BF16),
                        pltpu.VMEM((width, 4 * LANES), BF16),
                        pltpu.VMEM((4, rows, LANES), F32), pltpu.VMEM((4, rows, LANES), F32),
                        pltpu.VMEM((SUBLANES, 2 * LANES), F32), pltpu.VMEM((SUBLANES, 2 * LANES), F32)],
        compiler_params=_cparams("arbitrary"),
        name="s5_mix",
    )(z, kfull, bbp, cpt, lam)


def _glu_kernel(y_ref, wa_ref, wb_ref, ba_ref, bb_ref, o_ref, a_s):
    pw = 2 * S5_GROUP
    per = LANES // pw
    blocks = ROW_TILE // S5_BLOCK
    for tt in range(S5_BLOCK):
        for k in range(S5_WIDTH // LANES):
            a_s[k, pl.ds(tt, blocks, stride=S5_BLOCK), :] = jnp.concatenate(
                [y_ref[k * per + j, :, tt * pw:(tt + 1) * pw] for j in range(per)], axis=1)
    y = jnp.concatenate([a_s[k] for k in range(S5_WIDTH // LANES)], axis=1).astype(BF16)
    a = jnp.dot(y, wa_ref[...], preferred_element_type=F32) + ba_ref[...]
    g = jnp.dot(y, wb_ref[...], preferred_element_type=F32) + bb_ref[...]
    o_ref[0] = (a * jax.nn.sigmoid(g)).astype(BF16)


def s5_glu(y, bsz, lc, n, w_glu, b_glu):
    w = S5_WIDTH
    wb = w_glu.astype(BF16)
    blocks = ROW_TILE // S5_BLOCK
    y_spec = pl.BlockSpec((S5_PAIRS, blocks, y.shape[2]),
                          lambda b, t: (0, _s5_block_rows(b, t, bsz, n // ROW_TILE), 0))
    return pl.pallas_call(
        _glu_kernel,
        grid=(bsz, (lc + n) // ROW_TILE),
        in_specs=[y_spec, _const_spec((w, w)), _const_spec((w, w)), _const_spec((1, w)), _const_spec((1, w))],
        out_specs=pl.BlockSpec((1, ROW_TILE, w), lambda b, i: (b, i, 0)),
        out_shape=jax.ShapeDtypeStruct((bsz, lc + n, w), BF16),
        scratch_shapes=[pltpu.VMEM((w // LANES, ROW_TILE, LANES), F32)],
        compiler_params=_cparams("arbitrary", "arbitrary"),
        name="s5_glu",
    )(y, wb[:, :w], wb[:, w:], b_glu[:w].reshape(1, w), b_glu[w:].reshape(1, w))


def _softplus(x):
    return jnp.maximum(x, 0.0) + jnp.log(1.0 + jnp.exp(-jnp.abs(x)))


def _ssd_chunk_id(k, rev, nc, ncc):
    if not rev:
        return k
    return jnp.where(k < ncc, ncc - 1 - k, nc - 1 - (k - ncc))


def _ssd_kernel(*refs, rev, nc, ncc):
    if rev:
        (x_ref, xp_ref, xn_ref, dt_ref, dtt_ref, cw_ref, cb_ref, bias_ref, biast_ref, alog_ref, alogt_ref,
         z_ref, yf_ref, dsk_ref, nw_ref, o_ref, st_s, xp_s, y_s) = refs
    else:
        (x_ref, xp_ref, xn_ref, dt_ref, dtt_ref, cw_ref, cb_ref, bias_ref, biast_ref, alog_ref, alogt_ref,
         o_ref, st_s, xp_s) = refs
    lch = SSD_CHUNK
    k = pl.program_id(1)
    c = _ssd_chunk_id(k, rev, nc, ncc)

    @pl.when(k == 0)
    def _():
        st_s[...] = jnp.zeros(st_s.shape, F32)

    first = jnp.logical_or(c == 0, c == ncc)
    last = jnp.logical_or(c == ncc - 1, c == nc - 1)
    xp_s[0:SUBLANES, :] = jnp.where(first, 0.0, xp_ref[0])
    xp_s[SUBLANES:SUBLANES + lch, :] = x_ref[0]
    xp_s[SUBLANES + lch:2 * SUBLANES + lch, :] = jnp.where(last, 0.0, xn_ref[0])
    acc = cb_ref[...] + cw_ref[0:1, :] * xp_s[SUBLANES - 2:SUBLANES - 2 + lch, :]
    for tap in range(1, SSD_CONV):
        acc = acc + cw_ref[tap:tap + 1, :] * xp_s[SUBLANES - 2 + tap:SUBLANES - 2 + tap + lch, :]
    xc = acc * jax.nn.sigmoid(acc)
    gn = SSD_GROUPS * SSD_STATE
    xs = xc[:, :SSD_WIDTH]
    bm = xc[:, SSD_WIDTH:SSD_WIDTH + gn]
    cm = xc[:, SSD_WIDTH + gn:]

    dt = _softplus(dt_ref[0] + bias_ref[...])
    dtt = _softplus(dtt_ref[0] + biast_ref[...])
    da = dt * -jnp.exp(alog_ref[...])
    dat = dtt * -jnp.exp(alogt_ref[...])
    row_i = lax.broadcasted_iota(jnp.int32, (lch, lch), 0)
    col_i = lax.broadcasted_iota(jnp.int32, (lch, lch), 1)
    tri = (col_i >= row_i) if rev else (col_i <= row_i)
    trit = (row_i >= col_i) if rev else (row_i <= col_i)
    cum = jnp.dot(tri.astype(F32), da, preferred_element_type=F32, precision=HIGHEST)
    cumt = jnp.dot(dat, trit.astype(F32), preferred_element_type=F32, precision=HIGHEST)
    end = 0 if rev else lch - 1
    tot = cum[end:end + 1, :]
    wt = dtt * jnp.exp(cumt[:, end:end + 1] - cumt)
    lane = lax.broadcasted_iota(jnp.int32, (1, LANES), 1)
    low = lane < SSD_HEAD_DIM
    heads_per_group = SSD_HEADS // SSD_GROUPS
    for g in range(SSD_GROUPS):
        bg = bm[:, g * SSD_STATE:(g + 1) * SSD_STATE]
        cg = cm[:, g * SSD_STATE:(g + 1) * SSD_STATE].astype(BF16)
        scores = lax.dot_general(cg, bg.astype(BF16), (((1,), (1,)), ((), ())), preferred_element_type=F32)
        bgt = bg.T
        st = st_s[g]
        yoff = jnp.dot(cg, st.astype(BF16), preferred_element_type=F32)
        for jp in range(heads_per_group // 2):
            ha = g * heads_per_group + 2 * jp
            lo = (ha // 2) * LANES
            sl = jp * LANES
            xsp = xs[:, lo:lo + LANES]
            xblk = jnp.concatenate([jnp.where(low, xsp, 0.0), jnp.where(low, 0.0, xsp)], axis=0).astype(BF16)
            ms, ecols, lhs2 = [], [], []
            for h in (ha, ha + 1):
                col = cum[:, h:h + 1]
                seg = col - cumt[h:h + 1, :]
                dec = jnp.exp(jnp.where(tri, seg, -jnp.inf))
                ms.append((scores * dec * dtt[h:h + 1, :]).astype(BF16))
                ecols.append(jnp.exp(col))
                lhs2.append((bgt * wt[h:h + 1, :]).astype(BF16))
            ydiag = jnp.dot(jnp.concatenate(ms, axis=1), xblk, preferred_element_type=F32)
            y_pair = ydiag + jnp.where(low, ecols[0], ecols[1]) * yoff[:, sl:sl + LANES]
            upd = jnp.dot(jnp.concatenate(lhs2, axis=1), xblk, preferred_element_type=F32)
            cd = jnp.where(low, jnp.exp(tot[:, ha:ha + 1]), jnp.exp(tot[:, ha + 1:ha + 2]))
            st_s[g, :, sl:sl + LANES] = cd * st[:, sl:sl + LANES] + upd
            if rev:
                y_s[:, lo:lo + LANES] = y_pair
            else:
                o_ref[0, :, lo:lo + LANES] = y_pair
    if rev:
        y = yf_ref[0] + y_s[...] + dsk_ref[...] * xs
        z = z_ref[0]
        o_ref[0] = (_rms(y * (z * jax.nn.sigmoid(z))) * nw_ref[...]).astype(BF16)


def ssd_mix(z, xbc, dtf, dtb, lc, conv_w, conv_b, dt_bias, a_log, d_skip, norm_w):
    bsz, t, wx = xbc.shape
    lch = SSD_CHUNK
    nc, ncc = t // lch, lc // lch
    hb = lch // SUBLANES
    dsk = jnp.repeat(d_skip, SSD_HEAD_DIM).reshape(1, SSD_WIDTH)
    yf = None
    for rev in (False, True):
        cid = functools.partial(_ssd_chunk_id, rev=rev, nc=nc, ncc=ncc)
        d = int(rev)
        dt = dtb if rev else dtf
        row = lambda wd: pl.BlockSpec((1, lch, wd), lambda b, k: (b, cid(k), 0))
        in_specs = [row(wx),
                    pl.BlockSpec((1, SUBLANES, wx), lambda b, k: (b, jnp.maximum(cid(k) * hb - 1, 0), 0)),
                    pl.BlockSpec((1, SUBLANES, wx), lambda b, k: (b, jnp.minimum((cid(k) + 1) * hb, nc * hb - 1), 0)),
                    row(SSD_HEADS),
                    pl.BlockSpec((1, SSD_HEADS, lch), lambda b, k: (b, 0, cid(k))),
                    _const_spec((SSD_CONV, wx)), _const_spec((1, wx)),
                    _const_spec((1, SSD_HEADS)), _const_spec((SSD_HEADS, 1)),
                    _const_spec((1, SSD_HEADS)), _const_spec((SSD_HEADS, 1))]
        args = [xbc, xbc, xbc, dt, jnp.swapaxes(dt, 1, 2), conv_w, conv_b.reshape(1, wx),
                dt_bias[d].reshape(1, SSD_HEADS), dt_bias[d].reshape(SSD_HEADS, 1),
                a_log[d].reshape(1, SSD_HEADS), a_log[d].reshape(SSD_HEADS, 1)]
        scratch = [pltpu.VMEM((SSD_GROUPS, SSD_STATE, SSD_WIDTH // SSD_GROUPS), F32),
                   pltpu.VMEM((lch + 2 * SUBLANES, wx), F32)]
        if rev:
            in_specs += [row(SSD_WIDTH), row(SSD_WIDTH), _const_spec((1, SSD_WIDTH)), _const_spec((1, SSD_WIDTH))]
            args += [z, yf, dsk, norm_w.reshape(1, SSD_WIDTH)]
            scratch += [pltpu.VMEM((lch, SSD_WIDTH), F32)]
        out = pl.pallas_call(
            functools.partial(_ssd_kernel, rev=rev, nc=nc, ncc=ncc),
            grid=(bsz, nc),
            in_specs=in_specs,
            out_specs=row(SSD_WIDTH),
            out_shape=jax.ShapeDtypeStruct((bsz, t, SSD_WIDTH), BF16 if rev else F32),
            scratch_shapes=scratch,
            compiler_params=_cparams("arbitrary", "arbitrary"),
            name="ssd_bwd" if rev else "ssd_fwd",
        )(*args)
        yf = out
    return out


def _outproj_kernel(ys_ref, yd_ref, xc_ref, xl_ref, wt_ref, wb_ref, g1_ref, gate_ref, g2_ref, sc_ref, sh_ref,
                    wr_ref, xlo_ref, xco_ref, h2_ref, lg_ref):
    b, t = pl.program_id(0), pl.program_id(1)
    row = _mod_row(t, b)
    y = (jnp.dot(ys_ref[0], wt_ref[...], preferred_element_type=F32)
         + jnp.dot(yd_ref[0], wb_ref[...], preferred_element_type=F32))
    x = jnp.where(t == 0, xc_ref[0], xl_ref[0])
    xn = x + gate_ref[pl.ds(row, 1), :] * (_rms(y) * g1_ref[...])

    @pl.when(t == 0)
    def _():
        xco_ref[0] = xn

    @pl.when(t > 0)
    def _():
        xlo_ref[0] = xn

    h2 = _rms(xn) * g2_ref[...]
    h2 = h2 * (1.0 + sc_ref[pl.ds(row, 1), :]) + sh_ref[pl.ds(row, 1), :]
    h2_ref[0] = h2
    lg_ref[0] = jnp.dot(h2.astype(BF16), wr_ref[...], preferred_element_type=F32)


def out_proj(y_s5, y_ssd, xc, xl, w_out, g1, gate1, g2, sc2, sh2, w_router):
    bsz, lc, d = xc.shape
    n = xl.shape[1]
    nt = (lc + n) // ROW_TILE
    w = w_out.astype(BF16)
    hw = S5_WIDTH
    tile = lambda wd: pl.BlockSpec((1, ROW_TILE, wd), lambda b, t: (b, t, 0))
    pair_out = [pl.BlockSpec((1, ROW_TILE, d), lambda b, t: (b, jnp.maximum(t - 1, 0), 0)),
                pl.BlockSpec((1, ROW_TILE, d), lambda b, t: (b, 0, 0))]
    return pl.pallas_call(
        _outproj_kernel,
        grid=(bsz, nt),
        in_specs=[tile(hw), tile(SSD_WIDTH)] + _pair_specs(lc, d)
        + [_const_spec((hw, d)), _const_spec((SSD_WIDTH, d)), _const_spec((1, d)), _const_spec((8, d)),
           _const_spec((1, d)), _const_spec((8, d)), _const_spec((8, d)), _const_spec((d, N_EXPERTS))],
        out_specs=pair_out + [tile(d), tile(N_EXPERTS)],
        out_shape=[jax.ShapeDtypeStruct((bsz, n, d), F32), jax.ShapeDtypeStruct((bsz, lc, d), F32),
                   jax.ShapeDtypeStruct((bsz, lc + n, d), F32),
                   jax.ShapeDtypeStruct((bsz, lc + n, N_EXPERTS), F32)],
        compiler_params=_cparams("arbitrary", "arbitrary"),
        name="out_proj",
    )(y_s5, y_ssd, xc, xl, w[:hw], w[hw:], g1.reshape(1, d), gate1, g2.reshape(1, d), sc2, sh2,
      w_router.astype(BF16))


def _moe_kernel(x_ref, gate_ref, wg_ref, wu_ref, wd_ref, o_ref, wg_s, wu_s, wd_s):
    f = pl.program_id(1)
    nf = pl.num_programs(1)
    wg_s[...] = wg_ref[0, 0].astype(BF16)
    wu_s[...] = wu_ref[0, 0].astype(BF16)
    wd_s[...] = wd_ref[0, 0].astype(BF16)
    rows = x_ref.shape[1]
    for r0 in range(0, rows, MOE_ROW_BLOCK):
        rb = min(MOE_ROW_BLOCK, rows - r0)
        x = x_ref[0, r0:r0 + rb, :]
        g = jnp.dot(x, wg_s[...], preferred_element_type=F32)
        u = jnp.dot(x, wu_s[...], preferred_element_type=F32)
        h = ((g * jax.nn.sigmoid(g)) * u).astype(BF16)
        part = jnp.dot(h, wd_s[...], preferred_element_type=F32)

        @pl.when(f == 0)
        def _():
            o_ref[0, r0:r0 + rb, :] = part

        @pl.when(jnp.logical_and(f > 0, f < nf - 1))
        def _():
            o_ref[0, r0:r0 + rb, :] += part

        @pl.when(f == nf - 1)
        def _():
            o_ref[0, r0:r0 + rb, :] = (o_ref[0, r0:r0 + rb, :] + part) * gate_ref[0, r0:r0 + rb, :]


def moe_ffn(xs, gate, layer, w_gate, w_up, w_down):
    e, r, d = xs.shape
    ff = w_gate.shape[3]
    tf = MOE_F_TILE
    return pl.pallas_call(
        _moe_kernel,
        grid=(e, ff // tf),
        in_specs=[pl.BlockSpec((1, r, d), lambda i, f: (i, 0, 0)),
                  pl.BlockSpec((1, r, 1), lambda i, f: (i, 0, 0)),
                  pl.BlockSpec((1, 1, d, tf), lambda i, f: (layer, i, 0, f)),
                  pl.BlockSpec((1, 1, d, tf), lambda i, f: (layer, i, 0, f)),
                  pl.BlockSpec((1, 1, tf, d), lambda i, f: (layer, i, f, 0))],
        out_specs=pl.BlockSpec((1, r, d), lambda i, f: (i, 0, 0)),
        out_shape=jax.ShapeDtypeStruct((e, r, d), F32),
        scratch_shapes=[pltpu.VMEM((d, tf), BF16), pltpu.VMEM((d, tf), BF16), pltpu.VMEM((tf, d), BF16)],
        compiler_params=_cparams("arbitrary", "arbitrary"),
        name="moe_ffn",
    )(xs, gate, w_gate, w_up, w_down)


def rmsnorm(x, w):
    xf = x.astype(F32)
    y = xf * lax.rsqrt(jnp.mean(xf * xf, axis=-1, keepdims=True) + EPS)
    return (y * w.astype(F32)).astype(x.dtype)


def grid_transpose(x):
    b, n = x.shape[:2]
    rows = n // GRID_W
    return x.reshape((b, rows, GRID_W) + x.shape[2:]).swapaxes(1, 2).reshape(x.shape)


def expert_choice(h2, logits, lc, layer, w_gate, w_up, w_down, with_ctx):
    bsz, t, d = h2.shape
    e = N_EXPERTS
    parts = [(lc, t)] + ([(0, lc)] if with_ctx else [])
    xs_l, gate_l, idx_l = [], [], []
    for lo, hi in parts:
        n = hi - lo
        cap = CAPACITY_FACTOR * n // e
        aff = jax.nn.softmax(logits[:, lo:hi], axis=-1)
        gate, idx = lax.top_k(jnp.swapaxes(aff, 1, 2), cap)
        xs = jax.vmap(lambda xb, ib: xb[ib])(h2[:, lo:hi], idx)
        xs_l.append(jnp.swapaxes(xs, 0, 1).reshape(e, bsz * cap, d))
        gate_l.append(jnp.swapaxes(gate, 0, 1).reshape(e, bsz * cap))
        idx_l.append(idx)
    xs = jnp.concatenate(xs_l, axis=1).astype(BF16)
    gate = jnp.concatenate(gate_l, axis=1)[..., None]
    out = moe_ffn(xs, gate, layer, w_gate, w_up, w_down)
    res, r0 = [], 0
    for (lo, hi), idx in zip(parts, idx_l):
        n = hi - lo
        cap = idx.shape[-1]
        ob = jnp.swapaxes(out[:, r0:r0 + bsz * cap].reshape(e, bsz, cap, d), 0, 1)
        r0 += bsz * cap
        res.append(jax.vmap(lambda o, ib: jnp.zeros((n, d), o.dtype).at[ib.reshape(-1)].add(o.reshape(-1, d)))(ob, idx))
    return res


def kernel(x, c, ctx, c_ctx, ada_w, ada_b, norm_g, w_in, w_out, s5_lam_re, s5_lam_im, s5_log_step,
           s5_b_re, s5_b_im, s5_c_re, s5_c_im, s5_d, s5_w_glu, s5_b_glu, ssd_conv_w, ssd_conv_b,
           ssd_dt_bias, ssd_a_log, ssd_d, ssd_norm, moe_router, moe_w_gate, moe_w_up, moe_w_down):
    bsz, n, d = x.shape
    lc = ctx.shape[1]
    nseg = SUBLANES // bsz
    cs = jnp.zeros((8, d), F32).at[:bsz].set(c).at[2].set(c_ctx)
    mods = adaln_all(cs, ada_w, ada_b)
    xl, xc = x, ctx
    for i in range(DEPTH):
        col_major = i % 2 == 1
        last = i == DEPTH - 1
        sh1, sc1, g1, sh2, sc2, g2 = [mods[i, :, k * d:(k + 1) * d] for k in range(N_MOD)]
        if col_major:
            xl = grid_transpose(xl)
        u, z, xbc, dtf, dtb = in_proj(xc, xl, norm_g[i, 0], sc1, sh1, w_in[i])
        ops = s5_compact_operators(s5_lam_re[i], s5_lam_im[i], s5_log_step[i], s5_b_re[i], s5_b_im[i],
                                   s5_c_re[i], s5_c_im[i], s5_d[i], lc // nseg, n // nseg)
        y_s5 = s5_glu(s5_mix_packed(u, bsz, lc, n, ops), bsz, lc, n, s5_w_glu[i], s5_b_glu[i])
        y_ssd = ssd_mix(z, xbc, dtf, dtb, lc, ssd_conv_w[i], ssd_conv_b[i], ssd_dt_bias[i],
                        ssd_a_log[i], ssd_d[i], ssd_norm[i])
        xl, xc, h2, logits = out_proj(y_s5, y_ssd, xc, xl, w_out[i], norm_g[i, 1], g1, norm_g[i, 2],
                                      sc2, sh2, moe_router[i])
        res = expert_choice(h2, logits, lc, i, moe_w_gate, moe_w_up, moe_w_down, not last)
        xl = xl + g2[:bsz, None, :] * rmsnorm(res[0], norm_g[i, 3])
        if not last:
            xc = xc + g2[2][None, None, :] * rmsnorm(res[1], norm_g[i, 3])
        if col_major:
            xl = grid_transpose(xl)
    return xl
```

```python
import functools
import math

import jax
import jax.numpy as jnp
from jax import lax
from jax.experimental import pallas as pl
from jax.experimental.pallas import tpu as pltpu

D_MODEL = 2048
DEPTH = 4
GRID_W = 64
EPS = 1e-6
N_MOD = 6

S5_WIDTH = 1024
S5_GROUP = 16
S5_GROUPS = S5_WIDTH // S5_GROUP
S5_STATE = 64

SSD_WIDTH = 1024
SSD_HEAD_DIM = 64
SSD_HEADS = SSD_WIDTH // SSD_HEAD_DIM
SSD_GROUPS = 2
SSD_STATE = 128
SSD_CONV = 5
SSD_XBC = SSD_WIDTH + 2 * SSD_GROUPS * SSD_STATE

N_EXPERTS = 16
CAPACITY_FACTOR = 2
D_FF = 1536

F32 = jnp.float32
BF16 = jnp.bfloat16
HIGHEST = lax.Precision.HIGHEST

LANES = 128
SUBLANES = 8
ROW_TILE = 256
MOE_F_TILE = 256
MOE_ROW_BLOCK = 256
MOE_N_TILE = 256
COMBINE_CHUNK = 128
S5_BLOCK = 16
S5_PAIRS = S5_GROUPS // 2
SSD_CHUNK = 128
VMEM_LIMIT = 56 * 1024 * 1024


def _cparams(*sem):
    return pltpu.CompilerParams(dimension_semantics=sem, vmem_limit_bytes=VMEM_LIMIT)


def _adaln_kernel(c_ref, w_ref, b_ref, o_ref):
    c = c_ref[...]
    a = (c * jax.nn.sigmoid(c)).astype(BF16)
    o_ref[0] = jnp.dot(a, w_ref[0].astype(BF16), preferred_element_type=F32) + b_ref[0]


def adaln_all(cs, ada_w, ada_b):
    depth, d, n = ada_w.shape
    tn = 1024
    return pl.pallas_call(
        _adaln_kernel,
        grid=(depth, n // tn),
        in_specs=[pl.BlockSpec((8, d), lambda l, j: (0, 0)),
                  pl.BlockSpec((1, d, tn), lambda l, j: (l, 0, j)),
                  pl.BlockSpec((1, 1, tn), lambda l, j: (l, 0, j))],
        out_specs=pl.BlockSpec((1, 8, tn), lambda l, j: (l, 0, j)),
        out_shape=jax.ShapeDtypeStruct((depth, 8, n), F32),
        compiler_params=_cparams("arbitrary", "arbitrary"),
        name="adaln",
    )(cs, ada_w, ada_b.reshape(depth, 1, n))


def _mod_row(t, b):
    return jnp.where(t == 0, 2, b)


def _rms(x):
    return x * lax.rsqrt(jnp.mean(x * x, axis=-1, keepdims=True) + EPS)


def _s5_block_rows(b, t, bsz, lat_tiles):
    return jnp.where(t == 0, bsz * lat_tiles + b, b * lat_tiles + t - 1)


def _inproj_kernel(xc_ref, xl_ref, g_ref, sc_ref, sh_ref, wu_ref, wz_ref, wx_ref, wf_ref, wb_ref,
                   u_ref, z_ref, xbc_ref, dtf_ref, dtb_ref, r_s):
    b, t = pl.program_id(0), pl.program_id(1)
    x = jnp.where(t == 0, xc_ref[0], xl_ref[0])
    row = _mod_row(t, b)
    h = _rms(x) * g_ref[...]
    h = (h * (1.0 + sc_ref[pl.ds(row, 1), :]) + sh_ref[pl.ds(row, 1), :]).astype(BF16)
    u = jnp.dot(h, wu_ref[...], preferred_element_type=F32)
    nk = S5_WIDTH // LANES
    blocks = ROW_TILE // S5_BLOCK
    pw = 2 * S5_GROUP
    for k in range(nk):
        r_s[k] = u[:, k * LANES:(k + 1) * LANES]
    for k in range(nk):
        xts = [r_s[k, pl.ds(tt, blocks, stride=S5_BLOCK), :] for tt in range(S5_BLOCK)]
        for j in range(LANES // pw):
            u_ref[k * (LANES // pw) + j] = jnp.concatenate([xt[:, j * pw:(j + 1) * pw] for xt in xts], axis=1)
    z_ref[0] = jnp.dot(h, wz_ref[...], preferred_element_type=F32)
    xbc_ref[0] = jnp.dot(h, wx_ref[...], preferred_element_type=F32)
    dtf_ref[0] = jnp.dot(h, wf_ref[...], preferred_element_type=F32)
    dtb_ref[0] = jnp.dot(h, wb_ref[...], preferred_element_type=F32)


def _pair_specs(lc, d):
    assert lc == ROW_TILE
    return [pl.BlockSpec((1, ROW_TILE, d), lambda b, t: (b, 0, 0)),
            pl.BlockSpec((1, ROW_TILE, d), lambda b, t: (b, jnp.maximum(t - 1, 0), 0))]


def _const_spec(shape):
    return pl.BlockSpec(shape, lambda *_: (0,) * len(shape), pipeline_mode=pl.Buffered(1))


def in_proj(xc, xl, g, sc, sh, w_in):
    bsz, lc, d = xc.shape
    n = xl.shape[1]
    nt = (lc + n) // ROW_TILE
    s1, s2, s3 = S5_WIDTH, S5_WIDTH + SSD_WIDTH, S5_WIDTH + SSD_WIDTH + SSD_XBC
    w = w_in.astype(BF16)
    ws = [w[:, :s1], w[:, s1:s2], w[:, s2:s3], w[:, s3:s3 + SSD_HEADS], w[:, s3 + SSD_HEADS:]]
    widths = [x.shape[1] for x in ws]
    blocks = ROW_TILE // S5_BLOCK
    s5_rows = bsz * (lc + n) // S5_BLOCK
    s5_lanes = 2 * S5_GROUP * S5_BLOCK
    u_spec = pl.BlockSpec((S5_PAIRS, blocks, s5_lanes),
                          lambda b, t: (0, _s5_block_rows(b, t, bsz, n // ROW_TILE), 0))
    return pl.pallas_call(
        _inproj_kernel,
        grid=(bsz, nt),
        in_specs=_pair_specs(lc, d) + [_const_spec((1, d)), _const_spec((8, d)), _const_spec((8, d))]
        + [_const_spec((d, wd)) for wd in widths],
        out_specs=[u_spec] + [pl.BlockSpec((1, ROW_TILE, wd), lambda b, t: (b, t, 0)) for wd in widths[1:]],
        out_shape=[jax.ShapeDtypeStruct((S5_PAIRS, s5_rows, s5_lanes), F32)]
        + [jax.ShapeDtypeStruct((bsz, lc + n, wd), F32) for wd in widths[1:]],
        scratch_shapes=[pltpu.VMEM((S5_WIDTH // LANES, ROW_TILE, LANES), F32)],
        compiler_params=_cparams("arbitrary", "arbitrary"),
        name="in_proj",
    )(xc, xl, g.reshape(1, d), sc, sh, *ws)


def s5_operators(lam_re, lam_im, log_step, b_re, b_im, c_re, c_im, d_skip, seg_ctx, seg_lat):
    g_, p_, h_, lk = S5_GROUPS, S5_STATE, S5_GROUP, S5_BLOCK
    j = jnp.arange(lk + 1, dtype=F32)[:, None, None]
    eye2 = jnp.eye(2, dtype=F32)
    k_dir, bst_dir, coff_dir, lam_rows = [], [], [], []
    for d in range(2):
        step = jnp.exp(log_step[d])[:, None]
        e_re, ang = lam_re[d] * step, lam_im[d] * step
        pr = jnp.exp(j * e_re) * jnp.cos(j * ang)
        pi = jnp.exp(j * e_re) * jnp.sin(j * ang)
        den = lam_re[d] * lam_re[d] + lam_im[d] * lam_im[d]
        nr = pr[1] - 1.0
        f_re = (nr * lam_re[d] + pi[1] * lam_im[d]) / den
        f_im = (pi[1] * lam_re[d] - nr * lam_im[d]) / den
        bb_re = f_re[..., None] * b_re - f_im[..., None] * b_im
        bb_im = f_re[..., None] * b_im + f_im[..., None] * b_re
        w_re = pr[:lk, :, :, None] * bb_re - pi[:lk, :, :, None] * bb_im
        w_im = pr[:lk, :, :, None] * bb_im + pi[:lk, :, :, None] * bb_re
        k_dir.append(jnp.einsum('gop,jgph->jgoh', c_re, w_re, precision=HIGHEST)
                     - jnp.einsum('gop,jgph->jgoh', c_im, w_im, precision=HIGHEST))
        order = slice(None, None, -1) if d == 0 else slice(None)
        bst_dir.append(jnp.stack([w_re[order], w_im[order]], axis=0).transpose(2, 1, 4, 0, 3))
        kk = jnp.arange(1, lk + 1) if d == 0 else jnp.arange(lk, 0, -1)
        ar, ai = pr[kk], pi[kk]
        cr = jnp.einsum('gop,tgp->gpto', c_re, ar) - jnp.einsum('gop,tgp->gpto', c_im, ai)
        ci = -(jnp.einsum('gop,tgp->gpto', c_re, ai) + jnp.einsum('gop,tgp->gpto', c_im, ar))
        coff_dir.append(jnp.stack([cr, ci], axis=1))
        for n_pow in (lk, seg_ctx, seg_lat):
            lam_rows += [jnp.exp(n_pow * e_re) * jnp.cos(n_pow * ang), jnp.exp(n_pow * e_re) * jnp.sin(n_pow * ang)]
    s_i, t_i = jnp.arange(lk)[:, None], jnp.arange(lk)[None, :]
    kf = jnp.where((t_i >= s_i)[..., None, None, None], k_dir[0][jnp.clip(t_i - s_i, 0, lk - 1)], 0.0)
    kb = jnp.where((s_i >= t_i)[..., None, None, None], k_dir[1][jnp.clip(s_i - t_i, 0, lk - 1)], 0.0)
    dk = (jnp.eye(lk, dtype=F32)[:, :, None, None, None] * jnp.eye(h_, dtype=F32)[None, None, None]
          * d_skip.reshape(g_, h_)[None, None, :, :, None])
    tz = (kf + kb + dk).transpose(2, 0, 4, 1, 3).reshape(S5_PAIRS, 2, lk * h_, lk * h_)
    bst = jnp.stack(bst_dir, axis=3).reshape(S5_PAIRS, 2, lk, h_, 2, 2, p_)
    bst = jnp.einsum('aishdqp,ij->aishdqjp', bst, eye2).reshape(S5_PAIRS, 2 * lk * h_, 8 * p_)
    coff = jnp.stack(coff_dir, axis=1).reshape(S5_PAIRS, 2, 2, 2, p_, lk, h_)
    coff = jnp.einsum('aidqpto,ij->adqipjto', coff, eye2).reshape(S5_PAIRS, 8 * p_, 2 * lk * h_)
    lam = jnp.stack([r.reshape(S5_PAIRS, 2 * p_) for r in lam_rows], axis=1)
    order12 = jnp.array([0, 1, 6, 7, 2, 3, 8, 9, 4, 5, 10, 11])
    lam = jnp.concatenate([lam[:, order12], jnp.zeros((S5_PAIRS, 4, 2 * p_), F32)], axis=1)
    return tz.astype(BF16), bst.astype(BF16), coff.astype(BF16), lam


def _s5_kernel(u_ref, tz_ref, bst_ref, coff_ref, lam_ref, y_ref, x_s, h_s, t_s, g_s, *, n_lat, n_ctx, bsz):
    nseg = SUBLANES // bsz
    u = u_ref[0]
    x_s[...] = jnp.dot(u, bst_ref[0], preferred_element_type=F32)
    lam = lam_ref[0]

    def scan(base, nsteps, rev, co, lr, li, init, store):
        def body(k, carry):
            hr, hi = carry
            q = nsteps - 1 - k if rev else k
            r0 = pl.multiple_of(base + q * SUBLANES, SUBLANES)
            if store:
                h_s[pl.ds(r0, SUBLANES), co:co + LANES] = hr
                h_s[pl.ds(r0, SUBLANES), co + LANES:co + 2 * LANES] = hi
            xr = x_s[pl.ds(r0, SUBLANES), co:co + LANES]
            xi = x_s[pl.ds(r0, SUBLANES), co + LANES:co + 2 * LANES]
            return lr * hr - li * hi + xr, lr * hi + li * hr + xi
        return lax.fori_loop(0, nsteps, body, init, unroll=4)

    def seg_carry(tot, lr, li, inits, rev):
        t_s[:, :LANES], t_s[:, LANES:] = tot
        finals = []
        for b in range(bsz):
            gr, gi = inits[b]
            for seg in (range(nseg - 1, -1, -1) if rev else range(nseg)):
                s = b * nseg + seg
                g_s[s:s + 1, :LANES], g_s[s:s + 1, LANES:] = gr, gi
                tr, ti = t_s[s:s + 1, :LANES], t_s[s:s + 1, LANES:]
                gr, gi = lr * gr - li * gi + tr, lr * gi + li * gr + ti
            finals.append((gr, gi))
        return (g_s[:, :LANES], g_s[:, LANES:]), finals

    zero8 = (jnp.zeros((SUBLANES, LANES), F32), jnp.zeros((SUBLANES, LANES), F32))
    zero1 = (jnp.zeros((1, LANES), F32), jnp.zeros((1, LANES), F32))
    for d in range(2):
        rev, co = d == 1, d * 2 * LANES
        lr = jnp.broadcast_to(lam[2 * d:2 * d + 1], (SUBLANES, LANES))
        li = jnp.broadcast_to(lam[2 * d + 1:2 * d + 2], (SUBLANES, LANES))
        init1 = [zero1] * bsz
        for base, nsteps, row in ((n_lat * SUBLANES, n_ctx, 4 + 2 * d), (0, n_lat, 8 + 2 * d)):
            tot = scan(base, nsteps, rev, co, lr, li, zero8, False)
            g0, init1 = seg_carry(tot, lam[row:row + 1], lam[row + 1:row + 2], init1, rev)
            scan(base, nsteps, rev, co, lr, li, g0, True)
    half = S5_BLOCK * S5_GROUP
    y = jnp.dot(h_s[...].astype(BF16), coff_ref[0], preferred_element_type=F32)
    y = y + jnp.concatenate([jnp.dot(u[:, :half], tz_ref[0, 0], preferred_element_type=F32),
                             jnp.dot(u[:, half:], tz_ref[0, 1], preferred_element_type=F32)], axis=1)
    y_ref[0] = jax.nn.gelu(y).astype(BF16)


def _s5_pack(part, nseg):
    bsz, n, _ = part.shape
    steps = n // (nseg * S5_BLOCK)
    p = part.reshape(bsz, nseg, steps, S5_BLOCK, S5_PAIRS, 2, S5_GROUP)
    return p.transpose(4, 2, 0, 1, 5, 3, 6).reshape(S5_PAIRS, steps * bsz * nseg, 2 * S5_BLOCK * S5_GROUP)


def _s5_unpack(rows, bsz, nseg):
    steps = rows.shape[1] // (bsz * nseg)
    p = rows.reshape(S5_PAIRS, steps, bsz, nseg, 2, S5_BLOCK, S5_GROUP)
    return p.transpose(2, 3, 1, 5, 0, 4, 6).reshape(bsz, nseg * steps * S5_BLOCK, S5_WIDTH)


def s5_mix(u, lc, ops):
    bsz, t, _ = u.shape
    assert SUBLANES % bsz == 0
    nseg = SUBLANES // bsz
    tz, bst, coff, lam = ops
    n_lat, n_ctx = (t - lc) // (nseg * S5_BLOCK), lc // (nseg * S5_BLOCK)
    rows = (n_lat + n_ctx) * SUBLANES
    width = 2 * S5_BLOCK * S5_GROUP
    up = jnp.concatenate([_s5_pack(u[:, lc:], nseg), _s5_pack(u[:, :lc], nseg)], axis=1)
    blk = lambda *shape: pl.BlockSpec((1,) + shape, lambda i: (i,) + (0,) * len(shape))
    y = pl.pallas_call(
        functools.partial(_s5_kernel, n_lat=n_lat, n_ctx=n_ctx, bsz=bsz),
        grid=(S5_PAIRS,),
        in_specs=[blk(rows, width), blk(2, width // 2, width // 2), blk(width, 4 * LANES), blk(4 * LANES, width),
                  blk(16, LANES)],
        out_specs=blk(rows, width),
        out_shape=jax.ShapeDtypeStruct((S5_PAIRS, rows, width), BF16),
        scratch_shapes=[pltpu.VMEM((rows, 4 * LANES), F32), pltpu.VMEM((rows, 4 * LANES), F32),
                        pltpu.VMEM((SUBLANES, 2 * LANES), F32), pltpu.VMEM((SUBLANES, 2 * LANES), F32)],
        compiler_params=_cparams("arbitrary"),
        name="s5_mix",
    )(up, tz, bst, coff, lam)
    nl = n_lat * SUBLANES
    return jnp.concatenate([_s5_unpack(y[:, nl:], bsz, nseg), _s5_unpack(y[:, :nl], bsz, nseg)], axis=1)


def s5_compact_operators(lam_re, lam_im, log_step, b_re, b_im, c_re, c_im, d_skip, seg_ctx, seg_lat):
    g_, p_, h_, lk = S5_GROUPS, S5_STATE, S5_GROUP, S5_BLOCK
    j = jnp.arange(lk, dtype=F32)[:, None, None]
    eye2 = jnp.eye(2, dtype=F32)
    k_dir, bb_dir, lam_rows = [], [], []
    for d in range(2):
        step = jnp.exp(log_step[d])[:, None]
        e_re, ang = lam_re[d] * step, lam_im[d] * step
        pr = jnp.exp(j * e_re) * jnp.cos(j * ang)
        pi = jnp.exp(j * e_re) * jnp.sin(j * ang)
        den = lam_re[d] * lam_re[d] + lam_im[d] * lam_im[d]
        nr = pr[1] - 1.0
        f_re = (nr * lam_re[d] + pi[1] * lam_im[d]) / den
        f_im = (pi[1] * lam_re[d] - nr * lam_im[d]) / den
        bb_re = f_re[..., None] * b_re - f_im[..., None] * b_im
        bb_im = f_re[..., None] * b_im + f_im[..., None] * b_re
        w_re = pr[:, :, :, None] * bb_re - pi[:, :, :, None] * bb_im
        w_im = pr[:, :, :, None] * bb_im + pi[:, :, :, None] * bb_re
        k_dir.append(jnp.einsum('gop,jgph->jgho', c_re, w_re, precision=HIGHEST)
                     - jnp.einsum('gop,jgph->jgho', c_im, w_im, precision=HIGHEST))
        bb_dir.append(jnp.stack([bb_re, bb_im], axis=0).transpose(1, 3, 0, 2))
        for n_pow in (lk, seg_ctx, seg_lat, 1):
            lam_rows += [jnp.exp(n_pow * e_re) * jnp.cos(n_pow * ang), jnp.exp(n_pow * e_re) * jnp.sin(n_pow * ang)]
    mid = k_dir[0][0] + k_dir[1][0] + jnp.eye(h_, dtype=F32)[None] * d_skip.reshape(g_, h_, 1)
    taps = jnp.concatenate([k_dir[1][1:][::-1], mid[None], k_dir[0][1:]], axis=0)
    taps = taps.transpose(1, 2, 0, 3).reshape(S5_PAIRS, 2, h_, 2 * lk - 1, h_)
    kfull = jnp.einsum('aihmo,ij->aihmjo', taps, eye2).reshape(S5_PAIRS, 2 * h_, (2 * lk - 1) * 2 * h_)
    kfull = jnp.pad(kfull, ((0, 0), (0, 0), (0, 2 * lk * 2 * h_ - kfull.shape[2])))
    bbp = jnp.stack(bb_dir, axis=1).reshape(S5_PAIRS, 2, 2, h_, 2, p_)
    bbp = jnp.einsum('aidhqp,ij->adihqjp', bbp, eye2).reshape(S5_PAIRS, 2, 2 * h_, 4 * p_)
    cpt = jnp.stack([c_re, -c_im], axis=2).reshape(S5_PAIRS, 2, h_, 2, p_)
    cpt = jnp.einsum('ajoqp,ij->ajoqip', cpt, eye2).reshape(S5_PAIRS, 2 * h_, 4 * p_)
    order16 = [0, 1, 8, 9, 2, 3, 10, 11, 4, 5, 12, 13, 6, 7, 14, 15]
    lam = jnp.stack([lam_rows[r].reshape(S5_PAIRS, 2 * p_) for r in order16], axis=1)
    return kfull, bbp, cpt, lam


def _s5_fused_kernel(z_ref, kf_ref, bbp_ref, cpt_ref, lam_ref, y_ref, tz_s, bst_s, cft_s, x_s, h_s, t_s, g_s,
                     *, n_lat, n_ctx, bsz):
    nseg = SUBLANES // bsz
    pw = 2 * S5_GROUP
    lam = lam_ref[0]
    kf = kf_ref[0]
    for tt in range(S5_BLOCK):
        off = (S5_BLOCK - 1 - tt) * pw
        tz_s[tt * pw:(tt + 1) * pw, :] = kf[:, off:off + S5_BLOCK * pw].astype(BF16)
    for d in range(2):
        co = d * 2 * LANES
        lr = jnp.broadcast_to(lam[12 + 2 * d:13 + 2 * d], (pw, LANES))
        li = jnp.broadcast_to(lam[13 + 2 * d:14 + 2 * d], (pw, LANES))
        order = range(S5_BLOCK - 1, -1, -1) if d == 0 else range(S5_BLOCK)
        wr, wi = bbp_ref[0, d, :, :LANES], bbp_ref[0, d, :, LANES:]
        for tt in order:
            bst_s[tt * pw:(tt + 1) * pw, co:co + LANES] = wr.astype(BF16)
            bst_s[tt * pw:(tt + 1) * pw, co + LANES:co + 2 * LANES] = wi.astype(BF16)
            wr, wi = lr * wr - li * wi, lr * wi + li * wr
        xr, xi = cpt_ref[0, :, :LANES], cpt_ref[0, :, LANES:]
        for tt in (range(S5_BLOCK) if d == 0 else range(S5_BLOCK - 1, -1, -1)):
            xr, xi = lr * xr + li * xi, lr * xi - li * xr
            cft_s[tt * pw:(tt + 1) * pw, co:co + LANES] = xr.astype(BF16)
            cft_s[tt * pw:(tt + 1) * pw, co + LANES:co + 2 * LANES] = xi.astype(BF16)

    u = z_ref[0].astype(BF16)
    x = jnp.dot(u, bst_s[...], preferred_element_type=F32)
    for k in range(4):
        x_s[k] = x[:, k * LANES:(k + 1) * LANES]

    def scan(base, nsteps, stride, rev, cr, lr, li, init, store):
        def body(k, carry):
            hr, hi = carry
            q = nsteps - 1 - k if rev else k
            rows = pl.ds(base + q, SUBLANES, stride=stride)
            if store:
                h_s[cr, rows, :] = hr
                h_s[cr + 1, rows, :] = hi
            return lr * hr - li * hi + x_s[cr, rows, :], lr * hi + li * hr + x_s[cr + 1, rows, :]
        return lax.fori_loop(0, nsteps, body, init, unroll=4)

    def seg_carry(tot, lr, li, inits, rev):
        t_s[:, :LANES], t_s[:, LANES:] = tot
        finals = []
        for b in range(bsz):
            gr, gi = inits[b]
            for seg in (range(nseg - 1, -1, -1) if rev else range(nseg)):
                s = b * nseg + seg
                g_s[s:s + 1, :LANES], g_s[s:s + 1, LANES:] = gr, gi
                tr, ti = t_s[s:s + 1, :LANES], t_s[s:s + 1, LANES:]
                gr, gi = lr * gr - li * gi + tr, lr * gi + li * gr + ti
            finals.append((gr, gi))
        return (g_s[:, :LANES], g_s[:, LANES:]), finals

    zero8 = (jnp.zeros((SUBLANES, LANES), F32), jnp.zeros((SUBLANES, LANES), F32))
    zero1 = (jnp.zeros((1, LANES), F32), jnp.zeros((1, LANES), F32))
    for d in range(2):
        rev = d == 1
        lr = jnp.broadcast_to(lam[2 * d:2 * d + 1], (SUBLANES, LANES))
        li = jnp.broadcast_to(lam[2 * d + 1:2 * d + 2], (SUBLANES, LANES))
        init1 = [zero1] * bsz
        for base, nsteps, row in ((n_lat * SUBLANES, n_ctx, 4 + 2 * d), (0, n_lat, 8 + 2 * d)):
            tot = scan(base, nsteps, nsteps, rev, 2 * d, lr, li, zero8, False)
            g0, init1 = seg_carry(tot, lam[row:row + 1], lam[row + 1:row + 2], init1, rev)
            scan(base, nsteps, nsteps, rev, 2 * d, lr, li, g0, True)
    hcat = jnp.concatenate([h_s[k] for k in range(4)], axis=1).astype(BF16)
    y = lax.dot_general(hcat, cft_s[...], (((1,), (1,)), ((), ())), preferred_element_type=F32)
    y = y + jnp.dot(u, tz_s[...], preferred_element_type=F32)
    y_ref[0] = jax.nn.gelu(y)


def s5_mix_packed(z, bsz, lc, n, ops):
    assert SUBLANES % bsz == 0
    nseg = SUBLANES // bsz
    kfull, bbp, cpt, lam = ops
    n_lat, n_ctx = n // (nseg * S5_BLOCK), lc // (nseg * S5_BLOCK)
    rows, width = z.shape[1], z.shape[2]
    assert rows == (n_lat + n_ctx) * SUBLANES
    blk = lambda *shape: pl.BlockSpec((1,) + shape, lambda i: (i,) + (0,) * len(shape))
    return pl.pallas_call(
        functools.partial(_s5_fused_kernel, n_lat=n_lat, n_ctx=n_ctx, bsz=bsz),
        grid=(S5_PAIRS,),
        in_specs=[blk(rows, width), blk(*kfull.shape[1:]), blk(*bbp.shape[1:]), blk(*cpt.shape[1:]), blk(16, LANES)],
        out_specs=blk(rows, width),
        out_shape=jax.ShapeDtypeStruct((S5_PAIRS, rows, width), F32),
        scratch_shapes=[pltpu.VMEM((width, width), BF16), pltpu.VMEM((width, 4 * LANES), BF16),
                        pltpu.VMEM((width, 4 * LANES), BF16),
                        pltpu.VMEM((4, rows, LANES), F32), pltpu.VMEM((4, rows, LANES), F32),
                        pltpu.VMEM((SUBLANES, 2 * LANES), F32), pltpu.VMEM((SUBLANES, 2 * LANES), F32)],
        compiler_params=_cparams("arbitrary"),
        name="s5_mix",
    )(z, kfull, bbp, cpt, lam)


def _glu_kernel(y_ref, wa_ref, wb_ref, ba_ref, bb_ref, o_ref, a_s):
    pw = 2 * S5_GROUP
    per = LANES // pw
    blocks = ROW_TILE // S5_BLOCK
    for tt in range(S5_BLOCK):
        for k in range(S5_WIDTH // LANES):
            a_s[k, pl.ds(tt, blocks, stride=S5_BLOCK), :] = jnp.concatenate(
                [y_ref[k * per + j, :, tt * pw:(tt + 1) * pw] for j in range(per)], axis=1)
    y = jnp.concatenate([a_s[k] for k in range(S5_WIDTH // LANES)], axis=1).astype(BF16)
    a = jnp.dot(y, wa_ref[...], preferred_element_type=F32) + ba_ref[...]
    g = jnp.dot(y, wb_ref[...], preferred_element_type=F32) + bb_ref[...]
    o_ref[0] = (a * jax.nn.sigmoid(g)).astype(BF16)


def s5_glu(y, bsz, lc, n, w_glu, b_glu):
    w = S5_WIDTH
    wb = w_glu.astype(BF16)
    blocks = ROW_TILE // S5_BLOCK
    y_spec = pl.BlockSpec((S5_PAIRS, blocks, y.shape[2]),
                          lambda b, t: (0, _s5_block_rows(b, t, bsz, n // ROW_TILE), 0))
    return pl.pallas_call(
        _glu_kernel,
        grid=(bsz, (lc + n) // ROW_TILE),
        in_specs=[y_spec, _const_spec((w, w)), _const_spec((w, w)), _const_spec((1, w)), _const_spec((1, w))],
        out_specs=pl.BlockSpec((1, ROW_TILE, w), lambda b, i: (b, i, 0)),
        out_shape=jax.ShapeDtypeStruct((bsz, lc + n, w), BF16),
        scratch_shapes=[pltpu.VMEM((w // LANES, ROW_TILE, LANES), F32)],
        compiler_params=_cparams("arbitrary", "arbitrary"),
        name="s5_glu",
    )(y, wb[:, :w], wb[:, w:], b_glu[:w].reshape(1, w), b_glu[w:].reshape(1, w))


def _softplus(x):
    return jnp.maximum(x, 0.0) + jnp.log(1.0 + jnp.exp(-jnp.abs(x)))


def _ssd_chunk_id(k, rev, nc, ncc):
    if not rev:
        return k
    return jnp.where(k < ncc, ncc - 1 - k, nc - 1 - (k - ncc))


def _ssd_kernel(*refs, rev, nc, ncc):
    if rev:
        (x_ref, xp_ref, xn_ref, dt_ref, dtt_ref, cw_ref, cb_ref, bias_ref, biast_ref, alog_ref, alogt_ref,
         z_ref, yf_ref, dsk_ref, nw_ref, o_ref, st_s, xp_s, y_s) = refs
    else:
        (x_ref, xp_ref, xn_ref, dt_ref, dtt_ref, cw_ref, cb_ref, bias_ref, biast_ref, alog_ref, alogt_ref,
         o_ref, st_s, xp_s) = refs
    lch = SSD_CHUNK
    k = pl.program_id(1)
    c = _ssd_chunk_id(k, rev, nc, ncc)

    @pl.when(k == 0)
    def _():
        st_s[...] = jnp.zeros(st_s.shape, F32)

    first = jnp.logical_or(c == 0, c == ncc)
    last = jnp.logical_or(c == ncc - 1, c == nc - 1)
    xp_s[0:SUBLANES, :] = jnp.where(first, 0.0, xp_ref[0])
    xp_s[SUBLANES:SUBLANES + lch, :] = x_ref[0]
    xp_s[SUBLANES + lch:2 * SUBLANES + lch, :] = jnp.where(last, 0.0, xn_ref[0])
    acc = cb_ref[...] + cw_ref[0:1, :] * xp_s[SUBLANES - 2:SUBLANES - 2 + lch, :]
    for tap in range(1, SSD_CONV):
        acc = acc + cw_ref[tap:tap + 1, :] * xp_s[SUBLANES - 2 + tap:SUBLANES - 2 + tap + lch, :]
    xc = acc * jax.nn.sigmoid(acc)
    gn = SSD_GROUPS * SSD_STATE
    xs = xc[:, :SSD_WIDTH]
    bm = xc[:, SSD_WIDTH:SSD_WIDTH + gn]
    cm = xc[:, SSD_WIDTH + gn:]

    dt = _softplus(dt_ref[0] + bias_ref[...])
    dtt = _softplus(dtt_ref[0] + biast_ref[...])
    da = dt * -jnp.exp(alog_ref[...])
    dat = dtt * -jnp.exp(alogt_ref[...])
    row_i = lax.broadcasted_iota(jnp.int32, (lch, lch), 0)
    col_i = lax.broadcasted_iota(jnp.int32, (lch, lch), 1)
    tri = (col_i >= row_i) if rev else (col_i <= row_i)
    trit = (row_i >= col_i) if rev else (row_i <= col_i)
    cum = jnp.dot(tri.astype(F32), da, preferred_element_type=F32, precision=HIGHEST)
    cumt = jnp.dot(dat, trit.astype(F32), preferred_element_type=F32, precision=HIGHEST)
    end = 0 if rev else lch - 1
    tot = cum[end:end + 1, :]
    wt = dtt * jnp.exp(cumt[:, end:end + 1] - cumt)
    lane = lax.broadcasted_iota(jnp.int32, (1, LANES), 1)
    low = lane < SSD_HEAD_DIM
    heads_per_group = SSD_HEADS // SSD_GROUPS
    for g in range(SSD_GROUPS):
        bg = bm[:, g * SSD_STATE:(g + 1) * SSD_STATE]
        cg = cm[:, g * SSD_STATE:(g + 1) * SSD_STATE].astype(BF16)
        scores = lax.dot_general(cg, bg.astype(BF16), (((1,), (1,)), ((), ())), preferred_element_type=F32)
        bgt = bg.T
        st = st_s[g]
        yoff = jnp.dot(cg, st.astype(BF16), preferred_element_type=F32)
        for jp in range(heads_per_group // 2):
            ha = g * heads_per_group + 2 * jp
            lo = (ha // 2) * LANES
            sl = jp * LANES
            xsp = xs[:, lo:lo + LANES]
            xblk = jnp.concatenate([jnp.where(low, xsp, 0.0), jnp.where(low, 0.0, xsp)], axis=0).astype(BF16)
            ms, ecols, lhs2 = [], [], []
            for h in (ha, ha + 1):
                col = cum[:, h:h + 1]
                seg = col - cumt[h:h + 1, :]
                dec = jnp.exp(jnp.where(tri, seg, -jnp.inf))
                ms.append((scores * dec * dtt[h:h + 1, :]).astype(BF16))
                ecols.append(jnp.exp(col))
                lhs2.append((bgt * wt[h:h + 1, :]).astype(BF16))
            ydiag = jnp.dot(jnp.concatenate(ms, axis=1), xblk, preferred_element_type=F32)
            y_pair = ydiag + jnp.where(low, ecols[0], ecols[1]) * yoff[:, sl:sl + LANES]
            upd = jnp.dot(jnp.concatenate(lhs2, axis=1), xblk, preferred_element_type=F32)
            cd = jnp.where(low, jnp.exp(tot[:, ha:ha + 1]), jnp.exp(tot[:, ha + 1:ha + 2]))
            st_s[g, :, sl:sl + LANES] = cd * st[:, sl:sl + LANES] + upd
            if rev:
                y_s[:, lo:lo + LANES] = y_pair
            else:
                o_ref[0, :, lo:lo + LANES] = y_pair
    if rev:
        y = yf_ref[0] + y_s[...] + dsk_ref[...] * xs
        z = z_ref[0]
        o_ref[0] = (_rms(y * (z * jax.nn.sigmoid(z))) * nw_ref[...]).astype(BF16)


def ssd_mix(z, xbc, dtf, dtb, lc, conv_w, conv_b, dt_bias, a_log, d_skip, norm_w):
    bsz, t, wx = xbc.shape
    lch = SSD_CHUNK
    nc, ncc = t // lch, lc // lch
    hb = lch // SUBLANES
    dsk = jnp.repeat(d_skip, SSD_HEAD_DIM).reshape(1, SSD_WIDTH)
    yf = None
    for rev in (False, True):
        cid = functools.partial(_ssd_chunk_id, rev=rev, nc=nc, ncc=ncc)
        d = int(rev)
        dt = dtb if rev else dtf
        row = lambda wd: pl.BlockSpec((1, lch, wd), lambda b, k: (b, cid(k), 0))
        in_specs = [row(wx),
                    pl.BlockSpec((1, SUBLANES, wx), lambda b, k: (b, jnp.maximum(cid(k) * hb - 1, 0), 0)),
                    pl.BlockSpec((1, SUBLANES, wx), lambda b, k: (b, jnp.minimum((cid(k) + 1) * hb, nc * hb - 1), 0)),
                    row(SSD_HEADS),
                    pl.BlockSpec((1, SSD_HEADS, lch), lambda b, k: (b, 0, cid(k))),
                    _const_spec((SSD_CONV, wx)), _const_spec((1, wx)),
                    _const_spec((1, SSD_HEADS)), _const_spec((SSD_HEADS, 1)),
                    _const_spec((1, SSD_HEADS)), _const_spec((SSD_HEADS, 1))]
        args = [xbc, xbc, xbc, dt, jnp.swapaxes(dt, 1, 2), conv_w, conv_b.reshape(1, wx),
                dt_bias[d].reshape(1, SSD_HEADS), dt_bias[d].reshape(SSD_HEADS, 1),
                a_log[d].reshape(1, SSD_HEADS), a_log[d].reshape(SSD_HEADS, 1)]
        scratch = [pltpu.VMEM((SSD_GROUPS, SSD_STATE, SSD_WIDTH // SSD_GROUPS), F32),
                   pltpu.VMEM((lch + 2 * SUBLANES, wx), F32)]
        if rev:
            in_specs += [row(SSD_WIDTH), row(SSD_WIDTH), _const_spec((1, SSD_WIDTH)), _const_spec((1, SSD_WIDTH))]
            args += [z, yf, dsk, norm_w.reshape(1, SSD_WIDTH)]
            scratch += [pltpu.VMEM((lch, SSD_WIDTH), F32)]
        out = pl.pallas_call(
            functools.partial(_ssd_kernel, rev=rev, nc=nc, ncc=ncc),
            grid=(bsz, nc),
            in_specs=in_specs,
            out_specs=row(SSD_WIDTH),
            out_shape=jax.ShapeDtypeStruct((bsz, t, SSD_WIDTH), BF16 if rev else F32),
            scratch_shapes=scratch,
            compiler_params=_cparams("arbitrary", "arbitrary"),
            name="ssd_bwd" if rev else "ssd_fwd",
        )(*args)
        yf = out
    return out


def _outproj_kernel(ys_ref, yd_ref, xc_ref, xl_ref, wt_ref, wb_ref, g1_ref, gate_ref, g2_ref, sc_ref, sh_ref,
                    wr_ref, xlo_ref, xco_ref, h2_ref, lg_ref):
    b, t = pl.program_id(0), pl.program_id(1)
    row = _mod_row(t, b)
    y = (jnp.dot(ys_ref[0], wt_ref[...], preferred_element_type=F32)
         + jnp.dot(yd_ref[0], wb_ref[...], preferred_element_type=F32))
    x = jnp.where(t == 0, xc_ref[0], xl_ref[0])
    xn = x + gate_ref[pl.ds(row, 1), :] * (_rms(y) * g1_ref[...])

    @pl.when(t == 0)
    def _():
        xco_ref[0] = xn

    @pl.when(t > 0)
    def _():
        xlo_ref[0] = xn

    h2 = _rms(xn) * g2_ref[...]
    h2 = h2 * (1.0 + sc_ref[pl.ds(row, 1), :]) + sh_ref[pl.ds(row, 1), :]
    h2_ref[0] = h2
    lg_ref[0] = jnp.dot(h2.astype(BF16), wr_ref[...], preferred_element_type=F32)


def out_proj(y_s5, y_ssd, xc, xl, w_out, g1, gate1, g2, sc2, sh2, w_router):
    bsz, lc, d = xc.shape
    n = xl.shape[1]
    nt = (lc + n) // ROW_TILE
    w = w_out.astype(BF16)
    hw = S5_WIDTH
    tile = lambda wd: pl.BlockSpec((1, ROW_TILE, wd), lambda b, t: (b, t, 0))
    pair_out = [pl.BlockSpec((1, ROW_TILE, d), lambda b, t: (b, jnp.maximum(t - 1, 0), 0)),
                pl.BlockSpec((1, ROW_TILE, d), lambda b, t: (b, 0, 0))]
    return pl.pallas_call(
        _outproj_kernel,
        grid=(bsz, nt),
        in_specs=[tile(hw), tile(SSD_WIDTH)] + _pair_specs(lc, d)
        + [_const_spec((hw, d)), _const_spec((SSD_WIDTH, d)), _const_spec((1, d)), _const_spec((8, d)),
           _const_spec((1, d)), _const_spec((8, d)), _const_spec((8, d)), _const_spec((d, N_EXPERTS))],
        out_specs=pair_out + [tile(d), tile(N_EXPERTS)],
        out_shape=[jax.ShapeDtypeStruct((bsz, n, d), F32), jax.ShapeDtypeStruct((bsz, lc, d), F32),
                   jax.ShapeDtypeStruct((bsz, lc + n, d), F32),
                   jax.ShapeDtypeStruct((bsz, lc + n, N_EXPERTS), F32)],
        compiler_params=_cparams("arbitrary", "arbitrary"),
        name="out_proj",
    )(y_s5, y_ssd, xc, xl, w[:hw], w[hw:], g1.reshape(1, d), gate1, g2.reshape(1, d), sc2, sh2,
      w_router.astype(BF16))


def _row_blocks(rows):
    return [(r0, min(MOE_ROW_BLOCK, rows - r0)) for r0 in range(0, rows, MOE_ROW_BLOCK)]


def _moe_kernel(idx_ref, h2_hbm, gate_ref, wg_ref, wu_ref, wd_ref, o_ref, xf_s, xb_s, h_s, wg_s, wu_s, wd_s, sem,
                *, rows, nf):
    e, s = pl.program_id(0), pl.program_id(1)
    ne = pl.num_programs(0)

    def row_copy(src_row, r):
        return pltpu.make_async_copy(h2_hbm.at[pl.ds(src_row, 1), :], xf_s.at[pl.ds(r, 1), :], sem.at[0])

    def issue(ex):
        def body(r, c):
            row_copy(idx_ref[ex * rows + r], r).start()
            return c
        lax.fori_loop(0, rows, body, 0, unroll=8)

    @pl.when(jnp.logical_and(e == 0, s == 0))
    def _():
        issue(0)

    @pl.when(s == 0)
    def _():
        def body(r, c):
            row_copy(0, r).wait()
            return c
        lax.fori_loop(0, rows, body, 0, unroll=8)
        xb_s[...] = xf_s[...].astype(BF16)

    @pl.when(jnp.logical_and(s == 1, e + 1 < ne))
    def _():
        issue(e + 1)

    @pl.when(s < nf)
    def _():
        wg_s[...] = wg_ref[0, 0].astype(BF16)
        wu_s[...] = wu_ref[0, 0].astype(BF16)
        for r0, rb in _row_blocks(rows):
            x = xb_s[r0:r0 + rb, :]
            g = jnp.dot(x, wg_s[...], preferred_element_type=F32)
            u = jnp.dot(x, wu_s[...], preferred_element_type=F32)
            h_s[s, r0:r0 + rb, :] = ((g * jax.nn.sigmoid(g)) * u).astype(BF16)

    @pl.when(s >= nf)
    def _():
        wd_s[...] = wd_ref[0, 0].astype(BF16)
        for r0, rb in _row_blocks(rows):
            h = jnp.concatenate([h_s[f, r0:r0 + rb, :] for f in range(nf)], axis=1)
            y = jnp.dot(h, wd_s[...], preferred_element_type=F32)
            o_ref[0, r0:r0 + rb, :] = y * gate_ref[0, r0:r0 + rb, :]


def moe_ffn(idx_rows, h2, gate, layer, w_gate, w_up, w_down):
    e, r, _ = gate.shape
    d, ff = w_gate.shape[2], w_gate.shape[3]
    tf, tn = MOE_F_TILE, MOE_N_TILE
    nf, nn = ff // tf, d // tn
    fcl = lambda s: jnp.minimum(s, nf - 1)
    ncl = lambda s: jnp.maximum(s - nf, 0)
    grid_spec = pltpu.PrefetchScalarGridSpec(
        num_scalar_prefetch=1,
        grid=(e, nf + nn),
        in_specs=[pl.BlockSpec(memory_space=pl.ANY),
                  pl.BlockSpec((1, r, 1), lambda i, s, idx: (i, 0, 0)),
                  pl.BlockSpec((1, 1, d, tf), lambda i, s, idx: (layer, i, 0, fcl(s))),
                  pl.BlockSpec((1, 1, d, tf), lambda i, s, idx: (layer, i, 0, fcl(s))),
                  pl.BlockSpec((1, 1, ff, tn), lambda i, s, idx: (layer, i, 0, ncl(s)))],
        out_specs=pl.BlockSpec((1, r, tn), lambda i, s, idx: (i, 0, ncl(s))),
        scratch_shapes=[pltpu.VMEM((r, d), F32), pltpu.VMEM((r, d), BF16), pltpu.VMEM((nf, r, tf), BF16),
                        pltpu.VMEM((d, tf), BF16), pltpu.VMEM((d, tf), BF16), pltpu.VMEM((ff, tn), BF16),
                        pltpu.SemaphoreType.DMA((1,))],
    )
    return pl.pallas_call(
        functools.partial(_moe_kernel, rows=r, nf=nf),
        grid_spec=grid_spec,
        out_shape=jax.ShapeDtypeStruct((e, r, d), F32),
        compiler_params=_cparams("arbitrary", "arbitrary"),
        name="moe_ffn",
    )(idx_rows, h2, gate, w_gate, w_up, w_down)


def _combine_kernel(src_ref, starts_ref, tokl_ref, out_hbm, *refs, lat_tiles, tiles_per_sample, total, with_ctx):
    if with_ctx:
        xc_ref, xl_ref, gate_ref, g3_ref, xlo_ref, xco_ref, buf, acc, sem = refs
    else:
        xl_ref, gate_ref, g3_ref, xlo_ref, buf, acc, sem = refs
    j = pl.program_id(0)
    nt = pl.num_programs(0)
    ch = COMBINE_CHUNK
    nch_of = lambda t: (starts_ref[t + 1] - starts_ref[t] + ch - 1) // ch

    def issue(t, c, slot):
        base = starts_ref[t] + c * ch

        def body(i, carry):
            row = src_ref[jnp.minimum(base + i, total - 1)]
            pltpu.make_async_copy(out_hbm.at[pl.ds(row, 1), :], buf.at[slot, pl.ds(i, 1), :], sem.at[slot]).start()
            return carry
        lax.fori_loop(0, ch, body, 0, unroll=8)

    nch = nch_of(j)

    @pl.when(jnp.logical_and(j == 0, nch > 0))
    def _():
        issue(j, 0, 0)

    acc[...] = jnp.zeros(acc.shape, F32)
    tok = lax.broadcasted_iota(jnp.int32, (ROW_TILE, ch), 0)

    def chunk(c, carry):
        slot = c % 2

        @pl.when(c + 1 < nch)
        def _():
            issue(j, c + 1, 1 - slot)

        def wbody(i, cc):
            pltpu.make_async_copy(out_hbm.at[pl.ds(0, 1), :], buf.at[slot, pl.ds(i, 1), :], sem.at[slot]).wait()
            return cc
        lax.fori_loop(0, ch, wbody, 0, unroll=8)
        rows = buf[slot]
        onehot = (tok == tokl_ref[0, pl.ds(c, 1), :]).astype(BF16)
        hi = rows.astype(BF16)
        lo = (rows - hi.astype(F32)).astype(BF16)
        acc[...] += (jnp.dot(onehot, hi, preferred_element_type=F32)
                     + jnp.dot(onehot, lo, preferred_element_type=F32))
        return carry
    lax.fori_loop(0, nch, chunk, 0)

    nxt = jnp.minimum(j + 1, nt - 1)

    @pl.when(jnp.logical_and(j + 1 < nt, nch_of(nxt) > 0))
    def _():
        issue(nxt, 0, 0)

    y = gate_ref[pl.ds(jnp.where(j >= lat_tiles, 2, j // tiles_per_sample), 1), :] * (_rms(acc[...]) * g3_ref[...])
    if with_ctx:
        @pl.when(j >= lat_tiles)
        def _():
            xco_ref[0] = xc_ref[0] + y

    @pl.when(j < lat_tiles)
    def _():
        xlo_ref[0] = xl_ref[0] + y


def moe_combine(src, starts, tokl, out_rows, xc, xl, gate2, g3, with_ctx):
    bsz, n, d = xl.shape
    lc = xc.shape[1]
    tps = n // ROW_TILE
    lat_tiles = bsz * tps
    nt = lat_tiles + (bsz if with_ctx else 0)
    lat_idx = lambda j: (jnp.minimum(j, lat_tiles - 1) // tps, jnp.minimum(j, lat_tiles - 1) % tps, 0)
    ctx_idx = lambda j: (jnp.clip(j - lat_tiles, 0, bsz - 1), 0, 0)
    lat_spec = pl.BlockSpec((1, ROW_TILE, d), lambda j, s0, s1: lat_idx(j))
    ctx_spec = pl.BlockSpec((1, lc, d), lambda j, s0, s1: ctx_idx(j))
    grid_spec = pltpu.PrefetchScalarGridSpec(
        num_scalar_prefetch=2,
        grid=(nt,),
        in_specs=[pl.BlockSpec((1,) + tokl.shape[1:], lambda j, s0, s1: (j, 0, 0)),
                  pl.BlockSpec(memory_space=pl.ANY)]
        + ([ctx_spec] if with_ctx else []) + [lat_spec,
                                              pl.BlockSpec((8, d), lambda j, s0, s1: (0, 0)),
                                              pl.BlockSpec((1, d), lambda j, s0, s1: (0, 0))],
        out_specs=[lat_spec] + ([ctx_spec] if with_ctx else []),
        scratch_shapes=[pltpu.VMEM((2, COMBINE_CHUNK, d), F32), pltpu.VMEM((ROW_TILE, d), F32),
                        pltpu.SemaphoreType.DMA((2,))],
    )
    res = pl.pallas_call(
        functools.partial(_combine_kernel, lat_tiles=lat_tiles, tiles_per_sample=tps, total=src.shape[0],
                          with_ctx=with_ctx),
        grid_spec=grid_spec,
        out_shape=[jax.ShapeDtypeStruct(xl.shape, F32)] + ([jax.ShapeDtypeStruct(xc.shape, F32)] if with_ctx else []),
        compiler_params=_cparams("arbitrary"),
        name="moe_combine",
    )(src, starts, tokl, out_rows, *([xc] if with_ctx else []), xl, gate2, g3.reshape(1, d))
    return (res[0], res[1]) if with_ctx else (res[0], xc)


def rmsnorm(x, w):
    xf = x.astype(F32)
    y = xf * lax.rsqrt(jnp.mean(xf * xf, axis=-1, keepdims=True) + EPS)
    return (y * w.astype(F32)).astype(x.dtype)


def grid_transpose(x):
    b, n = x.shape[:2]
    rows = n // GRID_W
    return x.reshape((b, rows, GRID_W) + x.shape[2:]).swapaxes(1, 2).reshape(x.shape)


def route(logits, lc, with_ctx):
    bsz, t, e = logits.shape
    n = t - lc
    idx_l, gate_l, tok_l = [], [], []
    for lo, m, off in [(lc, n, 0)] + ([(0, lc, bsz * n)] if with_ctx else []):
        cap = CAPACITY_FACTOR * m // e
        aff = jax.nn.softmax(logits[:, lo:lo + m], axis=-1)
        gate, idx = lax.top_k(jnp.swapaxes(aff, 1, 2), cap)
        b_off = jnp.arange(bsz, dtype=jnp.int32)[:, None, None]
        idx_l.append(jnp.swapaxes(b_off * t + lo + idx, 0, 1).reshape(e, bsz * cap))
        tok_l.append(jnp.swapaxes(off + b_off * m + idx, 0, 1).reshape(e, bsz * cap))
        gate_l.append(jnp.swapaxes(gate, 0, 1).reshape(e, bsz * cap))
    idx_rows = jnp.concatenate(idx_l, axis=1).reshape(-1).astype(jnp.int32)
    gate = jnp.concatenate(gate_l, axis=1)[..., None]
    tok = jnp.concatenate(tok_l, axis=1).reshape(-1).astype(jnp.int32)
    total = tok.shape[0]
    src = jnp.argsort(tok).astype(jnp.int32)
    tok_sorted = tok[src]
    tiles = (bsz * n + (bsz * lc if with_ctx else 0)) // ROW_TILE
    starts = jnp.searchsorted(tok_sorted, jnp.arange(tiles + 1, dtype=jnp.int32) * ROW_TILE).astype(jnp.int32)
    pos = jnp.arange(ROW_TILE * e, dtype=jnp.int32)[None, :]
    k = starts[:-1, None] + pos
    tile_lo = jnp.arange(tiles, dtype=jnp.int32)[:, None] * ROW_TILE
    tokl = jnp.where(k < starts[1:, None], tok_sorted[jnp.minimum(k, total - 1)] - tile_lo, -1)
    return idx_rows, gate, src, starts, tokl.reshape(tiles, -1, COMBINE_CHUNK)


def kernel(x, c, ctx, c_ctx, ada_w, ada_b, norm_g, w_in, w_out, s5_lam_re, s5_lam_im, s5_log_step,
           s5_b_re, s5_b_im, s5_c_re, s5_c_im, s5_d, s5_w_glu, s5_b_glu, ssd_conv_w, ssd_conv_b,
           ssd_dt_bias, ssd_a_log, ssd_d, ssd_norm, moe_router, moe_w_gate, moe_w_up, moe_w_down):
    bsz, n, d = x.shape
    lc = ctx.shape[1]
    nseg = SUBLANES // bsz
    cs = jnp.zeros((8, d), F32).at[:bsz].set(c).at[2].set(c_ctx)
    mods = adaln_all(cs, ada_w, ada_b)
    xl, xc = x, ctx
    for i in range(DEPTH):
        col_major = i % 2 == 1
        last = i == DEPTH - 1
        sh1, sc1, g1, sh2, sc2, g2 = [mods[i, :, k * d:(k + 1) * d] for k in range(N_MOD)]
        if col_major:
            xl = grid_transpose(xl)
        u, z, xbc, dtf, dtb = in_proj(xc, xl, norm_g[i, 0], sc1, sh1, w_in[i])
        ops = s5_compact_operators(s5_lam_re[i], s5_lam_im[i], s5_log_step[i], s5_b_re[i], s5_b_im[i],
                                   s5_c_re[i], s5_c_im[i], s5_d[i], lc // nseg, n // nseg)
        y_s5 = s5_glu(s5_mix_packed(u, bsz, lc, n, ops), bsz, lc, n, s5_w_glu[i], s5_b_glu[i])
        y_ssd = ssd_mix(z, xbc, dtf, dtb, lc, ssd_conv_w[i], ssd_conv_b[i], ssd_dt_bias[i],
                        ssd_a_log[i], ssd_d[i], ssd_norm[i])
        xl, xc, h2, logits = out_proj(y_s5, y_ssd, xc, xl, w_out[i], norm_g[i, 1], g1, norm_g[i, 2],
                                      sc2, sh2, moe_router[i])
        idx_rows, gate, src, starts, tokl = route(logits, lc, not last)
        out = moe_ffn(idx_rows, h2.reshape(bsz * (lc + n), d), gate, i, moe_w_gate, moe_w_up, moe_w_down)
        xl, xc = moe_combine(src, starts, tokl, out.reshape(-1, d), xc, xl, g2, norm_g[i, 3], not last)
        if col_major:
            xl = grid_transpose(xl)
    return xl
```

```python
import functools
import math

import jax
import jax.numpy as jnp
from jax import lax
from jax.experimental import pallas as pl
from jax.experimental.pallas import tpu as pltpu

D_MODEL = 2048
DEPTH = 4
GRID_W = 64
EPS = 1e-6
N_MOD = 6

S5_WIDTH = 1024
S5_GROUP = 16
S5_GROUPS = S5_WIDTH // S5_GROUP
S5_STATE = 64

SSD_WIDTH = 1024
SSD_HEAD_DIM = 64
SSD_HEADS = SSD_WIDTH // SSD_HEAD_DIM
SSD_GROUPS = 2
SSD_STATE = 128
SSD_CONV = 5
SSD_XBC = SSD_WIDTH + 2 * SSD_GROUPS * SSD_STATE

N_EXPERTS = 16
CAPACITY_FACTOR = 2
D_FF = 1536

F32 = jnp.float32
BF16 = jnp.bfloat16
HIGHEST = lax.Precision.HIGHEST

LANES = 128
SUBLANES = 8
ROW_TILE = 256
MOE_F_TILE = 256
MOE_ROW_BLOCK = 256
MOE_N_TILE = 256
COMBINE_CHUNK = 128
S5_BLOCK = 16
S5_PAIRS = S5_GROUPS // 2
SSD_CHUNK = 128
VMEM_LIMIT = 56 * 1024 * 1024


def _cparams(*sem):
    return pltpu.CompilerParams(dimension_semantics=sem, vmem_limit_bytes=VMEM_LIMIT)


def _adaln_kernel(c_ref, w_ref, b_ref, o_ref):
    c = c_ref[...]
    a = (c * jax.nn.sigmoid(c)).astype(BF16)
    o_ref[0] = jnp.dot(a, w_ref[0].astype(BF16), preferred_element_type=F32) + b_ref[0]


def adaln_all(cs, ada_w, ada_b):
    depth, d, n = ada_w.shape
    tn = 1024
    return pl.pallas_call(
        _adaln_kernel,
        grid=(depth, n // tn),
        in_specs=[pl.BlockSpec((8, d), lambda l, j: (0, 0)),
                  pl.BlockSpec((1, d, tn), lambda l, j: (l, 0, j)),
                  pl.BlockSpec((1, 1, tn), lambda l, j: (l, 0, j))],
        out_specs=pl.BlockSpec((1, 8, tn), lambda l, j: (l, 0, j)),
        out_shape=jax.ShapeDtypeStruct((depth, 8, n), F32),
        compiler_params=_cparams("arbitrary", "arbitrary"),
        name="adaln",
    )(cs, ada_w, ada_b.reshape(depth, 1, n))


def _mod_row(t, b):
    return jnp.where(t == 0, 2, b)


def _rms(x):
    return x * lax.rsqrt(jnp.mean(x * x, axis=-1, keepdims=True) + EPS)


def _s5_block_rows(b, t, bsz, lat_tiles):
    return jnp.where(t == 0, bsz * lat_tiles + b, b * lat_tiles + t - 1)


def _inproj_kernel(xc_ref, xl_ref, g_ref, sc_ref, sh_ref, wu_ref, wz_ref, wx_ref, wf_ref, wb_ref,
                   u_ref, z_ref, xbc_ref, dtf_ref, dtb_ref, r_s):
    b, t = pl.program_id(0), pl.program_id(1)
    x = jnp.where(t == 0, xc_ref[0], xl_ref[0])
    row = _mod_row(t, b)
    h = _rms(x) * g_ref[...]
    h = (h * (1.0 + sc_ref[pl.ds(row, 1), :]) + sh_ref[pl.ds(row, 1), :]).astype(BF16)
    u = jnp.dot(h, wu_ref[...], preferred_element_type=F32)
    nk = S5_WIDTH // LANES
    blocks = ROW_TILE // S5_BLOCK
    pw = 2 * S5_GROUP
    for k in range(nk):
        r_s[k] = u[:, k * LANES:(k + 1) * LANES]
    for k in range(nk):
        xts = [r_s[k, pl.ds(tt, blocks, stride=S5_BLOCK), :] for tt in range(S5_BLOCK)]
        for j in range(LANES // pw):
            u_ref[k * (LANES // pw) + j] = jnp.concatenate([xt[:, j * pw:(j + 1) * pw] for xt in xts], axis=1)
    z_ref[0] = jnp.dot(h, wz_ref[...], preferred_element_type=F32)
    xbc_ref[0] = jnp.dot(h, wx_ref[...], preferred_element_type=F32)
    dtf_ref[0] = jnp.dot(h, wf_ref[...], preferred_element_type=F32)
    dtb_ref[0] = jnp.dot(h, wb_ref[...], preferred_element_type=F32)


def _pair_specs(lc, d):
    assert lc == ROW_TILE
    return [pl.BlockSpec((1, ROW_TILE, d), lambda b, t: (b, 0, 0)),
            pl.BlockSpec((1, ROW_TILE, d), lambda b, t: (b, jnp.maximum(t - 1, 0), 0))]


def _const_spec(shape):
    return pl.BlockSpec(shape, lambda *_: (0,) * len(shape), pipeline_mode=pl.Buffered(1))


def in_proj(xc, xl, g, sc, sh, w_in):
    bsz, lc, d = xc.shape
    n = xl.shape[1]
    nt = (lc + n) // ROW_TILE
    s1, s2, s3 = S5_WIDTH, S5_WIDTH + SSD_WIDTH, S5_WIDTH + SSD_WIDTH + SSD_XBC
    w = w_in.astype(BF16)
    ws = [w[:, :s1], w[:, s1:s2], w[:, s2:s3], w[:, s3:s3 + SSD_HEADS], w[:, s3 + SSD_HEADS:]]
    widths = [x.shape[1] for x in ws]
    blocks = ROW_TILE // S5_BLOCK
    s5_rows = bsz * (lc + n) // S5_BLOCK
    s5_lanes = 2 * S5_GROUP * S5_BLOCK
    u_spec = pl.BlockSpec((S5_PAIRS, blocks, s5_lanes),
                          lambda b, t: (0, _s5_block_rows(b, t, bsz, n // ROW_TILE), 0))
    return pl.pallas_call(
        _inproj_kernel,
        grid=(bsz, nt),
        in_specs=_pair_specs(lc, d) + [_const_spec((1, d)), _const_spec((8, d)), _const_spec((8, d))]
        + [_const_spec((d, wd)) for wd in widths],
        out_specs=[u_spec] + [pl.BlockSpec((1, ROW_TILE, wd), lambda b, t: (b, t, 0)) for wd in widths[1:]],
        out_shape=[jax.ShapeDtypeStruct((S5_PAIRS, s5_rows, s5_lanes), F32)]
        + [jax.ShapeDtypeStruct((bsz, lc + n, wd), F32) for wd in widths[1:]],
        scratch_shapes=[pltpu.VMEM((S5_WIDTH // LANES, ROW_TILE, LANES), F32)],
        compiler_params=_cparams("arbitrary", "arbitrary"),
        name="in_proj",
    )(xc, xl, g.reshape(1, d), sc, sh, *ws)


def s5_operators(lam_re, lam_im, log_step, b_re, b_im, c_re, c_im, d_skip, seg_ctx, seg_lat):
    g_, p_, h_, lk = S5_GROUPS, S5_STATE, S5_GROUP, S5_BLOCK
    j = jnp.arange(lk + 1, dtype=F32)[:, None, None]
    eye2 = jnp.eye(2, dtype=F32)
    k_dir, bst_dir, coff_dir, lam_rows = [], [], [], []
    for d in range(2):
        step = jnp.exp(log_step[d])[:, None]
        e_re, ang = lam_re[d] * step, lam_im[d] * step
        pr = jnp.exp(j * e_re) * jnp.cos(j * ang)
        pi = jnp.exp(j * e_re) * jnp.sin(j * ang)
        den = lam_re[d] * lam_re[d] + lam_im[d] * lam_im[d]
        nr = pr[1] - 1.0
        f_re = (nr * lam_re[d] + pi[1] * lam_im[d]) / den
        f_im = (pi[1] * lam_re[d] - nr * lam_im[d]) / den
        bb_re = f_re[..., None] * b_re - f_im[..., None] * b_im
        bb_im = f_re[..., None] * b_im + f_im[..., None] * b_re
        w_re = pr[:lk, :, :, None] * bb_re - pi[:lk, :, :, None] * bb_im
        w_im = pr[:lk, :, :, None] * bb_im + pi[:lk, :, :, None] * bb_re
        k_dir.append(jnp.einsum('gop,jgph->jgoh', c_re, w_re, precision=HIGHEST)
                     - jnp.einsum('gop,jgph->jgoh', c_im, w_im, precision=HIGHEST))
        order = slice(None, None, -1) if d == 0 else slice(None)
        bst_dir.append(jnp.stack([w_re[order], w_im[order]], axis=0).transpose(2, 1, 4, 0, 3))
        kk = jnp.arange(1, lk + 1) if d == 0 else jnp.arange(lk, 0, -1)
        ar, ai = pr[kk], pi[kk]
        cr = jnp.einsum('gop,tgp->gpto', c_re, ar) - jnp.einsum('gop,tgp->gpto', c_im, ai)
        ci = -(jnp.einsum('gop,tgp->gpto', c_re, ai) + jnp.einsum('gop,tgp->gpto', c_im, ar))
        coff_dir.append(jnp.stack([cr, ci], axis=1))
        for n_pow in (lk, seg_ctx, seg_lat):
            lam_rows += [jnp.exp(n_pow * e_re) * jnp.cos(n_pow * ang), jnp.exp(n_pow * e_re) * jnp.sin(n_pow * ang)]
    s_i, t_i = jnp.arange(lk)[:, None], jnp.arange(lk)[None, :]
    kf = jnp.where((t_i >= s_i)[..., None, None, None], k_dir[0][jnp.clip(t_i - s_i, 0, lk - 1)], 0.0)
    kb = jnp.where((s_i >= t_i)[..., None, None, None], k_dir[1][jnp.clip(s_i - t_i, 0, lk - 1)], 0.0)
    dk = (jnp.eye(lk, dtype=F32)[:, :, None, None, None] * jnp.eye(h_, dtype=F32)[None, None, None]
          * d_skip.reshape(g_, h_)[None, None, :, :, None])
    tz = (kf + kb + dk).transpose(2, 0, 4, 1, 3).reshape(S5_PAIRS, 2, lk * h_, lk * h_)
    bst = jnp.stack(bst_dir, axis=3).reshape(S5_PAIRS, 2, lk, h_, 2, 2, p_)
    bst = jnp.einsum('aishdqp,ij->aishdqjp', bst, eye2).reshape(S5_PAIRS, 2 * lk * h_, 8 * p_)
    coff = jnp.stack(coff_dir, axis=1).reshape(S5_PAIRS, 2, 2, 2, p_, lk, h_)
    coff = jnp.einsum('aidqpto,ij->adqipjto', coff, eye2).reshape(S5_PAIRS, 8 * p_, 2 * lk * h_)
    lam = jnp.stack([r.reshape(S5_PAIRS, 2 * p_) for r in lam_rows], axis=1)
    order12 = jnp.array([0, 1, 6, 7, 2, 3, 8, 9, 4, 5, 10, 11])
    lam = jnp.concatenate([lam[:, order12], jnp.zeros((S5_PAIRS, 4, 2 * p_), F32)], axis=1)
    return tz.astype(BF16), bst.astype(BF16), coff.astype(BF16), lam


def _s5_kernel(u_ref, tz_ref, bst_ref, coff_ref, lam_ref, y_ref, x_s, h_s, t_s, g_s, *, n_lat, n_ctx, bsz):
    nseg = SUBLANES // bsz
    u = u_ref[0]
    x_s[...] = jnp.dot(u, bst_ref[0], preferred_element_type=F32)
    lam = lam_ref[0]

    def scan(base, nsteps, rev, co, lr, li, init, store):
        def body(k, carry):
            hr, hi = carry
            q = nsteps - 1 - k if rev else k
            r0 = pl.multiple_of(base + q * SUBLANES, SUBLANES)
            if store:
                h_s[pl.ds(r0, SUBLANES), co:co + LANES] = hr
                h_s[pl.ds(r0, SUBLANES), co + LANES:co + 2 * LANES] = hi
            xr = x_s[pl.ds(r0, SUBLANES), co:co + LANES]
            xi = x_s[pl.ds(r0, SUBLANES), co + LANES:co + 2 * LANES]
            return lr * hr - li * hi + xr, lr * hi + li * hr + xi
        return lax.fori_loop(0, nsteps, body, init, unroll=4)

    def seg_carry(tot, lr, li, inits, rev):
        t_s[:, :LANES], t_s[:, LANES:] = tot
        finals = []
        for b in range(bsz):
            gr, gi = inits[b]
            for seg in (range(nseg - 1, -1, -1) if rev else range(nseg)):
                s = b * nseg + seg
                g_s[s:s + 1, :LANES], g_s[s:s + 1, LANES:] = gr, gi
                tr, ti = t_s[s:s + 1, :LANES], t_s[s:s + 1, LANES:]
                gr, gi = lr * gr - li * gi + tr, lr * gi + li * gr + ti
            finals.append((gr, gi))
        return (g_s[:, :LANES], g_s[:, LANES:]), finals

    zero8 = (jnp.zeros((SUBLANES, LANES), F32), jnp.zeros((SUBLANES, LANES), F32))
    zero1 = (jnp.zeros((1, LANES), F32), jnp.zeros((1, LANES), F32))
    for d in range(2):
        rev, co = d == 1, d * 2 * LANES
        lr = jnp.broadcast_to(lam[2 * d:2 * d + 1], (SUBLANES, LANES))
        li = jnp.broadcast_to(lam[2 * d + 1:2 * d + 2], (SUBLANES, LANES))
        init1 = [zero1] * bsz
        for base, nsteps, row in ((n_lat * SUBLANES, n_ctx, 4 + 2 * d), (0, n_lat, 8 + 2 * d)):
            tot = scan(base, nsteps, rev, co, lr, li, zero8, False)
            g0, init1 = seg_carry(tot, lam[row:row + 1], lam[row + 1:row + 2], init1, rev)
            scan(base, nsteps, rev, co, lr, li, g0, True)
    half = S5_BLOCK * S5_GROUP
    y = jnp.dot(h_s[...].astype(BF16), coff_ref[0], preferred_element_type=F32)
    y = y + jnp.concatenate([jnp.dot(u[:, :half], tz_ref[0, 0], preferred_element_type=F32),
                             jnp.dot(u[:, half:], tz_ref[0, 1], preferred_element_type=F32)], axis=1)
    y_ref[0] = jax.nn.gelu(y).astype(BF16)


def _s5_pack(part, nseg):
    bsz, n, _ = part.shape
    steps = n // (nseg * S5_BLOCK)
    p = part.reshape(bsz, nseg, steps, S5_BLOCK, S5_PAIRS, 2, S5_GROUP)
    return p.transpose(4, 2, 0, 1, 5, 3, 6).reshape(S5_PAIRS, steps * bsz * nseg, 2 * S5_BLOCK * S5_GROUP)


def _s5_unpack(rows, bsz, nseg):
    steps = rows.shape[1] // (bsz * nseg)
    p = rows.reshape(S5_PAIRS, steps, bsz, nseg, 2, S5_BLOCK, S5_GROUP)
    return p.transpose(2, 3, 1, 5, 0, 4, 6).reshape(bsz, nseg * steps * S5_BLOCK, S5_WIDTH)


def s5_mix(u, lc, ops):
    bsz, t, _ = u.shape
    assert SUBLANES % bsz == 0
    nseg = SUBLANES // bsz
    tz, bst, coff, lam = ops
    n_lat, n_ctx = (t - lc) // (nseg * S5_BLOCK), lc // (nseg * S5_BLOCK)
    rows = (n_lat + n_ctx) * SUBLANES
    width = 2 * S5_BLOCK * S5_GROUP
    up = jnp.concatenate([_s5_pack(u[:, lc:], nseg), _s5_pack(u[:, :lc], nseg)], axis=1)
    blk = lambda *shape: pl.BlockSpec((1,) + shape, lambda i: (i,) + (0,) * len(shape))
    y = pl.pallas_call(
        functools.partial(_s5_kernel, n_lat=n_lat, n_ctx=n_ctx, bsz=bsz),
        grid=(S5_PAIRS,),
        in_specs=[blk(rows, width), blk(2, width // 2, width // 2), blk(width, 4 * LANES), blk(4 * LANES, width),
                  blk(16, LANES)],
        out_specs=blk(rows, width),
        out_shape=jax.ShapeDtypeStruct((S5_PAIRS, rows, width), BF16),
        scratch_shapes=[pltpu.VMEM((rows, 4 * LANES), F32), pltpu.VMEM((rows, 4 * LANES), F32),
                        pltpu.VMEM((SUBLANES, 2 * LANES), F32), pltpu.VMEM((SUBLANES, 2 * LANES), F32)],
        compiler_params=_cparams("arbitrary"),
        name="s5_mix",
    )(up, tz, bst, coff, lam)
    nl = n_lat * SUBLANES
    return jnp.concatenate([_s5_unpack(y[:, nl:], bsz, nseg), _s5_unpack(y[:, :nl], bsz, nseg)], axis=1)


def s5_compact_operators(lam_re, lam_im, log_step, b_re, b_im, c_re, c_im, d_skip, seg_ctx, seg_lat):
    g_, p_, h_, lk = S5_GROUPS, S5_STATE, S5_GROUP, S5_BLOCK
    j = jnp.arange(lk, dtype=F32)[:, None, None]
    eye2 = jnp.eye(2, dtype=F32)
    k_dir, bb_dir, lam_rows = [], [], []
    for d in range(2):
        step = jnp.exp(log_step[d])[:, None]
        e_re, ang = lam_re[d] * step, lam_im[d] * step
        pr = jnp.exp(j * e_re) * jnp.cos(j * ang)
        pi = jnp.exp(j * e_re) * jnp.sin(j * ang)
        den = lam_re[d] * lam_re[d] + lam_im[d] * lam_im[d]
        nr = pr[1] - 1.0
        f_re = (nr * lam_re[d] + pi[1] * lam_im[d]) / den
        f_im = (pi[1] * lam_re[d] - nr * lam_im[d]) / den
        bb_re = f_re[..., None] * b_re - f_im[..., None] * b_im
        bb_im = f_re[..., None] * b_im + f_im[..., None] * b_re
        w_re = pr[:, :, :, None] * bb_re - pi[:, :, :, None] * bb_im
        w_im = pr[:, :, :, None] * bb_im + pi[:, :, :, None] * bb_re
        k_dir.append(jnp.einsum('gop,jgph->jgho', c_re, w_re, precision=HIGHEST)
                     - jnp.einsum('gop,jgph->jgho', c_im, w_im, precision=HIGHEST))
        bb_dir.append(jnp.stack([bb_re, bb_im], axis=0).transpose(1, 3, 0, 2))
        for n_pow in (lk, seg_ctx, seg_lat, 1):
            lam_rows += [jnp.exp(n_pow * e_re) * jnp.cos(n_pow * ang), jnp.exp(n_pow * e_re) * jnp.sin(n_pow * ang)]
    mid = k_dir[0][0] + k_dir[1][0] + jnp.eye(h_, dtype=F32)[None] * d_skip.reshape(g_, h_, 1)
    taps = jnp.concatenate([k_dir[1][1:][::-1], mid[None], k_dir[0][1:]], axis=0)
    taps = taps.transpose(1, 2, 0, 3).reshape(S5_PAIRS, 2, h_, 2 * lk - 1, h_)
    kfull = jnp.einsum('aihmo,ij->aihmjo', taps, eye2).reshape(S5_PAIRS, 2 * h_, (2 * lk - 1) * 2 * h_)
    kfull = jnp.pad(kfull, ((0, 0), (0, 0), (0, 2 * lk * 2 * h_ - kfull.shape[2])))
    bbp = jnp.stack(bb_dir, axis=1).reshape(S5_PAIRS, 2, 2, h_, 2, p_)
    bbp = jnp.einsum('aidhqp,ij->adihqjp', bbp, eye2).reshape(S5_PAIRS, 2, 2 * h_, 4 * p_)
    cpt = jnp.stack([c_re, -c_im], axis=2).reshape(S5_PAIRS, 2, h_, 2, p_)
    cpt = jnp.einsum('ajoqp,ij->ajoqip', cpt, eye2).reshape(S5_PAIRS, 2 * h_, 4 * p_)
    order16 = [0, 1, 8, 9, 2, 3, 10, 11, 4, 5, 12, 13, 6, 7, 14, 15]
    lam = jnp.stack([lam_rows[r].reshape(S5_PAIRS, 2 * p_) for r in order16], axis=1)
    return kfull, bbp, cpt, lam


def _s5_fused_kernel(z_ref, kf_ref, bbp_ref, cpt_ref, lam_ref, y_ref, tz_s, bst_s, cft_s, x_s, h_s, t_s, g_s,
                     *, n_lat, n_ctx, bsz):
    nseg = SUBLANES // bsz
    pw = 2 * S5_GROUP
    lam = lam_ref[0]
    kf = kf_ref[0]
    for tt in range(S5_BLOCK):
        off = (S5_BLOCK - 1 - tt) * pw
        tz_s[tt * pw:(tt + 1) * pw, :] = kf[:, off:off + S5_BLOCK * pw].astype(BF16)
    for d in range(2):
        co = d * 2 * LANES
        lr = jnp.broadcast_to(lam[12 + 2 * d:13 + 2 * d], (pw, LANES))
        li = jnp.broadcast_to(lam[13 + 2 * d:14 + 2 * d], (pw, LANES))
        order = range(S5_BLOCK - 1, -1, -1) if d == 0 else range(S5_BLOCK)
        wr, wi = bbp_ref[0, d, :, :LANES], bbp_ref[0, d, :, LANES:]
        for tt in order:
            bst_s[tt * pw:(tt + 1) * pw, co:co + LANES] = wr.astype(BF16)
            bst_s[tt * pw:(tt + 1) * pw, co + LANES:co + 2 * LANES] = wi.astype(BF16)
            wr, wi = lr * wr - li * wi, lr * wi + li * wr
        xr, xi = cpt_ref[0, :, :LANES], cpt_ref[0, :, LANES:]
        for tt in (range(S5_BLOCK) if d == 0 else range(S5_BLOCK - 1, -1, -1)):
            xr, xi = lr * xr + li * xi, lr * xi - li * xr
            cft_s[tt * pw:(tt + 1) * pw, co:co + LANES] = xr.astype(BF16)
            cft_s[tt * pw:(tt + 1) * pw, co + LANES:co + 2 * LANES] = xi.astype(BF16)

    u = z_ref[0].astype(BF16)
    x = jnp.dot(u, bst_s[...], preferred_element_type=F32)
    for k in range(4):
        x_s[k] = x[:, k * LANES:(k + 1) * LANES]

    def scan(base, nsteps, stride, rev, cr, lr, li, init, store):
        def body(k, carry):
            hr, hi = carry
            q = nsteps - 1 - k if rev else k
            rows = pl.ds(base + q, SUBLANES, stride=stride)
            if store:
                h_s[cr, rows, :] = hr
                h_s[cr + 1, rows, :] = hi
            return lr * hr - li * hi + x_s[cr, rows, :], lr * hi + li * hr + x_s[cr + 1, rows, :]
        return lax.fori_loop(0, nsteps, body, init, unroll=4)

    def seg_carry(tot, lr, li, inits, rev):
        t_s[:, :LANES], t_s[:, LANES:] = tot
        finals = []
        for b in range(bsz):
            gr, gi = inits[b]
            for seg in (range(nseg - 1, -1, -1) if rev else range(nseg)):
                s = b * nseg + seg
                g_s[s:s + 1, :LANES], g_s[s:s + 1, LANES:] = gr, gi
                tr, ti = t_s[s:s + 1, :LANES], t_s[s:s + 1, LANES:]
                gr, gi = lr * gr - li * gi + tr, lr * gi + li * gr + ti
            finals.append((gr, gi))
        return (g_s[:, :LANES], g_s[:, LANES:]), finals

    zero8 = (jnp.zeros((SUBLANES, LANES), F32), jnp.zeros((SUBLANES, LANES), F32))
    zero1 = (jnp.zeros((1, LANES), F32), jnp.zeros((1, LANES), F32))
    for d in range(2):
        rev = d == 1
        lr = jnp.broadcast_to(lam[2 * d:2 * d + 1], (SUBLANES, LANES))
        li = jnp.broadcast_to(lam[2 * d + 1:2 * d + 2], (SUBLANES, LANES))
        init1 = [zero1] * bsz
        for base, nsteps, row in ((n_lat * SUBLANES, n_ctx, 4 + 2 * d), (0, n_lat, 8 + 2 * d)):
            tot = scan(base, nsteps, nsteps, rev, 2 * d, lr, li, zero8, False)
            g0, init1 = seg_carry(tot, lam[row:row + 1], lam[row + 1:row + 2], init1, rev)
            scan(base, nsteps, nsteps, rev, 2 * d, lr, li, g0, True)
    hcat = jnp.concatenate([h_s[k] for k in range(4)], axis=1).astype(BF16)
    y = lax.dot_general(hcat, cft_s[...], (((1,), (1,)), ((), ())), preferred_element_type=F32)
    y = y + jnp.dot(u, tz_s[...], preferred_element_type=F32)
    y_ref[0] = jax.nn.gelu(y)


def s5_mix_packed(z, bsz, lc, n, ops):
    assert SUBLANES % bsz == 0
    nseg = SUBLANES // bsz
    kfull, bbp, cpt, lam = ops
    n_lat, n_ctx = n // (nseg * S5_BLOCK), lc // (nseg * S5_BLOCK)
    rows, width = z.shape[1], z.shape[2]
    assert rows == (n_lat + n_ctx) * SUBLANES
    blk = lambda *shape: pl.BlockSpec((1,) + shape, lambda i: (i,) + (0,) * len(shape))
    return pl.pallas_call(
        functools.partial(_s5_fused_kernel, n_lat=n_lat, n_ctx=n_ctx, bsz=bsz),
        grid=(S5_PAIRS,),
        in_specs=[blk(rows, width), blk(*kfull.shape[1:]), blk(*bbp.shape[1:]), blk(*cpt.shape[1:]), blk(16, LANES)],
        out_specs=blk(rows, width),
        out_shape=jax.ShapeDtypeStruct((S5_PAIRS, rows, width), F32),
        scratch_shapes=[pltpu.VMEM((width, width), BF16), pltpu.VMEM((width, 4 * LANES), BF16),
                        pltpu.VMEM((width, 4 * LANES), BF16),
                        pltpu.VMEM((4, rows, LANES), F32), pltpu.VMEM((4, rows, LANES), F32),
                        pltpu.VMEM((SUBLANES, 2 * LANES), F32), pltpu.VMEM((SUBLANES, 2 * LANES), F32)],
        compiler_params=_cparams("arbitrary"),
        name="s5_mix",
    )(z, kfull, bbp, cpt, lam)


def _glu_kernel(y_ref, wa_ref, wb_ref, ba_ref, bb_ref, o_ref, a_s):
    pw = 2 * S5_GROUP
    per = LANES // pw
    blocks = ROW_TILE // S5_BLOCK
    for tt in range(S5_BLOCK):
        for k in range(S5_WIDTH // LANES):
            a_s[k, pl.ds(tt, blocks, stride=S5_BLOCK), :] = jnp.concatenate(
                [y_ref[k * per + j, :, tt * pw:(tt + 1) * pw] for j in range(per)], axis=1)
    y = jnp.concatenate([a_s[k] for k in range(S5_WIDTH // LANES)], axis=1).astype(BF16)
    a = jnp.dot(y, wa_ref[...], preferred_element_type=F32) + ba_ref[...]
    g = jnp.dot(y, wb_ref[...], preferred_element_type=F32) + bb_ref[...]
    o_ref[0] = (a * jax.nn.sigmoid(g)).astype(BF16)


def s5_glu(y, bsz, lc, n, w_glu, b_glu):
    w = S5_WIDTH
    wb = w_glu.astype(BF16)
    blocks = ROW_TILE // S5_BLOCK
    y_spec = pl.BlockSpec((S5_PAIRS, blocks, y.shape[2]),
                          lambda b, t: (0, _s5_block_rows(b, t, bsz, n // ROW_TILE), 0))
    return pl.pallas_call(
        _glu_kernel,
        grid=(bsz, (lc + n) // ROW_TILE),
        in_specs=[y_spec, _const_spec((w, w)), _const_spec((w, w)), _const_spec((1, w)), _const_spec((1, w))],
        out_specs=pl.BlockSpec((1, ROW_TILE, w), lambda b, i: (b, i, 0)),
        out_shape=jax.ShapeDtypeStruct((bsz, lc + n, w), BF16),
        scratch_shapes=[pltpu.VMEM((w // LANES, ROW_TILE, LANES), F32)],
        compiler_params=_cparams("arbitrary", "arbitrary"),
        name="s5_glu",
    )(y, wb[:, :w], wb[:, w:], b_glu[:w].reshape(1, w), b_glu[w:].reshape(1, w))


def _softplus(x):
    return jnp.maximum(x, 0.0) + jnp.log(1.0 + jnp.exp(-jnp.abs(x)))


def _ssd_chunk_id(k, rev, nc, ncc):
    if not rev:
        return k
    return jnp.where(k < ncc, ncc - 1 - k, nc - 1 - (k - ncc))


def _ssd_kernel(*refs, rev, nc, ncc):
    if rev:
        (x_ref, xp_ref, xn_ref, dt_ref, dtt_ref, cw_ref, cb_ref, bias_ref, biast_ref, alog_ref, alogt_ref,
         z_ref, yf_ref, dsk_ref, nw_ref, o_ref, st_s, xp_s, y_s) = refs
    else:
        (x_ref, xp_ref, xn_ref, dt_ref, dtt_ref, cw_ref, cb_ref, bias_ref, biast_ref, alog_ref, alogt_ref,
         o_ref, st_s, xp_s) = refs
    lch = SSD_CHUNK
    k = pl.program_id(1)
    c = _ssd_chunk_id(k, rev, nc, ncc)

    @pl.when(k == 0)
    def _():
        st_s[...] = jnp.zeros(st_s.shape, F32)

    first = jnp.logical_or(c == 0, c == ncc)
    last = jnp.logical_or(c == ncc - 1, c == nc - 1)
    xp_s[0:SUBLANES, :] = jnp.where(first, 0.0, xp_ref[0])
    xp_s[SUBLANES:SUBLANES + lch, :] = x_ref[0]
    xp_s[SUBLANES + lch:2 * SUBLANES + lch, :] = jnp.where(last, 0.0, xn_ref[0])
    acc = cb_ref[...] + cw_ref[0:1, :] * xp_s[SUBLANES - 2:SUBLANES - 2 + lch, :]
    for tap in range(1, SSD_CONV):
        acc = acc + cw_ref[tap:tap + 1, :] * xp_s[SUBLANES - 2 + tap:SUBLANES - 2 + tap + lch, :]
    xc = acc * jax.nn.sigmoid(acc)
    gn = SSD_GROUPS * SSD_STATE
    xs = xc[:, :SSD_WIDTH]
    bm = xc[:, SSD_WIDTH:SSD_WIDTH + gn]
    cm = xc[:, SSD_WIDTH + gn:]

    dt = _softplus(dt_ref[0] + bias_ref[...])
    dtt = _softplus(dtt_ref[0] + biast_ref[...])
    da = dt * -jnp.exp(alog_ref[...])
    dat = dtt * -jnp.exp(alogt_ref[...])
    row_i = lax.broadcasted_iota(jnp.int32, (lch, lch), 0)
    col_i = lax.broadcasted_iota(jnp.int32, (lch, lch), 1)
    tri = (col_i >= row_i) if rev else (col_i <= row_i)
    trit = (row_i >= col_i) if rev else (row_i <= col_i)
    cum = jnp.dot(tri.astype(F32), da, preferred_element_type=F32, precision=HIGHEST)
    cumt = jnp.dot(dat, trit.astype(F32), preferred_element_type=F32, precision=HIGHEST)
    end = 0 if rev else lch - 1
    tot = cum[end:end + 1, :]
    wt = dtt * jnp.exp(cumt[:, end:end + 1] - cumt)
    lane = lax.broadcasted_iota(jnp.int32, (1, LANES), 1)
    low = lane < SSD_HEAD_DIM
    heads_per_group = SSD_HEADS // SSD_GROUPS
    for g in range(SSD_GROUPS):
        bg = bm[:, g * SSD_STATE:(g + 1) * SSD_STATE]
        cg = cm[:, g * SSD_STATE:(g + 1) * SSD_STATE].astype(BF16)
        scores = lax.dot_general(cg, bg.astype(BF16), (((1,), (1,)), ((), ())), preferred_element_type=F32)
        bgt = bg.T
        st = st_s[g]
        yoff = jnp.dot(cg, st.astype(BF16), preferred_element_type=F32)
        for jp in range(heads_per_group // 2):
            ha = g * heads_per_group + 2 * jp
            lo = (ha // 2) * LANES
            sl = jp * LANES
            xsp = xs[:, lo:lo + LANES]
            xblk = jnp.concatenate([jnp.where(low, xsp, 0.0), jnp.where(low, 0.0, xsp)], axis=0).astype(BF16)
            ms, ecols, lhs2 = [], [], []
            for h in (ha, ha + 1):
                col = cum[:, h:h + 1]
                seg = col - cumt[h:h + 1, :]
                dec = jnp.exp(jnp.where(tri, seg, -jnp.inf))
                ms.append((scores * dec * dtt[h:h + 1, :]).astype(BF16))
                ecols.append(jnp.exp(col))
                lhs2.append((bgt * wt[h:h + 1, :]).astype(BF16))
            ydiag = jnp.dot(jnp.concatenate(ms, axis=1), xblk, preferred_element_type=F32)
            y_pair = ydiag + jnp.where(low, ecols[0], ecols[1]) * yoff[:, sl:sl + LANES]
            upd = jnp.dot(jnp.concatenate(lhs2, axis=1), xblk, preferred_element_type=F32)
            cd = jnp.where(low, jnp.exp(tot[:, ha:ha + 1]), jnp.exp(tot[:, ha + 1:ha + 2]))
            st_s[g, :, sl:sl + LANES] = cd * st[:, sl:sl + LANES] + upd
            if rev:
                y_s[:, lo:lo + LANES] = y_pair
            else:
                o_ref[0, :, lo:lo + LANES] = y_pair
    if rev:
        y = yf_ref[0] + y_s[...] + dsk_ref[...] * xs
        z = z_ref[0]
        o_ref[0] = (_rms(y * (z * jax.nn.sigmoid(z))) * nw_ref[...]).astype(BF16)


def ssd_mix(z, xbc, dtf, dtb, lc, conv_w, conv_b, dt_bias, a_log, d_skip, norm_w):
    bsz, t, wx = xbc.shape
    lch = SSD_CHUNK
    nc, ncc = t // lch, lc // lch
    hb = lch // SUBLANES
    dsk = jnp.repeat(d_skip, SSD_HEAD_DIM).reshape(1, SSD_WIDTH)
    yf = None
    for rev in (False, True):
        cid = functools.partial(_ssd_chunk_id, rev=rev, nc=nc, ncc=ncc)
        d = int(rev)
        dt = dtb if rev else dtf
        row = lambda wd: pl.BlockSpec((1, lch, wd), lambda b, k: (b, cid(k), 0))
        in_specs = [row(wx),
                    pl.BlockSpec((1, SUBLANES, wx), lambda b, k: (b, jnp.maximum(cid(k) * hb - 1, 0), 0)),
                    pl.BlockSpec((1, SUBLANES, wx), lambda b, k: (b, jnp.minimum((cid(k) + 1) * hb, nc * hb - 1), 0)),
                    row(SSD_HEADS),
                    pl.BlockSpec((1, SSD_HEADS, lch), lambda b, k: (b, 0, cid(k))),
                    _const_spec((SSD_CONV, wx)), _const_spec((1, wx)),
                    _const_spec((1, SSD_HEADS)), _const_spec((SSD_HEADS, 1)),
                    _const_spec((1, SSD_HEADS)), _const_spec((SSD_HEADS, 1))]
        args = [xbc, xbc, xbc, dt, jnp.swapaxes(dt, 1, 2), conv_w, conv_b.reshape(1, wx),
                dt_bias[d].reshape(1, SSD_HEADS), dt_bias[d].reshape(SSD_HEADS, 1),
                a_log[d].reshape(1, SSD_HEADS), a_log[d].reshape(SSD_HEADS, 1)]
        scratch = [pltpu.VMEM((SSD_GROUPS, SSD_STATE, SSD_WIDTH // SSD_GROUPS), F32),
                   pltpu.VMEM((lch + 2 * SUBLANES, wx), F32)]
        if rev:
            in_specs += [row(SSD_WIDTH), row(SSD_WIDTH), _const_spec((1, SSD_WIDTH)), _const_spec((1, SSD_WIDTH))]
            args += [z, yf, dsk, norm_w.reshape(1, SSD_WIDTH)]
            scratch += [pltpu.VMEM((lch, SSD_WIDTH), F32)]
        out = pl.pallas_call(
            functools.partial(_ssd_kernel, rev=rev, nc=nc, ncc=ncc),
            grid=(bsz, nc),
            in_specs=in_specs,
            out_specs=row(SSD_WIDTH),
            out_shape=jax.ShapeDtypeStruct((bsz, t, SSD_WIDTH), BF16 if rev else F32),
            scratch_shapes=scratch,
            compiler_params=_cparams("arbitrary", "arbitrary"),
            name="ssd_bwd" if rev else "ssd_fwd",
        )(*args)
        yf = out
    return out


def _outproj_kernel(ys_ref, yd_ref, xc_ref, xl_ref, wt_ref, wb_ref, g1_ref, gate_ref, g2_ref, sc_ref, sh_ref,
                    wr_ref, xlo_ref, xco_ref, h2_ref, lg_ref):
    b, t = pl.program_id(0), pl.program_id(1)
    row = _mod_row(t, b)
    y = (jnp.dot(ys_ref[0], wt_ref[...], preferred_element_type=F32)
         + jnp.dot(yd_ref[0], wb_ref[...], preferred_element_type=F32))
    x = jnp.where(t == 0, xc_ref[0], xl_ref[0])
    xn = x + gate_ref[pl.ds(row, 1), :] * (_rms(y) * g1_ref[...])

    @pl.when(t == 0)
    def _():
        xco_ref[0] = xn

    @pl.when(t > 0)
    def _():
        xlo_ref[0] = xn

    h2 = _rms(xn) * g2_ref[...]
    h2 = h2 * (1.0 + sc_ref[pl.ds(row, 1), :]) + sh_ref[pl.ds(row, 1), :]
    h2_ref[0] = h2
    lg_ref[0] = jnp.dot(h2.astype(BF16), wr_ref[...], preferred_element_type=F32)


def out_proj(y_s5, y_ssd, xc, xl, w_out, g1, gate1, g2, sc2, sh2, w_router):
    bsz, lc, d = xc.shape
    n = xl.shape[1]
    nt = (lc + n) // ROW_TILE
    w = w_out.astype(BF16)
    hw = S5_WIDTH
    tile = lambda wd: pl.BlockSpec((1, ROW_TILE, wd), lambda b, t: (b, t, 0))
    pair_out = [pl.BlockSpec((1, ROW_TILE, d), lambda b, t: (b, jnp.maximum(t - 1, 0), 0)),
                pl.BlockSpec((1, ROW_TILE, d), lambda b, t: (b, 0, 0))]
    return pl.pallas_call(
        _outproj_kernel,
        grid=(bsz, nt),
        in_specs=[tile(hw), tile(SSD_WIDTH)] + _pair_specs(lc, d)
        + [_const_spec((hw, d)), _const_spec((SSD_WIDTH, d)), _const_spec((1, d)), _const_spec((8, d)),
           _const_spec((1, d)), _const_spec((8, d)), _const_spec((8, d)), _const_spec((d, N_EXPERTS))],
        out_specs=pair_out + [tile(d), tile(N_EXPERTS)],
        out_shape=[jax.ShapeDtypeStruct((bsz, n, d), F32), jax.ShapeDtypeStruct((bsz, lc, d), F32),
                   jax.ShapeDtypeStruct((bsz, lc + n, d), F32),
                   jax.ShapeDtypeStruct((bsz, lc + n, N_EXPERTS), F32)],
        compiler_params=_cparams("arbitrary", "arbitrary"),
        name="out_proj",
    )(y_s5, y_ssd, xc, xl, w[:hw], w[hw:], g1.reshape(1, d), gate1, g2.reshape(1, d), sc2, sh2,
      w_router.astype(BF16))


def _row_blocks(rows):
    return [(r0, min(MOE_ROW_BLOCK, rows - r0)) for r0 in range(0, rows, MOE_ROW_BLOCK)]


def _moe_kernel(idx_ref, h2_hbm, gate_ref, wg_ref, wu_ref, wd_ref, o_ref, xf_s, xb_s, h_s, wg_s, wu_s, wd_s, sem,
                *, rows, nf, nsteps):
    e, s = pl.program_id(0), pl.program_id(1)
    ne = pl.num_programs(0)

    def row_copy(src_row, r):
        return pltpu.make_async_copy(h2_hbm.at[pl.ds(src_row, 1), :], xf_s.at[pl.ds(r, 1), :], sem.at[0])

    def issue(ex, lo, hi):
        def body(r, c):
            row_copy(idx_ref[ex * rows + r], r).start()
            return c
        lax.fori_loop(lo, hi, body, 0)

    @pl.when(jnp.logical_and(e == 0, s == 0))
    def _():
        issue(0, 0, rows)

    @pl.when(s == 0)
    def _():
        def body(r, c):
            row_copy(0, r).wait()
            return c
        lax.fori_loop(0, rows, body, 0, unroll=8)
        xb_s[...] = xf_s[...].astype(BF16)

    per_step = -(-rows // (nsteps - 1))

    @pl.when(jnp.logical_and(s >= 1, e + 1 < ne))
    def _():
        issue(e + 1, jnp.minimum((s - 1) * per_step, rows), jnp.minimum(s * per_step, rows))

    @pl.when(s < nf)
    def _():
        wg_s[...] = wg_ref[0, 0].astype(BF16)
        wu_s[...] = wu_ref[0, 0].astype(BF16)
        for r0, rb in _row_blocks(rows):
            x = xb_s[r0:r0 + rb, :]
            g = jnp.dot(x, wg_s[...], preferred_element_type=F32)
            u = jnp.dot(x, wu_s[...], preferred_element_type=F32)
            h_s[s, r0:r0 + rb, :] = ((g * jax.nn.sigmoid(g)) * u).astype(BF16)

    @pl.when(s >= nf)
    def _():
        wd_s[...] = wd_ref[0, 0].astype(BF16)
        for r0, rb in _row_blocks(rows):
            h = jnp.concatenate([h_s[f, r0:r0 + rb, :] for f in range(nf)], axis=1)
            y = jnp.dot(h, wd_s[...], preferred_element_type=F32)
            o_ref[0, r0:r0 + rb, :] = y * gate_ref[0, r0:r0 + rb, :]


def moe_ffn(idx_rows, h2, gate, layer, w_gate, w_up, w_down):
    e, r, _ = gate.shape
    d, ff = w_gate.shape[2], w_gate.shape[3]
    tf, tn = MOE_F_TILE, MOE_N_TILE
    nf, nn = ff // tf, d // tn
    fcl = lambda s: jnp.minimum(s, nf - 1)
    ncl = lambda s: jnp.maximum(s - nf, 0)
    grid_spec = pltpu.PrefetchScalarGridSpec(
        num_scalar_prefetch=1,
        grid=(e, nf + nn),
        in_specs=[pl.BlockSpec(memory_space=pl.ANY),
                  pl.BlockSpec((1, r, 1), lambda i, s, idx: (i, 0, 0)),
                  pl.BlockSpec((1, 1, d, tf), lambda i, s, idx: (layer, i, 0, fcl(s))),
                  pl.BlockSpec((1, 1, d, tf), lambda i, s, idx: (layer, i, 0, fcl(s))),
                  pl.BlockSpec((1, 1, ff, tn), lambda i, s, idx: (layer, i, 0, ncl(s)))],
        out_specs=pl.BlockSpec((1, r, tn), lambda i, s, idx: (i, 0, ncl(s))),
        scratch_shapes=[pltpu.VMEM((r, d), F32), pltpu.VMEM((r, d), BF16), pltpu.VMEM((nf, r, tf), BF16),
                        pltpu.VMEM((d, tf), BF16), pltpu.VMEM((d, tf), BF16), pltpu.VMEM((ff, tn), BF16),
                        pltpu.SemaphoreType.DMA((1,))],
    )
    return pl.pallas_call(
        functools.partial(_moe_kernel, rows=r, nf=nf, nsteps=nf + nn),
        grid_spec=grid_spec,
        out_shape=jax.ShapeDtypeStruct((e, r, d), F32),
        compiler_params=_cparams("arbitrary", "arbitrary"),
        name="moe_ffn",
    )(idx_rows, h2, gate, w_gate, w_up, w_down)


def _combine_kernel(src_ref, starts_ref, tokl_ref, out_hbm, *refs, lat_tiles, tiles_per_sample, total, with_ctx):
    if with_ctx:
        xc_ref, xl_ref, gate_ref, g3_ref, xlo_ref, xco_ref, buf, acc, sem = refs
    else:
        xl_ref, gate_ref, g3_ref, xlo_ref, buf, acc, sem = refs
    j = pl.program_id(0)
    nt = pl.num_programs(0)
    ch = COMBINE_CHUNK
    first_of = lambda t: starts_ref[t] // ch
    nch_of = lambda t: jnp.where(starts_ref[t + 1] > starts_ref[t],
                                 (starts_ref[t + 1] + ch - 1) // ch - starts_ref[t] // ch, 0)

    def issue(t, c, slot):
        base = (first_of(t) + c) * ch

        def body(i, carry):
            for p in range(2):
                k = 2 * i + p
                pltpu.make_async_copy(out_hbm.at[pl.ds(src_ref[base + k], 1), :], buf.at[slot, pl.ds(k, 1), :],
                                      sem.at[slot]).start(priority=p)
            return carry
        lax.fori_loop(0, ch // 2, body, 0, unroll=4)

    nch = nch_of(j)

    @pl.when(jnp.logical_and(j == 0, nch > 0))
    def _():
        issue(j, 0, 0)

    acc[...] = jnp.zeros(acc.shape, F32)
    tok = lax.broadcasted_iota(jnp.int32, (ROW_TILE, ch), 0) + j * ROW_TILE

    def chunk(c, carry):
        slot = c % 2

        @pl.when(c + 1 < nch)
        def _():
            issue(j, c + 1, 1 - slot)

        def wbody(i, cc):
            pltpu.make_async_copy(out_hbm.at[pl.ds(0, 1), :], buf.at[slot, pl.ds(i, 1), :], sem.at[slot]).wait()
            return cc
        lax.fori_loop(0, ch, wbody, 0, unroll=8)
        rows = buf[slot]
        onehot = (tok == tokl_ref[pl.ds(first_of(j) + c, 1), :]).astype(BF16)
        hi = rows.astype(BF16)
        lo = (rows - hi.astype(F32)).astype(BF16)
        acc[...] += (jnp.dot(onehot, hi, preferred_element_type=F32)
                     + jnp.dot(onehot, lo, preferred_element_type=F32))
        return carry
    lax.fori_loop(0, nch, chunk, 0)

    nxt = jnp.minimum(j + 1, nt - 1)

    @pl.when(jnp.logical_and(j + 1 < nt, nch_of(nxt) > 0))
    def _():
        issue(nxt, 0, 0)

    y = gate_ref[pl.ds(jnp.where(j >= lat_tiles, 2, j // tiles_per_sample), 1), :] * (_rms(acc[...]) * g3_ref[...])
    if with_ctx:
        @pl.when(j >= lat_tiles)
        def _():
            xco_ref[0] = xc_ref[0] + y

    @pl.when(j < lat_tiles)
    def _():
        xlo_ref[0] = xl_ref[0] + y


def moe_combine(src, starts, tokl, out_rows, xc, xl, gate2, g3, with_ctx):
    bsz, n, d = xl.shape
    lc = xc.shape[1]
    tps = n // ROW_TILE
    lat_tiles = bsz * tps
    nt = lat_tiles + (bsz if with_ctx else 0)
    lat_idx = lambda j: (jnp.minimum(j, lat_tiles - 1) // tps, jnp.minimum(j, lat_tiles - 1) % tps, 0)
    ctx_idx = lambda j: (jnp.clip(j - lat_tiles, 0, bsz - 1), 0, 0)
    lat_spec = pl.BlockSpec((1, ROW_TILE, d), lambda j, s0, s1: lat_idx(j))
    ctx_spec = pl.BlockSpec((1, lc, d), lambda j, s0, s1: ctx_idx(j))
    grid_spec = pltpu.PrefetchScalarGridSpec(
        num_scalar_prefetch=2,
        grid=(nt,),
        in_specs=[pl.BlockSpec(tokl.shape, lambda j, s0, s1: (0, 0)),
                  pl.BlockSpec(memory_space=pl.ANY)]
        + ([ctx_spec] if with_ctx else []) + [lat_spec,
                                              pl.BlockSpec((8, d), lambda j, s0, s1: (0, 0)),
                                              pl.BlockSpec((1, d), lambda j, s0, s1: (0, 0))],
        out_specs=[lat_spec] + ([ctx_spec] if with_ctx else []),
        scratch_shapes=[pltpu.VMEM((2, COMBINE_CHUNK, d), F32), pltpu.VMEM((ROW_TILE, d), F32),
                        pltpu.SemaphoreType.DMA((2,))],
    )
    res = pl.pallas_call(
        functools.partial(_combine_kernel, lat_tiles=lat_tiles, tiles_per_sample=tps, total=src.shape[0],
                          with_ctx=with_ctx),
        grid_spec=grid_spec,
        out_shape=[jax.ShapeDtypeStruct(xl.shape, F32)] + ([jax.ShapeDtypeStruct(xc.shape, F32)] if with_ctx else []),
        compiler_params=_cparams("arbitrary"),
        name="moe_combine",
    )(src, starts, tokl, out_rows, *([xc] if with_ctx else []), xl, gate2, g3.reshape(1, d))
    return (res[0], res[1]) if with_ctx else (res[0], xc)


def rmsnorm(x, w):
    xf = x.astype(F32)
    y = xf * lax.rsqrt(jnp.mean(xf * xf, axis=-1, keepdims=True) + EPS)
    return (y * w.astype(F32)).astype(x.dtype)


def grid_transpose(x):
    b, n = x.shape[:2]
    rows = n // GRID_W
    return x.reshape((b, rows, GRID_W) + x.shape[2:]).swapaxes(1, 2).reshape(x.shape)


def route(logits, lc, with_ctx):
    bsz, t, e = logits.shape
    n = t - lc
    idx_l, gate_l, tok_l = [], [], []
    for lo, m, off in [(lc, n, 0)] + ([(0, lc, bsz * n)] if with_ctx else []):
        cap = CAPACITY_FACTOR * m // e
        aff = jax.nn.softmax(logits[:, lo:lo + m], axis=-1)
        gate, idx = lax.top_k(jnp.swapaxes(aff, 1, 2), cap)
        b_off = jnp.arange(bsz, dtype=jnp.int32)[:, None, None]
        idx_l.append(jnp.swapaxes(b_off * t + lo + idx, 0, 1).reshape(e, bsz * cap))
        tok_l.append(jnp.swapaxes(off + b_off * m + idx, 0, 1).reshape(e, bsz * cap))
        gate_l.append(jnp.swapaxes(gate, 0, 1).reshape(e, bsz * cap))
    idx_rows = jnp.concatenate(idx_l, axis=1).reshape(-1).astype(jnp.int32)
    gate = jnp.concatenate(gate_l, axis=1)[..., None]
    tok = jnp.concatenate(tok_l, axis=1).reshape(-1).astype(jnp.int32)
    total = tok.shape[0]
    assert total % COMBINE_CHUNK == 0
    tok_sorted, src = lax.sort((tok, jnp.arange(total, dtype=jnp.int32)), num_keys=1)
    tiles = (bsz * n + (bsz * lc if with_ctx else 0)) // ROW_TILE
    bounds = jnp.arange(tiles + 1, dtype=jnp.int32) * ROW_TILE
    starts = jnp.sum((tok[None, :] < bounds[:, None]).astype(jnp.int32), axis=1)
    return idx_rows, gate, src, starts, tok_sorted.reshape(total // COMBINE_CHUNK, COMBINE_CHUNK)


def kernel(x, c, ctx, c_ctx, ada_w, ada_b, norm_g, w_in, w_out, s5_lam_re, s5_lam_im, s5_log_step,
           s5_b_re, s5_b_im, s5_c_re, s5_c_im, s5_d, s5_w_glu, s5_b_glu, ssd_conv_w, ssd_conv_b,
           ssd_dt_bias, ssd_a_log, ssd_d, ssd_norm, moe_router, moe_w_gate, moe_w_up, moe_w_down):
    bsz, n, d = x.shape
    lc = ctx.shape[1]
    nseg = SUBLANES // bsz
    cs = jnp.zeros((8, d), F32).at[:bsz].set(c).at[2].set(c_ctx)
    mods = adaln_all(cs, ada_w, ada_b)
    xl, xc = x, ctx
    for i in range(DEPTH):
        col_major = i % 2 == 1
        last = i == DEPTH - 1
        sh1, sc1, g1, sh2, sc2, g2 = [mods[i, :, k * d:(k + 1) * d] for k in range(N_MOD)]
        if col_major:
            xl = grid_transpose(xl)
        u, z, xbc, dtf, dtb = in_proj(xc, xl, norm_g[i, 0], sc1, sh1, w_in[i])
        ops = s5_compact_operators(s5_lam_re[i], s5_lam_im[i], s5_log_step[i], s5_b_re[i], s5_b_im[i],
                                   s5_c_re[i], s5_c_im[i], s5_d[i], lc // nseg, n // nseg)
        y_s5 = s5_glu(s5_mix_packed(u, bsz, lc, n, ops), bsz, lc, n, s5_w_glu[i], s5_b_glu[i])
        y_ssd = ssd_mix(z, xbc, dtf, dtb, lc, ssd_conv_w[i], ssd_conv_b[i], ssd_dt_bias[i],
                        ssd_a_log[i], ssd_d[i], ssd_norm[i])
        xl, xc, h2, logits = out_proj(y_s5, y_ssd, xc, xl, w_out[i], norm_g[i, 1], g1, norm_g[i, 2],
                                      sc2, sh2, moe_router[i])
        idx_rows, gate, src, starts, tokl = route(logits, lc, not last)
        out = moe_ffn(idx_rows, h2.reshape(bsz * (lc + n), d), gate, i, moe_w_gate, moe_w_up, moe_w_down)
        xl, xc = moe_combine(src, starts, tokl, out.reshape(-1, d), xc, xl, g2, norm_g[i, 3], not last)
        if col_major:
            xl = grid_transpose(xl)
    return xl
```

```python
import functools
import math

import jax
import jax.numpy as jnp
from jax import lax
from jax.experimental import pallas as pl
from jax.experimental.pallas import tpu as pltpu

D_MODEL = 2048
DEPTH = 4
GRID_W = 64
EPS = 1e-6
N_MOD = 6

S5_WIDTH = 1024
S5_GROUP = 16
S5_GROUPS = S5_WIDTH // S5_GROUP
S5_STATE = 64

SSD_WIDTH = 1024
SSD_HEAD_DIM = 64
SSD_HEADS = SSD_WIDTH // SSD_HEAD_DIM
SSD_GROUPS = 2
SSD_STATE = 128
SSD_CONV = 5
SSD_XBC = SSD_WIDTH + 2 * SSD_GROUPS * SSD_STATE

N_EXPERTS = 16
CAPACITY_FACTOR = 2
D_FF = 1536

F32 = jnp.float32
BF16 = jnp.bfloat16
HIGHEST = lax.Precision.HIGHEST

LANES = 128
SUBLANES = 8
ROW_TILE = 256
MOE_F_TILE = 256
MOE_ROW_BLOCK = 256
MOE_N_TILE = 1024
MOE_VMEM_LIMIT = 60 * 1024 * 1024
COMBINE_CHUNK = 128
S5_BLOCK = 16
S5_PAIRS = S5_GROUPS // 2
SSD_CHUNK = 128
VMEM_LIMIT = 56 * 1024 * 1024


def _cparams(*sem):
    return pltpu.CompilerParams(dimension_semantics=sem, vmem_limit_bytes=VMEM_LIMIT)


def _adaln_kernel(c_ref, w_ref, b_ref, o_ref):
    c = c_ref[...]
    a = (c * jax.nn.sigmoid(c)).astype(BF16)
    o_ref[0] = jnp.dot(a, w_ref[0].astype(BF16), preferred_element_type=F32) + b_ref[0]


def adaln_all(cs, ada_w, ada_b):
    depth, d, n = ada_w.shape
    tn = 1024
    return pl.pallas_call(
        _adaln_kernel,
        grid=(depth, n // tn),
        in_specs=[pl.BlockSpec((8, d), lambda l, j: (0, 0)),
                  pl.BlockSpec((1, d, tn), lambda l, j: (l, 0, j)),
                  pl.BlockSpec((1, 1, tn), lambda l, j: (l, 0, j))],
        out_specs=pl.BlockSpec((1, 8, tn), lambda l, j: (l, 0, j)),
        out_shape=jax.ShapeDtypeStruct((depth, 8, n), F32),
        compiler_params=_cparams("arbitrary", "arbitrary"),
        name="adaln",
    )(cs, ada_w, ada_b.reshape(depth, 1, n))


def _mod_row(t, b):
    return jnp.where(t == 0, 2, b)


def _rms(x):
    return x * lax.rsqrt(jnp.mean(x * x, axis=-1, keepdims=True) + EPS)


def _s5_block_rows(b, t, bsz, lat_tiles):
    return jnp.where(t == 0, bsz * lat_tiles + b, b * lat_tiles + t - 1)


def _inproj_kernel(xc_ref, xl_ref, g_ref, sc_ref, sh_ref, wu_ref, wz_ref, wx_ref, wf_ref, wb_ref,
                   u_ref, z_ref, xbc_ref, dtf_ref, dtb_ref, r_s):
    b, t = pl.program_id(0), pl.program_id(1)
    x = jnp.where(t == 0, xc_ref[0], xl_ref[0])
    row = _mod_row(t, b)
    h = _rms(x) * g_ref[...]
    h = (h * (1.0 + sc_ref[pl.ds(row, 1), :]) + sh_ref[pl.ds(row, 1), :]).astype(BF16)
    u = jnp.dot(h, wu_ref[...], preferred_element_type=F32)
    nk = S5_WIDTH // LANES
    blocks = ROW_TILE // S5_BLOCK
    pw = 2 * S5_GROUP
    for k in range(nk):
        r_s[k] = u[:, k * LANES:(k + 1) * LANES]
    for k in range(nk):
        xts = [r_s[k, pl.ds(tt, blocks, stride=S5_BLOCK), :] for tt in range(S5_BLOCK)]
        for j in range(LANES // pw):
            u_ref[k * (LANES // pw) + j] = jnp.concatenate([xt[:, j * pw:(j + 1) * pw] for xt in xts], axis=1)
    z_ref[0] = jnp.dot(h, wz_ref[...], preferred_element_type=F32)
    xbc_ref[0] = jnp.dot(h, wx_ref[...], preferred_element_type=F32)
    dtf_ref[0] = jnp.dot(h, wf_ref[...], preferred_element_type=F32)
    dtb_ref[0] = jnp.dot(h, wb_ref[...], preferred_element_type=F32)


def _pair_specs(lc, d):
    assert lc == ROW_TILE
    return [pl.BlockSpec((1, ROW_TILE, d), lambda b, t: (b, 0, 0)),
            pl.BlockSpec((1, ROW_TILE, d), lambda b, t: (b, jnp.maximum(t - 1, 0), 0))]


def _const_spec(shape):
    return pl.BlockSpec(shape, lambda *_: (0,) * len(shape), pipeline_mode=pl.Buffered(1))


def in_proj(xc, xl, g, sc, sh, w_in):
    bsz, lc, d = xc.shape
    n = xl.shape[1]
    nt = (lc + n) // ROW_TILE
    s1, s2, s3 = S5_WIDTH, S5_WIDTH + SSD_WIDTH, S5_WIDTH + SSD_WIDTH + SSD_XBC
    w = w_in.astype(BF16)
    ws = [w[:, :s1], w[:, s1:s2], w[:, s2:s3], w[:, s3:s3 + SSD_HEADS], w[:, s3 + SSD_HEADS:]]
    widths = [x.shape[1] for x in ws]
    blocks = ROW_TILE // S5_BLOCK
    s5_rows = bsz * (lc + n) // S5_BLOCK
    s5_lanes = 2 * S5_GROUP * S5_BLOCK
    u_spec = pl.BlockSpec((S5_PAIRS, blocks, s5_lanes),
                          lambda b, t: (0, _s5_block_rows(b, t, bsz, n // ROW_TILE), 0))
    return pl.pallas_call(
        _inproj_kernel,
        grid=(bsz, nt),
        in_specs=_pair_specs(lc, d) + [_const_spec((1, d)), _const_spec((8, d)), _const_spec((8, d))]
        + [_const_spec((d, wd)) for wd in widths],
        out_specs=[u_spec] + [pl.BlockSpec((1, ROW_TILE, wd), lambda b, t: (b, t, 0)) for wd in widths[1:]],
        out_shape=[jax.ShapeDtypeStruct((S5_PAIRS, s5_rows, s5_lanes), F32)]
        + [jax.ShapeDtypeStruct((bsz, lc + n, wd), F32) for wd in widths[1:]],
        scratch_shapes=[pltpu.VMEM((S5_WIDTH // LANES, ROW_TILE, LANES), F32)],
        compiler_params=_cparams("arbitrary", "arbitrary"),
        name="in_proj",
    )(xc, xl, g.reshape(1, d), sc, sh, *ws)


def s5_operators(lam_re, lam_im, log_step, b_re, b_im, c_re, c_im, d_skip, seg_ctx, seg_lat):
    g_, p_, h_, lk = S5_GROUPS, S5_STATE, S5_GROUP, S5_BLOCK
    j = jnp.arange(lk + 1, dtype=F32)[:, None, None]
    eye2 = jnp.eye(2, dtype=F32)
    k_dir, bst_dir, coff_dir, lam_rows = [], [], [], []
    for d in range(2):
        step = jnp.exp(log_step[d])[:, None]
        e_re, ang = lam_re[d] * step, lam_im[d] * step
        pr = jnp.exp(j * e_re) * jnp.cos(j * ang)
        pi = jnp.exp(j * e_re) * jnp.sin(j * ang)
        den = lam_re[d] * lam_re[d] + lam_im[d] * lam_im[d]
        nr = pr[1] - 1.0
        f_re = (nr * lam_re[d] + pi[1] * lam_im[d]) / den
        f_im = (pi[1] * lam_re[d] - nr * lam_im[d]) / den
        bb_re = f_re[..., None] * b_re - f_im[..., None] * b_im
        bb_im = f_re[..., None] * b_im + f_im[..., None] * b_re
        w_re = pr[:lk, :, :, None] * bb_re - pi[:lk, :, :, None] * bb_im
        w_im = pr[:lk, :, :, None] * bb_im + pi[:lk, :, :, None] * bb_re
        k_dir.append(jnp.einsum('gop,jgph->jgoh', c_re, w_re, precision=HIGHEST)
                     - jnp.einsum('gop,jgph->jgoh', c_im, w_im, precision=HIGHEST))
        order = slice(None, None, -1) if d == 0 else slice(None)
        bst_dir.append(jnp.stack([w_re[order], w_im[order]], axis=0).transpose(2, 1, 4, 0, 3))
        kk = jnp.arange(1, lk + 1) if d == 0 else jnp.arange(lk, 0, -1)
        ar, ai = pr[kk], pi[kk]
        cr = jnp.einsum('gop,tgp->gpto', c_re, ar) - jnp.einsum('gop,tgp->gpto', c_im, ai)
        ci = -(jnp.einsum('gop,tgp->gpto', c_re, ai) + jnp.einsum('gop,tgp->gpto', c_im, ar))
        coff_dir.append(jnp.stack([cr, ci], axis=1))
        for n_pow in (lk, seg_ctx, seg_lat):
            lam_rows += [jnp.exp(n_pow * e_re) * jnp.cos(n_pow * ang), jnp.exp(n_pow * e_re) * jnp.sin(n_pow * ang)]
    s_i, t_i = jnp.arange(lk)[:, None], jnp.arange(lk)[None, :]
    kf = jnp.where((t_i >= s_i)[..., None, None, None], k_dir[0][jnp.clip(t_i - s_i, 0, lk - 1)], 0.0)
    kb = jnp.where((s_i >= t_i)[..., None, None, None], k_dir[1][jnp.clip(s_i - t_i, 0, lk - 1)], 0.0)
    dk = (jnp.eye(lk, dtype=F32)[:, :, None, None, None] * jnp.eye(h_, dtype=F32)[None, None, None]
          * d_skip.reshape(g_, h_)[None, None, :, :, None])
    tz = (kf + kb + dk).transpose(2, 0, 4, 1, 3).reshape(S5_PAIRS, 2, lk * h_, lk * h_)
    bst = jnp.stack(bst_dir, axis=3).reshape(S5_PAIRS, 2, lk, h_, 2, 2, p_)
    bst = jnp.einsum('aishdqp,ij->aishdqjp', bst, eye2).reshape(S5_PAIRS, 2 * lk * h_, 8 * p_)
    coff = jnp.stack(coff_dir, axis=1).reshape(S5_PAIRS, 2, 2, 2, p_, lk, h_)
    coff = jnp.einsum('aidqpto,ij->adqipjto', coff, eye2).reshape(S5_PAIRS, 8 * p_, 2 * lk * h_)
    lam = jnp.stack([r.reshape(S5_PAIRS, 2 * p_) for r in lam_rows], axis=1)
    order12 = jnp.array([0, 1, 6, 7, 2, 3, 8, 9, 4, 5, 10, 11])
    lam = jnp.concatenate([lam[:, order12], jnp.zeros((S5_PAIRS, 4, 2 * p_), F32)], axis=1)
    return tz.astype(BF16), bst.astype(BF16), coff.astype(BF16), lam


def _s5_kernel(u_ref, tz_ref, bst_ref, coff_ref, lam_ref, y_ref, x_s, h_s, t_s, g_s, *, n_lat, n_ctx, bsz):
    nseg = SUBLANES // bsz
    u = u_ref[0]
    x_s[...] = jnp.dot(u, bst_ref[0], preferred_element_type=F32)
    lam = lam_ref[0]

    def scan(base, nsteps, rev, co, lr, li, init, store):
        def body(k, carry):
            hr, hi = carry
            q = nsteps - 1 - k if rev else k
            r0 = pl.multiple_of(base + q * SUBLANES, SUBLANES)
            if store:
                h_s[pl.ds(r0, SUBLANES), co:co + LANES] = hr
                h_s[pl.ds(r0, SUBLANES), co + LANES:co + 2 * LANES] = hi
            xr = x_s[pl.ds(r0, SUBLANES), co:co + LANES]
            xi = x_s[pl.ds(r0, SUBLANES), co + LANES:co + 2 * LANES]
            return lr * hr - li * hi + xr, lr * hi + li * hr + xi
        return lax.fori_loop(0, nsteps, body, init, unroll=4)

    def seg_carry(tot, lr, li, inits, rev):
        t_s[:, :LANES], t_s[:, LANES:] = tot
        finals = []
        for b in range(bsz):
            gr, gi = inits[b]
            for seg in (range(nseg - 1, -1, -1) if rev else range(nseg)):
                s = b * nseg + seg
                g_s[s:s + 1, :LANES], g_s[s:s + 1, LANES:] = gr, gi
                tr, ti = t_s[s:s + 1, :LANES], t_s[s:s + 1, LANES:]
                gr, gi = lr * gr - li * gi + tr, lr * gi + li * gr + ti
            finals.append((gr, gi))
        return (g_s[:, :LANES], g_s[:, LANES:]), finals

    zero8 = (jnp.zeros((SUBLANES, LANES), F32), jnp.zeros((SUBLANES, LANES), F32))
    zero1 = (jnp.zeros((1, LANES), F32), jnp.zeros((1, LANES), F32))
    for d in range(2):
        rev, co = d == 1, d * 2 * LANES
        lr = jnp.broadcast_to(lam[2 * d:2 * d + 1], (SUBLANES, LANES))
        li = jnp.broadcast_to(lam[2 * d + 1:2 * d + 2], (SUBLANES, LANES))
        init1 = [zero1] * bsz
        for base, nsteps, row in ((n_lat * SUBLANES, n_ctx, 4 + 2 * d), (0, n_lat, 8 + 2 * d)):
            tot = scan(base, nsteps, rev, co, lr, li, zero8, False)
            g0, init1 = seg_carry(tot, lam[row:row + 1], lam[row + 1:row + 2], init1, rev)
            scan(base, nsteps, rev, co, lr, li, g0, True)
    half = S5_BLOCK * S5_GROUP
    y = jnp.dot(h_s[...].astype(BF16), coff_ref[0], preferred_element_type=F32)
    y = y + jnp.concatenate([jnp.dot(u[:, :half], tz_ref[0, 0], preferred_element_type=F32),
                             jnp.dot(u[:, half:], tz_ref[0, 1], preferred_element_type=F32)], axis=1)
    y_ref[0] = jax.nn.gelu(y).astype(BF16)


def _s5_pack(part, nseg):
    bsz, n, _ = part.shape
    steps = n // (nseg * S5_BLOCK)
    p = part.reshape(bsz, nseg, steps, S5_BLOCK, S5_PAIRS, 2, S5_GROUP)
    return p.transpose(4, 2, 0, 1, 5, 3, 6).reshape(S5_PAIRS, steps * bsz * nseg, 2 * S5_BLOCK * S5_GROUP)


def _s5_unpack(rows, bsz, nseg):
    steps = rows.shape[1] // (bsz * nseg)
    p = rows.reshape(S5_PAIRS, steps, bsz, nseg, 2, S5_BLOCK, S5_GROUP)
    return p.transpose(2, 3, 1, 5, 0, 4, 6).reshape(bsz, nseg * steps * S5_BLOCK, S5_WIDTH)


def s5_mix(u, lc, ops):
    bsz, t, _ = u.shape
    assert SUBLANES % bsz == 0
    nseg = SUBLANES // bsz
    tz, bst, coff, lam = ops
    n_lat, n_ctx = (t - lc) // (nseg * S5_BLOCK), lc // (nseg * S5_BLOCK)
    rows = (n_lat + n_ctx) * SUBLANES
    width = 2 * S5_BLOCK * S5_GROUP
    up = jnp.concatenate([_s5_pack(u[:, lc:], nseg), _s5_pack(u[:, :lc], nseg)], axis=1)
    blk = lambda *shape: pl.BlockSpec((1,) + shape, lambda i: (i,) + (0,) * len(shape))
    y = pl.pallas_call(
        functools.partial(_s5_kernel, n_lat=n_lat, n_ctx=n_ctx, bsz=bsz),
        grid=(S5_PAIRS,),
        in_specs=[blk(rows, width), blk(2, width // 2, width // 2), blk(width, 4 * LANES), blk(4 * LANES, width),
                  blk(16, LANES)],
        out_specs=blk(rows, width),
        out_shape=jax.ShapeDtypeStruct((S5_PAIRS, rows, width), BF16),
        scratch_shapes=[pltpu.VMEM((rows, 4 * LANES), F32), pltpu.VMEM((rows, 4 * LANES), F32),
                        pltpu.VMEM((SUBLANES, 2 * LANES), F32), pltpu.VMEM((SUBLANES, 2 * LANES), F32)],
        compiler_params=_cparams("arbitrary"),
        name="s5_mix",
    )(up, tz, bst, coff, lam)
    nl = n_lat * SUBLANES
    return jnp.concatenate([_s5_unpack(y[:, nl:], bsz, nseg), _s5_unpack(y[:, :nl], bsz, nseg)], axis=1)


def s5_compact_operators(lam_re, lam_im, log_step, b_re, b_im, c_re, c_im, d_skip, seg_ctx, seg_lat):
    g_, p_, h_, lk = S5_GROUPS, S5_STATE, S5_GROUP, S5_BLOCK
    j = jnp.arange(lk, dtype=F32)[:, None, None]
    eye2 = jnp.eye(2, dtype=F32)
    k_dir, bb_dir, lam_rows = [], [], []
    for d in range(2):
        step = jnp.exp(log_step[d])[:, None]
        e_re, ang = lam_re[d] * step, lam_im[d] * step
        pr = jnp.exp(j * e_re) * jnp.cos(j * ang)
        pi = jnp.exp(j * e_re) * jnp.sin(j * ang)
        den = lam_re[d] * lam_re[d] + lam_im[d] * lam_im[d]
        nr = pr[1] - 1.0
        f_re = (nr * lam_re[d] + pi[1] * lam_im[d]) / den
        f_im = (pi[1] * lam_re[d] - nr * lam_im[d]) / den
        bb_re = f_re[..., None] * b_re - f_im[..., None] * b_im
        bb_im = f_re[..., None] * b_im + f_im[..., None] * b_re
        w_re = pr[:, :, :, None] * bb_re - pi[:, :, :, None] * bb_im
        w_im = pr[:, :, :, None] * bb_im + pi[:, :, :, None] * bb_re
        k_dir.append(jnp.einsum('gop,jgph->jgho', c_re, w_re, precision=HIGHEST)
                     - jnp.einsum('gop,jgph->jgho', c_im, w_im, precision=HIGHEST))
        bb_dir.append(jnp.stack([bb_re, bb_im], axis=0).transpose(1, 3, 0, 2))
        for n_pow in (lk, seg_ctx, seg_lat, 1):
            lam_rows += [jnp.exp(n_pow * e_re) * jnp.cos(n_pow * ang), jnp.exp(n_pow * e_re) * jnp.sin(n_pow * ang)]
    mid = k_dir[0][0] + k_dir[1][0] + jnp.eye(h_, dtype=F32)[None] * d_skip.reshape(g_, h_, 1)
    taps = jnp.concatenate([k_dir[1][1:][::-1], mid[None], k_dir[0][1:]], axis=0)
    taps = taps.transpose(1, 2, 0, 3).reshape(S5_PAIRS, 2, h_, 2 * lk - 1, h_)
    kfull = jnp.einsum('aihmo,ij->aihmjo', taps, eye2).reshape(S5_PAIRS, 2 * h_, (2 * lk - 1) * 2 * h_)
    kfull = jnp.pad(kfull, ((0, 0), (0, 0), (0, 2 * lk * 2 * h_ - kfull.shape[2])))
    bbp = jnp.stack(bb_dir, axis=1).reshape(S5_PAIRS, 2, 2, h_, 2, p_)
    bbp = jnp.einsum('aidhqp,ij->adihqjp', bbp, eye2).reshape(S5_PAIRS, 2, 2 * h_, 4 * p_)
    cpt = jnp.stack([c_re, -c_im], axis=2).reshape(S5_PAIRS, 2, h_, 2, p_)
    cpt = jnp.einsum('ajoqp,ij->ajoqip', cpt, eye2).reshape(S5_PAIRS, 2 * h_, 4 * p_)
    order16 = [0, 1, 8, 9, 2, 3, 10, 11, 4, 5, 12, 13, 6, 7, 14, 15]
    lam = jnp.stack([lam_rows[r].reshape(S5_PAIRS, 2 * p_) for r in order16], axis=1)
    return kfull, bbp, cpt, lam


def _s5_fused_kernel(z_ref, kf_ref, bbp_ref, cpt_ref, lam_ref, y_ref, tz_s, bst_s, cft_s, x_s, h_s, t_s, g_s,
                     *, n_lat, n_ctx, bsz):
    nseg = SUBLANES // bsz
    pw = 2 * S5_GROUP
    lam = lam_ref[0]
    kf = kf_ref[0]
    for tt in range(S5_BLOCK):
        off = (S5_BLOCK - 1 - tt) * pw
        tz_s[tt * pw:(tt + 1) * pw, :] = kf[:, off:off + S5_BLOCK * pw].astype(BF16)
    for d in range(2):
        co = d * 2 * LANES
        lr = jnp.broadcast_to(lam[12 + 2 * d:13 + 2 * d], (pw, LANES))
        li = jnp.broadcast_to(lam[13 + 2 * d:14 + 2 * d], (pw, LANES))
        order = range(S5_BLOCK - 1, -1, -1) if d == 0 else range(S5_BLOCK)
        wr, wi = bbp_ref[0, d, :, :LANES], bbp_ref[0, d, :, LANES:]
        for tt in order:
            bst_s[tt * pw:(tt + 1) * pw, co:co + LANES] = wr.astype(BF16)
            bst_s[tt * pw:(tt + 1) * pw, co + LANES:co + 2 * LANES] = wi.astype(BF16)
            wr, wi = lr * wr - li * wi, lr * wi + li * wr
        xr, xi = cpt_ref[0, :, :LANES], cpt_ref[0, :, LANES:]
        for tt in (range(S5_BLOCK) if d == 0 else range(S5_BLOCK - 1, -1, -1)):
            xr, xi = lr * xr + li * xi, lr * xi - li * xr
            cft_s[tt * pw:(tt + 1) * pw, co:co + LANES] = xr.astype(BF16)
            cft_s[tt * pw:(tt + 1) * pw, co + LANES:co + 2 * LANES] = xi.astype(BF16)

    u = z_ref[0].astype(BF16)
    x = jnp.dot(u, bst_s[...], preferred_element_type=F32)
    for k in range(4):
        x_s[k] = x[:, k * LANES:(k + 1) * LANES]

    def scan(base, nsteps, stride, rev, cr, lr, li, init, store):
        def body(k, carry):
            hr, hi = carry
            q = nsteps - 1 - k if rev else k
            rows = pl.ds(base + q, SUBLANES, stride=stride)
            if store:
                h_s[cr, rows, :] = hr
                h_s[cr + 1, rows, :] = hi
            return lr * hr - li * hi + x_s[cr, rows, :], lr * hi + li * hr + x_s[cr + 1, rows, :]
        return lax.fori_loop(0, nsteps, body, init, unroll=4)

    def seg_carry(tot, lr, li, inits, rev):
        t_s[:, :LANES], t_s[:, LANES:] = tot
        finals = []
        for b in range(bsz):
            gr, gi = inits[b]
            for seg in (range(nseg - 1, -1, -1) if rev else range(nseg)):
                s = b * nseg + seg
                g_s[s:s + 1, :LANES], g_s[s:s + 1, LANES:] = gr, gi
                tr, ti = t_s[s:s + 1, :LANES], t_s[s:s + 1, LANES:]
                gr, gi = lr * gr - li * gi + tr, lr * gi + li * gr + ti
            finals.append((gr, gi))
        return (g_s[:, :LANES], g_s[:, LANES:]), finals

    zero8 = (jnp.zeros((SUBLANES, LANES), F32), jnp.zeros((SUBLANES, LANES), F32))
    zero1 = (jnp.zeros((1, LANES), F32), jnp.zeros((1, LANES), F32))
    for d in range(2):
        rev = d == 1
        lr = jnp.broadcast_to(lam[2 * d:2 * d + 1], (SUBLANES, LANES))
        li = jnp.broadcast_to(lam[2 * d + 1:2 * d + 2], (SUBLANES, LANES))
        init1 = [zero1] * bsz
        for base, nsteps, row in ((n_lat * SUBLANES, n_ctx, 4 + 2 * d), (0, n_lat, 8 + 2 * d)):
            tot = scan(base, nsteps, nsteps, rev, 2 * d, lr, li, zero8, False)
            g0, init1 = seg_carry(tot, lam[row:row + 1], lam[row + 1:row + 2], init1, rev)
            scan(base, nsteps, nsteps, rev, 2 * d, lr, li, g0, True)
    hcat = jnp.concatenate([h_s[k] for k in range(4)], axis=1).astype(BF16)
    y = lax.dot_general(hcat, cft_s[...], (((1,), (1,)), ((), ())), preferred_element_type=F32)
    y = y + jnp.dot(u, tz_s[...], preferred_element_type=F32)
    y_ref[0] = jax.nn.gelu(y)


def s5_mix_packed(z, bsz, lc, n, ops):
    assert SUBLANES % bsz == 0
    nseg = SUBLANES // bsz
    kfull, bbp, cpt, lam = ops
    n_lat, n_ctx = n // (nseg * S5_BLOCK), lc // (nseg * S5_BLOCK)
    rows, width = z.shape[1], z.shape[2]
    assert rows == (n_lat + n_ctx) * SUBLANES
    blk = lambda *shape: pl.BlockSpec((1,) + shape, lambda i: (i,) + (0,) * len(shape))
    return pl.pallas_call(
        functools.partial(_s5_fused_kernel, n_lat=n_lat, n_ctx=n_ctx, bsz=bsz),
        grid=(S5_PAIRS,),
        in_specs=[blk(rows, width), blk(*kfull.shape[1:]), blk(*bbp.shape[1:]), blk(*cpt.shape[1:]), blk(16, LANES)],
        out_specs=blk(rows, width),
        out_shape=jax.ShapeDtypeStruct((S5_PAIRS, rows, width), F32),
        scratch_shapes=[pltpu.VMEM((width, width), BF16), pltpu.VMEM((width, 4 * LANES), BF16),
                        pltpu.VMEM((width, 4 * LANES), BF16),
                        pltpu.VMEM((4, rows, LANES), F32), pltpu.VMEM((4, rows, LANES), F32),
                        pltpu.VMEM((SUBLANES, 2 * LANES), F32), pltpu.VMEM((SUBLANES, 2 * LANES), F32)],
        compiler_params=_cparams("arbitrary"),
        name="s5_mix",
    )(z, kfull, bbp, cpt, lam)


def _glu_kernel(y_ref, wa_ref, wb_ref, ba_ref, bb_ref, o_ref, a_s):
    pw = 2 * S5_GROUP
    per = LANES // pw
    blocks = ROW_TILE // S5_BLOCK
    for tt in range(S5_BLOCK):
        for k in range(S5_WIDTH // LANES):
            a_s[k, pl.ds(tt, blocks, stride=S5_BLOCK), :] = jnp.concatenate(
                [y_ref[k * per + j, :, tt * pw:(tt + 1) * pw] for j in range(per)], axis=1)
    y = jnp.concatenate([a_s[k] for k in range(S5_WIDTH // LANES)], axis=1).astype(BF16)
    a = jnp.dot(y, wa_ref[...], preferred_element_type=F32) + ba_ref[...]
    g = jnp.dot(y, wb_ref[...], preferred_element_type=F32) + bb_ref[...]
    o_ref[0] = (a * jax.nn.sigmoid(g)).astype(BF16)


def s5_glu(y, bsz, lc, n, w_glu, b_glu):
    w = S5_WIDTH
    wb = w_glu.astype(BF16)
    blocks = ROW_TILE // S5_BLOCK
    y_spec = pl.BlockSpec((S5_PAIRS, blocks, y.shape[2]),
                          lambda b, t: (0, _s5_block_rows(b, t, bsz, n // ROW_TILE), 0))
    return pl.pallas_call(
        _glu_kernel,
        grid=(bsz, (lc + n) // ROW_TILE),
        in_specs=[y_spec, _const_spec((w, w)), _const_spec((w, w)), _const_spec((1, w)), _const_spec((1, w))],
        out_specs=pl.BlockSpec((1, ROW_TILE, w), lambda b, i: (b, i, 0)),
        out_shape=jax.ShapeDtypeStruct((bsz, lc + n, w), BF16),
        scratch_shapes=[pltpu.VMEM((w // LANES, ROW_TILE, LANES), F32)],
        compiler_params=_cparams("arbitrary", "arbitrary"),
        name="s5_glu",
    )(y, wb[:, :w], wb[:, w:], b_glu[:w].reshape(1, w), b_glu[w:].reshape(1, w))


def _softplus(x):
    return jnp.maximum(x, 0.0) + jnp.log(1.0 + jnp.exp(-jnp.abs(x)))


def _ssd_chunk_id(k, rev, nc, ncc):
    if not rev:
        return k
    return jnp.where(k < ncc, ncc - 1 - k, nc - 1 - (k - ncc))


def _ssd_kernel(*refs, rev, nc, ncc):
    if rev:
        (x_ref, xp_ref, xn_ref, dt_ref, dtt_ref, cw_ref, cb_ref, bias_ref, biast_ref, alog_ref, alogt_ref,
         z_ref, yf_ref, dsk_ref, nw_ref, o_ref, st_s, xp_s, y_s) = refs
    else:
        (x_ref, xp_ref, xn_ref, dt_ref, dtt_ref, cw_ref, cb_ref, bias_ref, biast_ref, alog_ref, alogt_ref,
         o_ref, st_s, xp_s) = refs
    lch = SSD_CHUNK
    k = pl.program_id(1)
    c = _ssd_chunk_id(k, rev, nc, ncc)

    @pl.when(k == 0)
    def _():
        st_s[...] = jnp.zeros(st_s.shape, F32)

    first = jnp.logical_or(c == 0, c == ncc)
    last = jnp.logical_or(c == ncc - 1, c == nc - 1)
    xp_s[0:SUBLANES, :] = jnp.where(first, 0.0, xp_ref[0])
    xp_s[SUBLANES:SUBLANES + lch, :] = x_ref[0]
    xp_s[SUBLANES + lch:2 * SUBLANES + lch, :] = jnp.where(last, 0.0, xn_ref[0])
    acc = cb_ref[...] + cw_ref[0:1, :] * xp_s[SUBLANES - 2:SUBLANES - 2 + lch, :]
    for tap in range(1, SSD_CONV):
        acc = acc + cw_ref[tap:tap + 1, :] * xp_s[SUBLANES - 2 + tap:SUBLANES - 2 + tap + lch, :]
    xc = acc * jax.nn.sigmoid(acc)
    gn = SSD_GROUPS * SSD_STATE
    xs = xc[:, :SSD_WIDTH]
    bm = xc[:, SSD_WIDTH:SSD_WIDTH + gn]
    cm = xc[:, SSD_WIDTH + gn:]

    dt = _softplus(dt_ref[0] + bias_ref[...])
    dtt = _softplus(dtt_ref[0] + biast_ref[...])
    da = dt * -jnp.exp(alog_ref[...])
    dat = dtt * -jnp.exp(alogt_ref[...])
    row_i = lax.broadcasted_iota(jnp.int32, (lch, lch), 0)
    col_i = lax.broadcasted_iota(jnp.int32, (lch, lch), 1)
    tri = (col_i >= row_i) if rev else (col_i <= row_i)
    trit = (row_i >= col_i) if rev else (row_i <= col_i)
    cum = jnp.dot(tri.astype(F32), da, preferred_element_type=F32, precision=HIGHEST)
    cumt = jnp.dot(dat, trit.astype(F32), preferred_element_type=F32, precision=HIGHEST)
    end = 0 if rev else lch - 1
    tot = cum[end:end + 1, :]
    wt = dtt * jnp.exp(cumt[:, end:end + 1] - cumt)
    lane = lax.broadcasted_iota(jnp.int32, (1, LANES), 1)
    low = lane < SSD_HEAD_DIM
    heads_per_group = SSD_HEADS // SSD_GROUPS
    for g in range(SSD_GROUPS):
        bg = bm[:, g * SSD_STATE:(g + 1) * SSD_STATE]
        cg = cm[:, g * SSD_STATE:(g + 1) * SSD_STATE].astype(BF16)
        scores = lax.dot_general(cg, bg.astype(BF16), (((1,), (1,)), ((), ())), preferred_element_type=F32)
        bgt = bg.T
        st = st_s[g]
        yoff = jnp.dot(cg, st.astype(BF16), preferred_element_type=F32)
        for jp in range(heads_per_group // 2):
            ha = g * heads_per_group + 2 * jp
            lo = (ha // 2) * LANES
            sl = jp * LANES
            xsp = xs[:, lo:lo + LANES]
            xblk = jnp.concatenate([jnp.where(low, xsp, 0.0), jnp.where(low, 0.0, xsp)], axis=0).astype(BF16)
            ms, ecols, lhs2 = [], [], []
            for h in (ha, ha + 1):
                col = cum[:, h:h + 1]
                seg = col - cumt[h:h + 1, :]
                dec = jnp.exp(jnp.where(tri, seg, -jnp.inf))
                ms.append((scores * dec * dtt[h:h + 1, :]).astype(BF16))
                ecols.append(jnp.exp(col))
                lhs2.append((bgt * wt[h:h + 1, :]).astype(BF16))
            ydiag = jnp.dot(jnp.concatenate(ms, axis=1), xblk, preferred_element_type=F32)
            y_pair = ydiag + jnp.where(low, ecols[0], ecols[1]) * yoff[:, sl:sl + LANES]
            upd = jnp.dot(jnp.concatenate(lhs2, axis=1), xblk, preferred_element_type=F32)
            cd = jnp.where(low, jnp.exp(tot[:, ha:ha + 1]), jnp.exp(tot[:, ha + 1:ha + 2]))
            st_s[g, :, sl:sl + LANES] = cd * st[:, sl:sl + LANES] + upd
            if rev:
                y_s[:, lo:lo + LANES] = y_pair
            else:
                o_ref[0, :, lo:lo + LANES] = y_pair
    if rev:
        y = yf_ref[0] + y_s[...] + dsk_ref[...] * xs
        z = z_ref[0]
        o_ref[0] = (_rms(y * (z * jax.nn.sigmoid(z))) * nw_ref[...]).astype(BF16)


def ssd_mix(z, xbc, dtf, dtb, lc, conv_w, conv_b, dt_bias, a_log, d_skip, norm_w):
    bsz, t, wx = xbc.shape
    lch = SSD_CHUNK
    nc, ncc = t // lch, lc // lch
    hb = lch // SUBLANES
    dsk = jnp.repeat(d_skip, SSD_HEAD_DIM).reshape(1, SSD_WIDTH)
    yf = None
    for rev in (False, True):
        cid = functools.partial(_ssd_chunk_id, rev=rev, nc=nc, ncc=ncc)
        d = int(rev)
        dt = dtb if rev else dtf
        row = lambda wd: pl.BlockSpec((1, lch, wd), lambda b, k: (b, cid(k), 0))
        in_specs = [row(wx),
                    pl.BlockSpec((1, SUBLANES, wx), lambda b, k: (b, jnp.maximum(cid(k) * hb - 1, 0), 0)),
                    pl.BlockSpec((1, SUBLANES, wx), lambda b, k: (b, jnp.minimum((cid(k) + 1) * hb, nc * hb - 1), 0)),
                    row(SSD_HEADS),
                    pl.BlockSpec((1, SSD_HEADS, lch), lambda b, k: (b, 0, cid(k))),
                    _const_spec((SSD_CONV, wx)), _const_spec((1, wx)),
                    _const_spec((1, SSD_HEADS)), _const_spec((SSD_HEADS, 1)),
                    _const_spec((1, SSD_HEADS)), _const_spec((SSD_HEADS, 1))]
        args = [xbc, xbc, xbc, dt, jnp.swapaxes(dt, 1, 2), conv_w, conv_b.reshape(1, wx),
                dt_bias[d].reshape(1, SSD_HEADS), dt_bias[d].reshape(SSD_HEADS, 1),
                a_log[d].reshape(1, SSD_HEADS), a_log[d].reshape(SSD_HEADS, 1)]
        scratch = [pltpu.VMEM((SSD_GROUPS, SSD_STATE, SSD_WIDTH // SSD_GROUPS), F32),
                   pltpu.VMEM((lch + 2 * SUBLANES, wx), F32)]
        if rev:
            in_specs += [row(SSD_WIDTH), row(SSD_WIDTH), _const_spec((1, SSD_WIDTH)), _const_spec((1, SSD_WIDTH))]
            args += [z, yf, dsk, norm_w.reshape(1, SSD_WIDTH)]
            scratch += [pltpu.VMEM((lch, SSD_WIDTH), F32)]
        out = pl.pallas_call(
            functools.partial(_ssd_kernel, rev=rev, nc=nc, ncc=ncc),
            grid=(bsz, nc),
            in_specs=in_specs,
            out_specs=row(SSD_WIDTH),
            out_shape=jax.ShapeDtypeStruct((bsz, t, SSD_WIDTH), BF16 if rev else F32),
            scratch_shapes=scratch,
            compiler_params=_cparams("arbitrary", "arbitrary"),
            name="ssd_bwd" if rev else "ssd_fwd",
        )(*args)
        yf = out
    return out


def _outproj_kernel(ys_ref, yd_ref, xc_ref, xl_ref, wt_ref, wb_ref, g1_ref, gate_ref, g2_ref, sc_ref, sh_ref,
                    wr_ref, xlo_ref, xco_ref, h2_ref, lg_ref):
    b, t = pl.program_id(0), pl.program_id(1)
    row = _mod_row(t, b)
    y = (jnp.dot(ys_ref[0], wt_ref[...], preferred_element_type=F32)
         + jnp.dot(yd_ref[0], wb_ref[...], preferred_element_type=F32))
    x = jnp.where(t == 0, xc_ref[0], xl_ref[0])
    xn = x + gate_ref[pl.ds(row, 1), :] * (_rms(y) * g1_ref[...])

    @pl.when(t == 0)
    def _():
        xco_ref[0] = xn

    @pl.when(t > 0)
    def _():
        xlo_ref[0] = xn

    h2 = _rms(xn) * g2_ref[...]
    h2 = h2 * (1.0 + sc_ref[pl.ds(row, 1), :]) + sh_ref[pl.ds(row, 1), :]
    h2_ref[0] = h2.reshape(h2_ref.shape[1:])
    lg_ref[0] = jnp.dot(h2.astype(BF16), wr_ref[...], preferred_element_type=F32)


def out_proj(y_s5, y_ssd, xc, xl, w_out, g1, gate1, g2, sc2, sh2, w_router):
    bsz, lc, d = xc.shape
    n = xl.shape[1]
    nt = (lc + n) // ROW_TILE
    w = w_out.astype(BF16)
    hw = S5_WIDTH
    tile = lambda wd: pl.BlockSpec((1, ROW_TILE, wd), lambda b, t: (b, t, 0))
    pair_out = [pl.BlockSpec((1, ROW_TILE, d), lambda b, t: (b, jnp.maximum(t - 1, 0), 0)),
                pl.BlockSpec((1, ROW_TILE, d), lambda b, t: (b, 0, 0))]
    return pl.pallas_call(
        _outproj_kernel,
        grid=(bsz, nt),
        in_specs=[tile(hw), tile(SSD_WIDTH)] + _pair_specs(lc, d)
        + [_const_spec((hw, d)), _const_spec((SSD_WIDTH, d)), _const_spec((1, d)), _const_spec((8, d)),
           _const_spec((1, d)), _const_spec((8, d)), _const_spec((8, d)), _const_spec((d, N_EXPERTS))],
        out_specs=pair_out + [pl.BlockSpec((1, ROW_TILE, d // LANES, LANES), lambda b, t: (b, t, 0, 0)),
                              tile(N_EXPERTS)],
        out_shape=[jax.ShapeDtypeStruct((bsz, n, d), F32), jax.ShapeDtypeStruct((bsz, lc, d), F32),
                   jax.ShapeDtypeStruct((bsz, lc + n, d // LANES, LANES), F32),
                   jax.ShapeDtypeStruct((bsz, lc + n, N_EXPERTS), F32)],
        compiler_params=_cparams("arbitrary", "arbitrary"),
        name="out_proj",
    )(y_s5, y_ssd, xc, xl, w[:hw], w[hw:], g1.reshape(1, d), gate1, g2.reshape(1, d), sc2, sh2,
      w_router.astype(BF16))


def _row_blocks(rows):
    return [(r0, min(MOE_ROW_BLOCK, rows - r0)) for r0 in range(0, rows, MOE_ROW_BLOCK)]


def _moe_kernel(idx_ref, h2_hbm, gate_ref, wg_ref, wu_ref, wd_ref, o_ref, xf_s, xb_s, h_s, sem, *, rows, nf, nsteps):
    e, s = pl.program_id(0), pl.program_id(1)
    ne = pl.num_programs(0)
    d = xb_s.shape[1]

    def row_copy(src_row, r):
        return pltpu.make_async_copy(h2_hbm.at[pl.ds(src_row, 1)], xf_s.at[pl.ds(r, 1)], sem.at[0])

    def issue(ex, lo, hi):
        def body(r, c):
            row_copy(idx_ref[ex * rows + r], r).start()
            return c
        lax.fori_loop(lo, hi, body, 0)

    @pl.when(jnp.logical_and(e == 0, s == 0))
    def _():
        issue(0, 0, rows)

    @pl.when(s == 0)
    def _():
        def body(r, c):
            row_copy(0, r).wait()
            return c
        lax.fori_loop(0, rows, body, 0, unroll=8)
        for r0, rb in _row_blocks(rows):
            xb_s[r0:r0 + rb, :] = xf_s[r0:r0 + rb].reshape(rb, d).astype(BF16)

    per_step = -(-rows // (nsteps - 1))

    @pl.when(jnp.logical_and(s >= 1, e + 1 < ne))
    def _():
        issue(e + 1, jnp.minimum((s - 1) * per_step, rows), jnp.minimum(s * per_step, rows))

    @pl.when(s < nf)
    def _():
        def phase_a(wg_s, wu_s):
            wg_s[...] = wg_ref[0, 0].astype(BF16)
            wu_s[...] = wu_ref[0, 0].astype(BF16)
            for r0, rb in _row_blocks(rows):
                x = xb_s[r0:r0 + rb, :]
                g = jnp.dot(x, wg_s[...], preferred_element_type=F32)
                u = jnp.dot(x, wu_s[...], preferred_element_type=F32)
                h_s[s, r0:r0 + rb, :] = ((g * jax.nn.sigmoid(g)) * u).astype(BF16)
        pl.run_scoped(phase_a, pltpu.VMEM(wg_ref.shape[2:], BF16), pltpu.VMEM(wu_ref.shape[2:], BF16))

    @pl.when(s >= nf)
    def _():
        def phase_b(wd_s):
            wd_s[...] = wd_ref[0, 0].astype(BF16)
            for r0, rb in _row_blocks(rows):
                h = jnp.concatenate([h_s[f, r0:r0 + rb, :] for f in range(nf)], axis=1)
                y = jnp.dot(h, wd_s[...], preferred_element_type=F32) * gate_ref[0, r0:r0 + rb, :]
                o_ref[0, r0:r0 + rb] = y.reshape((rb,) + o_ref.shape[2:])
        pl.run_scoped(phase_b, pltpu.VMEM(wd_ref.shape[2:], BF16))


def moe_ffn(idx_rows, h2, gate, layer, w_gate, w_up, w_down):
    e, r, _ = gate.shape
    d, ff = w_gate.shape[2], w_gate.shape[3]
    tf, tn = MOE_F_TILE, MOE_N_TILE
    nf, nn = ff // tf, d // tn
    fcl = lambda s: jnp.minimum(s, nf - 1)
    ncl = lambda s: jnp.maximum(s - nf, 0)
    grid_spec = pltpu.PrefetchScalarGridSpec(
        num_scalar_prefetch=1,
        grid=(e, nf + nn),
        in_specs=[pl.BlockSpec(memory_space=pl.ANY),
                  pl.BlockSpec((1, r, 1), lambda i, s, idx: (i, 0, 0)),
                  pl.BlockSpec((1, 1, d, tf), lambda i, s, idx: (layer, i, 0, fcl(s))),
                  pl.BlockSpec((1, 1, d, tf), lambda i, s, idx: (layer, i, 0, fcl(s))),
                  pl.BlockSpec((1, 1, ff, tn), lambda i, s, idx: (layer, i, 0, ncl(s)))],
        out_specs=pl.BlockSpec((1, r, tn // LANES, LANES), lambda i, s, idx: (i, 0, ncl(s), 0)),
        scratch_shapes=[pltpu.VMEM((r, d // LANES, LANES), F32), pltpu.VMEM((r, d), BF16),
                        pltpu.VMEM((nf, r, tf), BF16), pltpu.SemaphoreType.DMA((1,))],
    )
    return pl.pallas_call(
        functools.partial(_moe_kernel, rows=r, nf=nf, nsteps=nf + nn),
        grid_spec=grid_spec,
        out_shape=jax.ShapeDtypeStruct((e, r, d // LANES, LANES), F32),
        compiler_params=pltpu.CompilerParams(dimension_semantics=("arbitrary", "arbitrary"),
                                             vmem_limit_bytes=MOE_VMEM_LIMIT),
        name="moe_ffn",
    )(idx_rows, h2, gate, w_gate, w_up, w_down)


def _combine_kernel(src_ref, starts_ref, tokl_ref, out_hbm, *refs, lat_tiles, tiles_per_sample, total, with_ctx):
    if with_ctx:
        xc_ref, xl_ref, gate_ref, g3_ref, xlo_ref, xco_ref, buf, acc, sem = refs
    else:
        xl_ref, gate_ref, g3_ref, xlo_ref, buf, acc, sem = refs
    j = pl.program_id(0)
    nt = pl.num_programs(0)
    ch = COMBINE_CHUNK
    first_of = lambda t: starts_ref[t] // ch
    nch_of = lambda t: jnp.where(starts_ref[t + 1] > starts_ref[t],
                                 (starts_ref[t + 1] + ch - 1) // ch - starts_ref[t] // ch, 0)

    def issue(t, c, slot):
        base = (first_of(t) + c) * ch

        def body(i, carry):
            for p in range(2):
                k = 2 * i + p
                pltpu.make_async_copy(out_hbm.at[pl.ds(src_ref[base + k], 1)], buf.at[slot, pl.ds(k, 1)],
                                      sem.at[slot]).start(priority=p)
            return carry
        lax.fori_loop(0, ch // 2, body, 0, unroll=4)

    nch = nch_of(j)

    @pl.when(jnp.logical_and(j == 0, nch > 0))
    def _():
        issue(j, 0, 0)

    acc[...] = jnp.zeros(acc.shape, F32)
    tok = lax.broadcasted_iota(jnp.int32, (ROW_TILE, ch), 0) + j * ROW_TILE

    def chunk(c, carry):
        slot = c % 2

        @pl.when(c + 1 < nch)
        def _():
            issue(j, c + 1, 1 - slot)

        def wbody(i, cc):
            pltpu.make_async_copy(out_hbm.at[pl.ds(0, 1)], buf.at[slot, pl.ds(i, 1)], sem.at[slot]).wait()
            return cc
        lax.fori_loop(0, ch, wbody, 0, unroll=8)
        rows = buf[slot].reshape(ch, acc.shape[1])
        onehot = (tok == tokl_ref[pl.ds(first_of(j) + c, 1), :]).astype(BF16)
        hi = rows.astype(BF16)
        lo = (rows - hi.astype(F32)).astype(BF16)
        acc[...] += (jnp.dot(onehot, hi, preferred_element_type=F32)
                     + jnp.dot(onehot, lo, preferred_element_type=F32))
        return carry
    lax.fori_loop(0, nch, chunk, 0)

    nxt = jnp.minimum(j + 1, nt - 1)

    @pl.when(jnp.logical_and(j + 1 < nt, nch_of(nxt) > 0))
    def _():
        issue(nxt, 0, 0)

    y = gate_ref[pl.ds(jnp.where(j >= lat_tiles, 2, j // tiles_per_sample), 1), :] * (_rms(acc[...]) * g3_ref[...])
    if with_ctx:
        @pl.when(j >= lat_tiles)
        def _():
            xco_ref[0] = xc_ref[0] + y

    @pl.when(j < lat_tiles)
    def _():
        xlo_ref[0] = xl_ref[0] + y


def moe_combine(src, starts, tokl, out_rows, xc, xl, gate2, g3, with_ctx):
    bsz, n, d = xl.shape
    lc = xc.shape[1]
    tps = n // ROW_TILE
    lat_tiles = bsz * tps
    nt = lat_tiles + (bsz if with_ctx else 0)
    lat_idx = lambda j: (jnp.minimum(j, lat_tiles - 1) // tps, jnp.minimum(j, lat_tiles - 1) % tps, 0)
    ctx_idx = lambda j: (jnp.clip(j - lat_tiles, 0, bsz - 1), 0, 0)
    lat_spec = pl.BlockSpec((1, ROW_TILE, d), lambda j, s0, s1: lat_idx(j))
    ctx_spec = pl.BlockSpec((1, lc, d), lambda j, s0, s1: ctx_idx(j))
    grid_spec = pltpu.PrefetchScalarGridSpec(
        num_scalar_prefetch=2,
        grid=(nt,),
        in_specs=[pl.BlockSpec(tokl.shape, lambda j, s0, s1: (0, 0)),
                  pl.BlockSpec(memory_space=pl.ANY)]
        + ([ctx_spec] if with_ctx else []) + [lat_spec,
                                              pl.BlockSpec((8, d), lambda j, s0, s1: (0, 0)),
                                              pl.BlockSpec((1, d), lambda j, s0, s1: (0, 0))],
        out_specs=[lat_spec] + ([ctx_spec] if with_ctx else []),
        scratch_shapes=[pltpu.VMEM((2, COMBINE_CHUNK, d // LANES, LANES), F32), pltpu.VMEM((ROW_TILE, d), F32),
                        pltpu.SemaphoreType.DMA((2,))],
    )
    res = pl.pallas_call(
        functools.partial(_combine_kernel, lat_tiles=lat_tiles, tiles_per_sample=tps, total=src.shape[0],
                          with_ctx=with_ctx),
        grid_spec=grid_spec,
        out_shape=[jax.ShapeDtypeStruct(xl.shape, F32)] + ([jax.ShapeDtypeStruct(xc.shape, F32)] if with_ctx else []),
        compiler_params=_cparams("arbitrary"),
        name="moe_combine",
    )(src, starts, tokl, out_rows, *([xc] if with_ctx else []), xl, gate2, g3.reshape(1, d))
    return (res[0], res[1]) if with_ctx else (res[0], xc)


def rmsnorm(x, w):
    xf = x.astype(F32)
    y = xf * lax.rsqrt(jnp.mean(xf * xf, axis=-1, keepdims=True) + EPS)
    return (y * w.astype(F32)).astype(x.dtype)


def grid_transpose(x):
    b, n = x.shape[:2]
    rows = n // GRID_W
    return x.reshape((b, rows, GRID_W) + x.shape[2:]).swapaxes(1, 2).reshape(x.shape)


def route(logits, lc, with_ctx):
    bsz, t, e = logits.shape
    n = t - lc
    idx_l, gate_l, tok_l = [], [], []
    for lo, m, off in [(lc, n, 0)] + ([(0, lc, bsz * n)] if with_ctx else []):
        cap = CAPACITY_FACTOR * m // e
        aff = jax.nn.softmax(logits[:, lo:lo + m], axis=-1)
        gate, idx = lax.top_k(jnp.swapaxes(aff, 1, 2), cap)
        b_off = jnp.arange(bsz, dtype=jnp.int32)[:, None, None]
        idx_l.append(jnp.swapaxes(b_off * t + lo + idx, 0, 1).reshape(e, bsz * cap))
        tok_l.append(jnp.swapaxes(off + b_off * m + idx, 0, 1).reshape(e, bsz * cap))
        gate_l.append(jnp.swapaxes(gate, 0, 1).reshape(e, bsz * cap))
    idx_rows = jnp.concatenate(idx_l, axis=1).reshape(-1).astype(jnp.int32)
    gate = jnp.concatenate(gate_l, axis=1)[..., None]
    tok = jnp.concatenate(tok_l, axis=1).reshape(-1).astype(jnp.int32)
    total = tok.shape[0]
    assert total % COMBINE_CHUNK == 0
    tok_sorted, src = lax.sort((tok, jnp.arange(total, dtype=jnp.int32)), num_keys=1)
    tiles = (bsz * n + (bsz * lc if with_ctx else 0)) // ROW_TILE
    bounds = jnp.arange(tiles + 1, dtype=jnp.int32) * ROW_TILE
    starts = jnp.sum((tok[None, :] < bounds[:, None]).astype(jnp.int32), axis=1)
    return idx_rows, gate, src, starts, tok_sorted.reshape(total // COMBINE_CHUNK, COMBINE_CHUNK)


def kernel(x, c, ctx, c_ctx, ada_w, ada_b, norm_g, w_in, w_out, s5_lam_re, s5_lam_im, s5_log_step,
           s5_b_re, s5_b_im, s5_c_re, s5_c_im, s5_d, s5_w_glu, s5_b_glu, ssd_conv_w, ssd_conv_b,
           ssd_dt_bias, ssd_a_log, ssd_d, ssd_norm, moe_router, moe_w_gate, moe_w_up, moe_w_down):
    bsz, n, d = x.shape
    lc = ctx.shape[1]
    nseg = SUBLANES // bsz
    cs = jnp.zeros((8, d), F32).at[:bsz].set(c).at[2].set(c_ctx)
    mods = adaln_all(cs, ada_w, ada_b)
    xl, xc = x, ctx
    for i in range(DEPTH):
        col_major = i % 2 == 1
        last = i == DEPTH - 1
        sh1, sc1, g1, sh2, sc2, g2 = [mods[i, :, k * d:(k + 1) * d] for k in range(N_MOD)]
        if col_major:
            xl = grid_transpose(xl)
        u, z, xbc, dtf, dtb = in_proj(xc, xl, norm_g[i, 0], sc1, sh1, w_in[i])
        ops = s5_compact_operators(s5_lam_re[i], s5_lam_im[i], s5_log_step[i], s5_b_re[i], s5_b_im[i],
                                   s5_c_re[i], s5_c_im[i], s5_d[i], lc // nseg, n // nseg)
        y_s5 = s5_glu(s5_mix_packed(u, bsz, lc, n, ops), bsz, lc, n, s5_w_glu[i], s5_b_glu[i])
        y_ssd = ssd_mix(z, xbc, dtf, dtb, lc, ssd_conv_w[i], ssd_conv_b[i], ssd_dt_bias[i],
                        ssd_a_log[i], ssd_d[i], ssd_norm[i])
        xl, xc, h2, logits = out_proj(y_s5, y_ssd, xc, xl, w_out[i], norm_g[i, 1], g1, norm_g[i, 2],
                                      sc2, sh2, moe_router[i])
        idx_rows, gate, src, starts, tokl = route(logits, lc, not last)
        out = moe_ffn(idx_rows, h2.reshape((bsz * (lc + n),) + h2.shape[2:]), gate, i,
                      moe_w_gate, moe_w_up, moe_w_down)
        xl, xc = moe_combine(src, starts, tokl, out.reshape((-1,) + out.shape[2:]), xc, xl, g2, norm_g[i, 3],
                             not last)
        if col_major:
            xl = grid_transpose(xl)
    return xl
```

```python
import functools
import math

import jax
import jax.numpy as jnp
from jax import lax
from jax.experimental import pallas as pl
from jax.experimental.pallas import tpu as pltpu

D_MODEL = 2048
DEPTH = 4
GRID_W = 64
EPS = 1e-6
N_MOD = 6

S5_WIDTH = 1024
S5_GROUP = 16
S5_GROUPS = S5_WIDTH // S5_GROUP
S5_STATE = 64

SSD_WIDTH = 1024
SSD_HEAD_DIM = 64
SSD_HEADS = SSD_WIDTH // SSD_HEAD_DIM
SSD_GROUPS = 2
SSD_STATE = 128
SSD_CONV = 5
SSD_XBC = SSD_WIDTH + 2 * SSD_GROUPS * SSD_STATE

N_EXPERTS = 16
CAPACITY_FACTOR = 2
D_FF = 1536

F32 = jnp.float32
BF16 = jnp.bfloat16
HIGHEST = lax.Precision.HIGHEST

LANES = 128
SUBLANES = 8
ROW_TILE = 256
MOE_F_TILE = 256
MOE_ROW_BLOCK = 544
MOE_GATHER_SLICES = 4
MOE_N_TILE = 1024
MOE_VMEM_LIMIT = 60 * 1024 * 1024
COMBINE_CHUNK = 256
S5_BLOCK = 16
S5_PAIRS = S5_GROUPS // 2
SSD_CHUNK = 128
VMEM_LIMIT = 56 * 1024 * 1024


def _cparams(*sem):
    return pltpu.CompilerParams(dimension_semantics=sem, vmem_limit_bytes=VMEM_LIMIT)


def _adaln_kernel(c_ref, w_ref, b_ref, o_ref):
    c = c_ref[...]
    a = (c * jax.nn.sigmoid(c)).astype(BF16)
    o_ref[0] = jnp.dot(a, w_ref[0].astype(BF16), preferred_element_type=F32) + b_ref[0]


def adaln_all(cs, ada_w, ada_b):
    depth, d, n = ada_w.shape
    tn = 1024
    return pl.pallas_call(
        _adaln_kernel,
        grid=(depth, n // tn),
        in_specs=[pl.BlockSpec((8, d), lambda l, j: (0, 0)),
                  pl.BlockSpec((1, d, tn), lambda l, j: (l, 0, j)),
                  pl.BlockSpec((1, 1, tn), lambda l, j: (l, 0, j))],
        out_specs=pl.BlockSpec((1, 8, tn), lambda l, j: (l, 0, j)),
        out_shape=jax.ShapeDtypeStruct((depth, 8, n), F32),
        compiler_params=_cparams("arbitrary", "arbitrary"),
        name="adaln",
    )(cs, ada_w, ada_b.reshape(depth, 1, n))


def _mod_row(t, b):
    return jnp.where(t == 0, 2, b)


def _rms(x):
    return x * lax.rsqrt(jnp.mean(x * x, axis=-1, keepdims=True) + EPS)


def _s5_block_rows(b, t, bsz, lat_tiles):
    return jnp.where(t == 0, bsz * lat_tiles + b, b * lat_tiles + t - 1)


def _inproj_kernel(xc_ref, xl_ref, g_ref, sc_ref, sh_ref, wu_ref, wz_ref, wx_ref, wf_ref, wb_ref,
                   u_ref, z_ref, xbc_ref, dtf_ref, dtb_ref, r_s):
    b, t = pl.program_id(0), pl.program_id(1)
    x = jnp.where(t == 0, xc_ref[0], xl_ref[0])
    row = _mod_row(t, b)
    h = _rms(x) * g_ref[...]
    h = (h * (1.0 + sc_ref[pl.ds(row, 1), :]) + sh_ref[pl.ds(row, 1), :]).astype(BF16)
    u = jnp.dot(h, wu_ref[...], preferred_element_type=F32)
    nk = S5_WIDTH // LANES
    blocks = ROW_TILE // S5_BLOCK
    pw = 2 * S5_GROUP
    for k in range(nk):
        r_s[k] = u[:, k * LANES:(k + 1) * LANES]
    for k in range(nk):
        xts = [r_s[k, pl.ds(tt, blocks, stride=S5_BLOCK), :] for tt in range(S5_BLOCK)]
        for j in range(LANES // pw):
            u_ref[k * (LANES // pw) + j] = jnp.concatenate([xt[:, j * pw:(j + 1) * pw] for xt in xts], axis=1)
    z_ref[0] = jnp.dot(h, wz_ref[...], preferred_element_type=F32)
    xbc_ref[0] = jnp.dot(h, wx_ref[...], preferred_element_type=F32)
    dtf_ref[0] = jnp.dot(h, wf_ref[...], preferred_element_type=F32)
    dtb_ref[0] = jnp.dot(h, wb_ref[...], preferred_element_type=F32)


def _pair_specs(lc, d):
    assert lc == ROW_TILE
    return [pl.BlockSpec((1, ROW_TILE, d), lambda b, t: (b, 0, 0)),
            pl.BlockSpec((1, ROW_TILE, d), lambda b, t: (b, jnp.maximum(t - 1, 0), 0))]


def _const_spec(shape):
    return pl.BlockSpec(shape, lambda *_: (0,) * len(shape), pipeline_mode=pl.Buffered(1))


def in_proj(xc, xl, g, sc, sh, w_in):
    bsz, lc, d = xc.shape
    n = xl.shape[1]
    nt = (lc + n) // ROW_TILE
    s1, s2, s3 = S5_WIDTH, S5_WIDTH + SSD_WIDTH, S5_WIDTH + SSD_WIDTH + SSD_XBC
    w = w_in.astype(BF16)
    ws = [w[:, :s1], w[:, s1:s2], w[:, s2:s3], w[:, s3:s3 + SSD_HEADS], w[:, s3 + SSD_HEADS:]]
    widths = [x.shape[1] for x in ws]
    blocks = ROW_TILE // S5_BLOCK
    s5_rows = bsz * (lc + n) // S5_BLOCK
    s5_lanes = 2 * S5_GROUP * S5_BLOCK
    u_spec = pl.BlockSpec((S5_PAIRS, blocks, s5_lanes),
                          lambda b, t: (0, _s5_block_rows(b, t, bsz, n // ROW_TILE), 0))
    return pl.pallas_call(
        _inproj_kernel,
        grid=(bsz, nt),
        in_specs=_pair_specs(lc, d) + [_const_spec((1, d)), _const_spec((8, d)), _const_spec((8, d))]
        + [_const_spec((d, wd)) for wd in widths],
        out_specs=[u_spec] + [pl.BlockSpec((1, ROW_TILE, wd), lambda b, t: (b, t, 0)) for wd in widths[1:]],
        out_shape=[jax.ShapeDtypeStruct((S5_PAIRS, s5_rows, s5_lanes), F32)]
        + [jax.ShapeDtypeStruct((bsz, lc + n, wd), F32) for wd in widths[1:]],
        scratch_shapes=[pltpu.VMEM((S5_WIDTH // LANES, ROW_TILE, LANES), F32)],
        compiler_params=_cparams("arbitrary", "arbitrary"),
        name="in_proj",
    )(xc, xl, g.reshape(1, d), sc, sh, *ws)


def s5_operators(lam_re, lam_im, log_step, b_re, b_im, c_re, c_im, d_skip, seg_ctx, seg_lat):
    g_, p_, h_, lk = S5_GROUPS, S5_STATE, S5_GROUP, S5_BLOCK
    j = jnp.arange(lk + 1, dtype=F32)[:, None, None]
    eye2 = jnp.eye(2, dtype=F32)
    k_dir, bst_dir, coff_dir, lam_rows = [], [], [], []
    for d in range(2):
        step = jnp.exp(log_step[d])[:, None]
        e_re, ang = lam_re[d] * step, lam_im[d] * step
        pr = jnp.exp(j * e_re) * jnp.cos(j * ang)
        pi = jnp.exp(j * e_re) * jnp.sin(j * ang)
        den = lam_re[d] * lam_re[d] + lam_im[d] * lam_im[d]
        nr = pr[1] - 1.0
        f_re = (nr * lam_re[d] + pi[1] * lam_im[d]) / den
        f_im = (pi[1] * lam_re[d] - nr * lam_im[d]) / den
        bb_re = f_re[..., None] * b_re - f_im[..., None] * b_im
        bb_im = f_re[..., None] * b_im + f_im[..., None] * b_re
        w_re = pr[:lk, :, :, None] * bb_re - pi[:lk, :, :, None] * bb_im
        w_im = pr[:lk, :, :, None] * bb_im + pi[:lk, :, :, None] * bb_re
        k_dir.append(jnp.einsum('gop,jgph->jgoh', c_re, w_re, precision=HIGHEST)
                     - jnp.einsum('gop,jgph->jgoh', c_im, w_im, precision=HIGHEST))
        order = slice(None, None, -1) if d == 0 else slice(None)
        bst_dir.append(jnp.stack([w_re[order], w_im[order]], axis=0).transpose(2, 1, 4, 0, 3))
        kk = jnp.arange(1, lk + 1) if d == 0 else jnp.arange(lk, 0, -1)
        ar, ai = pr[kk], pi[kk]
        cr = jnp.einsum('gop,tgp->gpto', c_re, ar) - jnp.einsum('gop,tgp->gpto', c_im, ai)
        ci = -(jnp.einsum('gop,tgp->gpto', c_re, ai) + jnp.einsum('gop,tgp->gpto', c_im, ar))
        coff_dir.append(jnp.stack([cr, ci], axis=1))
        for n_pow in (lk, seg_ctx, seg_lat):
            lam_rows += [jnp.exp(n_pow * e_re) * jnp.cos(n_pow * ang), jnp.exp(n_pow * e_re) * jnp.sin(n_pow * ang)]
    s_i, t_i = jnp.arange(lk)[:, None], jnp.arange(lk)[None, :]
    kf = jnp.where((t_i >= s_i)[..., None, None, None], k_dir[0][jnp.clip(t_i - s_i, 0, lk - 1)], 0.0)
    kb = jnp.where((s_i >= t_i)[..., None, None, None], k_dir[1][jnp.clip(s_i - t_i, 0, lk - 1)], 0.0)
    dk = (jnp.eye(lk, dtype=F32)[:, :, None, None, None] * jnp.eye(h_, dtype=F32)[None, None, None]
          * d_skip.reshape(g_, h_)[None, None, :, :, None])
    tz = (kf + kb + dk).transpose(2, 0, 4, 1, 3).reshape(S5_PAIRS, 2, lk * h_, lk * h_)
    bst = jnp.stack(bst_dir, axis=3).reshape(S5_PAIRS, 2, lk, h_, 2, 2, p_)
    bst = jnp.einsum('aishdqp,ij->aishdqjp', bst, eye2).reshape(S5_PAIRS, 2 * lk * h_, 8 * p_)
    coff = jnp.stack(coff_dir, axis=1).reshape(S5_PAIRS, 2, 2, 2, p_, lk, h_)
    coff = jnp.einsum('aidqpto,ij->adqipjto', coff, eye2).reshape(S5_PAIRS, 8 * p_, 2 * lk * h_)
    lam = jnp.stack([r.reshape(S5_PAIRS, 2 * p_) for r in lam_rows], axis=1)
    order12 = jnp.array([0, 1, 6, 7, 2, 3, 8, 9, 4, 5, 10, 11])
    lam = jnp.concatenate([lam[:, order12], jnp.zeros((S5_PAIRS, 4, 2 * p_), F32)], axis=1)
    return tz.astype(BF16), bst.astype(BF16), coff.astype(BF16), lam


def _s5_kernel(u_ref, tz_ref, bst_ref, coff_ref, lam_ref, y_ref, x_s, h_s, t_s, g_s, *, n_lat, n_ctx, bsz):
    nseg = SUBLANES // bsz
    u = u_ref[0]
    x_s[...] = jnp.dot(u, bst_ref[0], preferred_element_type=F32)
    lam = lam_ref[0]

    def scan(base, nsteps, rev, co, lr, li, init, store):
        def body(k, carry):
            hr, hi = carry
            q = nsteps - 1 - k if rev else k
            r0 = pl.multiple_of(base + q * SUBLANES, SUBLANES)
            if store:
                h_s[pl.ds(r0, SUBLANES), co:co + LANES] = hr
                h_s[pl.ds(r0, SUBLANES), co + LANES:co + 2 * LANES] = hi
            xr = x_s[pl.ds(r0, SUBLANES), co:co + LANES]
            xi = x_s[pl.ds(r0, SUBLANES), co + LANES:co + 2 * LANES]
            return lr * hr - li * hi + xr, lr * hi + li * hr + xi
        return lax.fori_loop(0, nsteps, body, init, unroll=4)

    def seg_carry(tot, lr, li, inits, rev):
        t_s[:, :LANES], t_s[:, LANES:] = tot
        finals = []
        for b in range(bsz):
            gr, gi = inits[b]
            for seg in (range(nseg - 1, -1, -1) if rev else range(nseg)):
                s = b * nseg + seg
                g_s[s:s + 1, :LANES], g_s[s:s + 1, LANES:] = gr, gi
                tr, ti = t_s[s:s + 1, :LANES], t_s[s:s + 1, LANES:]
                gr, gi = lr * gr - li * gi + tr, lr * gi + li * gr + ti
            finals.append((gr, gi))
        return (g_s[:, :LANES], g_s[:, LANES:]), finals

    zero8 = (jnp.zeros((SUBLANES, LANES), F32), jnp.zeros((SUBLANES, LANES), F32))
    zero1 = (jnp.zeros((1, LANES), F32), jnp.zeros((1, LANES), F32))
    for d in range(2):
        rev, co = d == 1, d * 2 * LANES
        lr = jnp.broadcast_to(lam[2 * d:2 * d + 1], (SUBLANES, LANES))
        li = jnp.broadcast_to(lam[2 * d + 1:2 * d + 2], (SUBLANES, LANES))
        init1 = [zero1] * bsz
        for base, nsteps, row in ((n_lat * SUBLANES, n_ctx, 4 + 2 * d), (0, n_lat, 8 + 2 * d)):
            tot = scan(base, nsteps, rev, co, lr, li, zero8, False)
            g0, init1 = seg_carry(tot, lam[row:row + 1], lam[row + 1:row + 2], init1, rev)
            scan(base, nsteps, rev, co, lr, li, g0, True)
    half = S5_BLOCK * S5_GROUP
    y = jnp.dot(h_s[...].astype(BF16), coff_ref[0], preferred_element_type=F32)
    y = y + jnp.concatenate([jnp.dot(u[:, :half], tz_ref[0, 0], preferred_element_type=F32),
                             jnp.dot(u[:, half:], tz_ref[0, 1], preferred_element_type=F32)], axis=1)
    y_ref[0] = jax.nn.gelu(y).astype(BF16)


def _s5_pack(part, nseg):
    bsz, n, _ = part.shape
    steps = n // (nseg * S5_BLOCK)
    p = part.reshape(bsz, nseg, steps, S5_BLOCK, S5_PAIRS, 2, S5_GROUP)
    return p.transpose(4, 2, 0, 1, 5, 3, 6).reshape(S5_PAIRS, steps * bsz * nseg, 2 * S5_BLOCK * S5_GROUP)


def _s5_unpack(rows, bsz, nseg):
    steps = rows.shape[1] // (bsz * nseg)
    p = rows.reshape(S5_PAIRS, steps, bsz, nseg, 2, S5_BLOCK, S5_GROUP)
    return p.transpose(2, 3, 1, 5, 0, 4, 6).reshape(bsz, nseg * steps * S5_BLOCK, S5_WIDTH)


def s5_mix(u, lc, ops):
    bsz, t, _ = u.shape
    assert SUBLANES % bsz == 0
    nseg = SUBLANES // bsz
    tz, bst, coff, lam = ops
    n_lat, n_ctx = (t - lc) // (nseg * S5_BLOCK), lc // (nseg * S5_BLOCK)
    rows = (n_lat + n_ctx) * SUBLANES
    width = 2 * S5_BLOCK * S5_GROUP
    up = jnp.concatenate([_s5_pack(u[:, lc:], nseg), _s5_pack(u[:, :lc], nseg)], axis=1)
    blk = lambda *shape: pl.BlockSpec((1,) + shape, lambda i: (i,) + (0,) * len(shape))
    y = pl.pallas_call(
        functools.partial(_s5_kernel, n_lat=n_lat, n_ctx=n_ctx, bsz=bsz),
        grid=(S5_PAIRS,),
        in_specs=[blk(rows, width), blk(2, width // 2, width // 2), blk(width, 4 * LANES), blk(4 * LANES, width),
                  blk(16, LANES)],
        out_specs=blk(rows, width),
        out_shape=jax.ShapeDtypeStruct((S5_PAIRS, rows, width), BF16),
        scratch_shapes=[pltpu.VMEM((rows, 4 * LANES), F32), pltpu.VMEM((rows, 4 * LANES), F32),
                        pltpu.VMEM((SUBLANES, 2 * LANES), F32), pltpu.VMEM((SUBLANES, 2 * LANES), F32)],
        compiler_params=_cparams("arbitrary"),
        name="s5_mix",
    )(up, tz, bst, coff, lam)
    nl = n_lat * SUBLANES
    return jnp.concatenate([_s5_unpack(y[:, nl:], bsz, nseg), _s5_unpack(y[:, :nl], bsz, nseg)], axis=1)


def s5_compact_operators(lam_re, lam_im, log_step, b_re, b_im, c_re, c_im, d_skip, seg_ctx, seg_lat):
    g_, p_, h_, lk = S5_GROUPS, S5_STATE, S5_GROUP, S5_BLOCK
    j = jnp.arange(lk, dtype=F32)[:, None, None]
    eye2 = jnp.eye(2, dtype=F32)
    k_dir, bb_dir, lam_rows = [], [], []
    for d in range(2):
        step = jnp.exp(log_step[d])[:, None]
        e_re, ang = lam_re[d] * step, lam_im[d] * step
        pr = jnp.exp(j * e_re) * jnp.cos(j * ang)
        pi = jnp.exp(j * e_re) * jnp.sin(j * ang)
        den = lam_re[d] * lam_re[d] + lam_im[d] * lam_im[d]
        nr = pr[1] - 1.0
        f_re = (nr * lam_re[d] + pi[1] * lam_im[d]) / den
        f_im = (pi[1] * lam_re[d] - nr * lam_im[d]) / den
        bb_re = f_re[..., None] * b_re - f_im[..., None] * b_im
        bb_im = f_re[..., None] * b_im + f_im[..., None] * b_re
        w_re = pr[:, :, :, None] * bb_re - pi[:, :, :, None] * bb_im
        w_im = pr[:, :, :, None] * bb_im + pi[:, :, :, None] * bb_re
        k_dir.append(jnp.einsum('gop,jgph->jgho', c_re, w_re, precision=HIGHEST)
                     - jnp.einsum('gop,jgph->jgho', c_im, w_im, precision=HIGHEST))
        bb_dir.append(jnp.stack([bb_re, bb_im], axis=0).transpose(1, 3, 0, 2))
        for n_pow in (lk, seg_ctx, seg_lat, 1):
            lam_rows += [jnp.exp(n_pow * e_re) * jnp.cos(n_pow * ang), jnp.exp(n_pow * e_re) * jnp.sin(n_pow * ang)]
    mid = k_dir[0][0] + k_dir[1][0] + jnp.eye(h_, dtype=F32)[None] * d_skip.reshape(g_, h_, 1)
    taps = jnp.concatenate([k_dir[1][1:][::-1], mid[None], k_dir[0][1:]], axis=0)
    taps = taps.transpose(1, 2, 0, 3).reshape(S5_PAIRS, 2, h_, 2 * lk - 1, h_)
    kfull = jnp.einsum('aihmo,ij->aihmjo', taps, eye2).reshape(S5_PAIRS, 2 * h_, (2 * lk - 1) * 2 * h_)
    kfull = jnp.pad(kfull, ((0, 0), (0, 0), (0, 2 * lk * 2 * h_ - kfull.shape[2])))
    bbp = jnp.stack(bb_dir, axis=1).reshape(S5_PAIRS, 2, 2, h_, 2, p_)
    bbp = jnp.einsum('aidhqp,ij->adihqjp', bbp, eye2).reshape(S5_PAIRS, 2, 2 * h_, 4 * p_)
    cpt = jnp.stack([c_re, -c_im], axis=2).reshape(S5_PAIRS, 2, h_, 2, p_)
    cpt = jnp.einsum('ajoqp,ij->ajoqip', cpt, eye2).reshape(S5_PAIRS, 2 * h_, 4 * p_)
    order16 = [0, 1, 8, 9, 2, 3, 10, 11, 4, 5, 12, 13, 6, 7, 14, 15]
    lam = jnp.stack([lam_rows[r].reshape(S5_PAIRS, 2 * p_) for r in order16], axis=1)
    return kfull, bbp, cpt, lam


def _s5_fused_kernel(z_ref, kf_ref, bbp_ref, cpt_ref, lam_ref, y_ref, tz_s, bst_s, cft_s, x_s, h_s, t_s, g_s,
                     *, n_lat, n_ctx, bsz):
    nseg = SUBLANES // bsz
    pw = 2 * S5_GROUP
    lam = lam_ref[0]
    kf = kf_ref[0]
    for tt in range(S5_BLOCK):
        off = (S5_BLOCK - 1 - tt) * pw
        tz_s[tt * pw:(tt + 1) * pw, :] = kf[:, off:off + S5_BLOCK * pw].astype(BF16)
    for d in range(2):
        co = d * 2 * LANES
        lr = jnp.broadcast_to(lam[12 + 2 * d:13 + 2 * d], (pw, LANES))
        li = jnp.broadcast_to(lam[13 + 2 * d:14 + 2 * d], (pw, LANES))
        order = range(S5_BLOCK - 1, -1, -1) if d == 0 else range(S5_BLOCK)
        wr, wi = bbp_ref[0, d, :, :LANES], bbp_ref[0, d, :, LANES:]
        for tt in order:
            bst_s[tt * pw:(tt + 1) * pw, co:co + LANES] = wr.astype(BF16)
            bst_s[tt * pw:(tt + 1) * pw, co + LANES:co + 2 * LANES] = wi.astype(BF16)
            wr, wi = lr * wr - li * wi, lr * wi + li * wr
        xr, xi = cpt_ref[0, :, :LANES], cpt_ref[0, :, LANES:]
        for tt in (range(S5_BLOCK) if d == 0 else range(S5_BLOCK - 1, -1, -1)):
            xr, xi = lr * xr + li * xi, lr * xi - li * xr
            cft_s[tt * pw:(tt + 1) * pw, co:co + LANES] = xr.astype(BF16)
            cft_s[tt * pw:(tt + 1) * pw, co + LANES:co + 2 * LANES] = xi.astype(BF16)

    u = z_ref[0].astype(BF16)
    x = jnp.dot(u, bst_s[...], preferred_element_type=F32)
    for k in range(4):
        x_s[k] = x[:, k * LANES:(k + 1) * LANES]

    def scan(base, nsteps, stride, rev, cr, lr, li, init, store):
        def body(k, carry):
            hr, hi = carry
            q = nsteps - 1 - k if rev else k
            rows = pl.ds(base + q, SUBLANES, stride=stride)
            if store:
                h_s[cr, rows, :] = hr
                h_s[cr + 1, rows, :] = hi
            return lr * hr - li * hi + x_s[cr, rows, :], lr * hi + li * hr + x_s[cr + 1, rows, :]
        return lax.fori_loop(0, nsteps, body, init, unroll=4)

    def seg_carry(tot, lr, li, inits, rev):
        t_s[:, :LANES], t_s[:, LANES:] = tot
        finals = []
        for b in range(bsz):
            gr, gi = inits[b]
            for seg in (range(nseg - 1, -1, -1) if rev else range(nseg)):
                s = b * nseg + seg
                g_s[s:s + 1, :LANES], g_s[s:s + 1, LANES:] = gr, gi
                tr, ti = t_s[s:s + 1, :LANES], t_s[s:s + 1, LANES:]
                gr, gi = lr * gr - li * gi + tr, lr * gi + li * gr + ti
            finals.append((gr, gi))
        return (g_s[:, :LANES], g_s[:, LANES:]), finals

    zero8 = (jnp.zeros((SUBLANES, LANES), F32), jnp.zeros((SUBLANES, LANES), F32))
    zero1 = (jnp.zeros((1, LANES), F32), jnp.zeros((1, LANES), F32))
    for d in range(2):
        rev = d == 1
        lr = jnp.broadcast_to(lam[2 * d:2 * d + 1], (SUBLANES, LANES))
        li = jnp.broadcast_to(lam[2 * d + 1:2 * d + 2], (SUBLANES, LANES))
        init1 = [zero1] * bsz
        for base, nsteps, row in ((n_lat * SUBLANES, n_ctx, 4 + 2 * d), (0, n_lat, 8 + 2 * d)):
            tot = scan(base, nsteps, nsteps, rev, 2 * d, lr, li, zero8, False)
            g0, init1 = seg_carry(tot, lam[row:row + 1], lam[row + 1:row + 2], init1, rev)
            scan(base, nsteps, nsteps, rev, 2 * d, lr, li, g0, True)
    hcat = jnp.concatenate([h_s[k] for k in range(4)], axis=1).astype(BF16)
    y = lax.dot_general(hcat, cft_s[...], (((1,), (1,)), ((), ())), preferred_element_type=F32)
    y = y + jnp.dot(u, tz_s[...], preferred_element_type=F32)
    y_ref[0] = jax.nn.gelu(y)


def s5_mix_packed(z, bsz, lc, n, ops):
    assert SUBLANES % bsz == 0
    nseg = SUBLANES // bsz
    kfull, bbp, cpt, lam = ops
    n_lat, n_ctx = n // (nseg * S5_BLOCK), lc // (nseg * S5_BLOCK)
    rows, width = z.shape[1], z.shape[2]
    assert rows == (n_lat + n_ctx) * SUBLANES
    blk = lambda *shape: pl.BlockSpec((1,) + shape, lambda i: (i,) + (0,) * len(shape))
    return pl.pallas_call(
        functools.partial(_s5_fused_kernel, n_lat=n_lat, n_ctx=n_ctx, bsz=bsz),
        grid=(S5_PAIRS,),
        in_specs=[blk(rows, width), blk(*kfull.shape[1:]), blk(*bbp.shape[1:]), blk(*cpt.shape[1:]), blk(16, LANES)],
        out_specs=blk(rows, width),
        out_shape=jax.ShapeDtypeStruct((S5_PAIRS, rows, width), F32),
        scratch_shapes=[pltpu.VMEM((width, width), BF16), pltpu.VMEM((width, 4 * LANES), BF16),
                        pltpu.VMEM((width, 4 * LANES), BF16),
                        pltpu.VMEM((4, rows, LANES), F32), pltpu.VMEM((4, rows, LANES), F32),
                        pltpu.VMEM((SUBLANES, 2 * LANES), F32), pltpu.VMEM((SUBLANES, 2 * LANES), F32)],
        compiler_params=_cparams("arbitrary"),
        name="s5_mix",
    )(z, kfull, bbp, cpt, lam)


def _glu_kernel(y_ref, wa_ref, wb_ref, ba_ref, bb_ref, o_ref, a_s):
    pw = 2 * S5_GROUP
    per = LANES // pw
    blocks = ROW_TILE // S5_BLOCK
    for tt in range(S5_BLOCK):
        for k in range(S5_WIDTH // LANES):
            a_s[k, pl.ds(tt, blocks, stride=S5_BLOCK), :] = jnp.concatenate(
                [y_ref[k * per + j, :, tt * pw:(tt + 1) * pw] for j in range(per)], axis=1)
    y = jnp.concatenate([a_s[k] for k in range(S5_WIDTH // LANES)], axis=1).astype(BF16)
    a = jnp.dot(y, wa_ref[...], preferred_element_type=F32) + ba_ref[...]
    g = jnp.dot(y, wb_ref[...], preferred_element_type=F32) + bb_ref[...]
    o_ref[0] = (a * jax.nn.sigmoid(g)).astype(BF16)


def s5_glu(y, bsz, lc, n, w_glu, b_glu):
    w = S5_WIDTH
    wb = w_glu.astype(BF16)
    blocks = ROW_TILE // S5_BLOCK
    y_spec = pl.BlockSpec((S5_PAIRS, blocks, y.shape[2]),
                          lambda b, t: (0, _s5_block_rows(b, t, bsz, n // ROW_TILE), 0))
    return pl.pallas_call(
        _glu_kernel,
        grid=(bsz, (lc + n) // ROW_TILE),
        in_specs=[y_spec, _const_spec((w, w)), _const_spec((w, w)), _const_spec((1, w)), _const_spec((1, w))],
        out_specs=pl.BlockSpec((1, ROW_TILE, w), lambda b, i: (b, i, 0)),
        out_shape=jax.ShapeDtypeStruct((bsz, lc + n, w), BF16),
        scratch_shapes=[pltpu.VMEM((w // LANES, ROW_TILE, LANES), F32)],
        compiler_params=_cparams("arbitrary", "arbitrary"),
        name="s5_glu",
    )(y, wb[:, :w], wb[:, w:], b_glu[:w].reshape(1, w), b_glu[w:].reshape(1, w))


def _softplus(x):
    return jnp.maximum(x, 0.0) + jnp.log(1.0 + jnp.exp(-jnp.abs(x)))


def _ssd_chunk_id(k, rev, nc, ncc):
    if not rev:
        return k
    return jnp.where(k < ncc, ncc - 1 - k, nc - 1 - (k - ncc))


def _ssd_kernel(*refs, rev, nc, ncc):
    if rev:
        (x_ref, xp_ref, xn_ref, dt_ref, dtt_ref, cw_ref, cb_ref, bias_ref, biast_ref, alog_ref, alogt_ref,
         z_ref, yf_ref, dsk_ref, nw_ref, o_ref, st_s, xp_s, y_s) = refs
    else:
        (x_ref, xp_ref, xn_ref, dt_ref, dtt_ref, cw_ref, cb_ref, bias_ref, biast_ref, alog_ref, alogt_ref,
         o_ref, st_s, xp_s) = refs
    lch = SSD_CHUNK
    k = pl.program_id(1)
    c = _ssd_chunk_id(k, rev, nc, ncc)

    @pl.when(k == 0)
    def _():
        st_s[...] = jnp.zeros(st_s.shape, F32)

    first = jnp.logical_or(c == 0, c == ncc)
    last = jnp.logical_or(c == ncc - 1, c == nc - 1)
    xp_s[0:SUBLANES, :] = jnp.where(first, 0.0, xp_ref[0])
    xp_s[SUBLANES:SUBLANES + lch, :] = x_ref[0]
    xp_s[SUBLANES + lch:2 * SUBLANES + lch, :] = jnp.where(last, 0.0, xn_ref[0])
    acc = cb_ref[...] + cw_ref[0:1, :] * xp_s[SUBLANES - 2:SUBLANES - 2 + lch, :]
    for tap in range(1, SSD_CONV):
        acc = acc + cw_ref[tap:tap + 1, :] * xp_s[SUBLANES - 2 + tap:SUBLANES - 2 + tap + lch, :]
    xc = acc * jax.nn.sigmoid(acc)
    gn = SSD_GROUPS * SSD_STATE
    xs = xc[:, :SSD_WIDTH]
    bm = xc[:, SSD_WIDTH:SSD_WIDTH + gn]
    cm = xc[:, SSD_WIDTH + gn:]

    dt = _softplus(dt_ref[0] + bias_ref[...])
    dtt = _softplus(dtt_ref[0] + biast_ref[...])
    da = dt * -jnp.exp(alog_ref[...])
    dat = dtt * -jnp.exp(alogt_ref[...])
    row_i = lax.broadcasted_iota(jnp.int32, (lch, lch), 0)
    col_i = lax.broadcasted_iota(jnp.int32, (lch, lch), 1)
    tri = (col_i >= row_i) if rev else (col_i <= row_i)
    trit = (row_i >= col_i) if rev else (row_i <= col_i)
    cum = jnp.dot(tri.astype(F32), da, preferred_element_type=F32, precision=HIGHEST)
    cumt = jnp.dot(dat, trit.astype(F32), preferred_element_type=F32, precision=HIGHEST)
    end = 0 if rev else lch - 1
    tot = cum[end:end + 1, :]
    wt = dtt * jnp.exp(cumt[:, end:end + 1] - cumt)
    lane = lax.broadcasted_iota(jnp.int32, (1, LANES), 1)
    low = lane < SSD_HEAD_DIM
    heads_per_group = SSD_HEADS // SSD_GROUPS
    for g in range(SSD_GROUPS):
        bg = bm[:, g * SSD_STATE:(g + 1) * SSD_STATE]
        cg = cm[:, g * SSD_STATE:(g + 1) * SSD_STATE].astype(BF16)
        scores = lax.dot_general(cg, bg.astype(BF16), (((1,), (1,)), ((), ())), preferred_element_type=F32)
        bgt = bg.T
        st = st_s[g]
        yoff = jnp.dot(cg, st.astype(BF16), preferred_element_type=F32)
        for jp in range(heads_per_group // 2):
            ha = g * heads_per_group + 2 * jp
            lo = (ha // 2) * LANES
            sl = jp * LANES
            xsp = xs[:, lo:lo + LANES]
            xblk = jnp.concatenate([jnp.where(low, xsp, 0.0), jnp.where(low, 0.0, xsp)], axis=0).astype(BF16)
            ms, ecols, lhs2 = [], [], []
            for h in (ha, ha + 1):
                col = cum[:, h:h + 1]
                seg = col - cumt[h:h + 1, :]
                dec = jnp.exp(jnp.where(tri, seg, -jnp.inf))
                ms.append((scores * dec * dtt[h:h + 1, :]).astype(BF16))
                ecols.append(jnp.exp(col))
                lhs2.append((bgt * wt[h:h + 1, :]).astype(BF16))
            ydiag = jnp.dot(jnp.concatenate(ms, axis=1), xblk, preferred_element_type=F32)
            y_pair = ydiag + jnp.where(low, ecols[0], ecols[1]) * yoff[:, sl:sl + LANES]
            upd = jnp.dot(jnp.concatenate(lhs2, axis=1), xblk, preferred_element_type=F32)
            cd = jnp.where(low, jnp.exp(tot[:, ha:ha + 1]), jnp.exp(tot[:, ha + 1:ha + 2]))
            st_s[g, :, sl:sl + LANES] = cd * st[:, sl:sl + LANES] + upd
            if rev:
                y_s[:, lo:lo + LANES] = y_pair
            else:
                o_ref[0, :, lo:lo + LANES] = y_pair
    if rev:
        y = yf_ref[0] + y_s[...] + dsk_ref[...] * xs
        z = z_ref[0]
        o_ref[0] = (_rms(y * (z * jax.nn.sigmoid(z))) * nw_ref[...]).astype(BF16)


def ssd_mix(z, xbc, dtf, dtb, lc, conv_w, conv_b, dt_bias, a_log, d_skip, norm_w):
    bsz, t, wx = xbc.shape
    lch = SSD_CHUNK
    nc, ncc = t // lch, lc // lch
    hb = lch // SUBLANES
    dsk = jnp.repeat(d_skip, SSD_HEAD_DIM).reshape(1, SSD_WIDTH)
    yf = None
    for rev in (False, True):
        cid = functools.partial(_ssd_chunk_id, rev=rev, nc=nc, ncc=ncc)
        d = int(rev)
        dt = dtb if rev else dtf
        row = lambda wd: pl.BlockSpec((1, lch, wd), lambda b, k: (b, cid(k), 0))
        in_specs = [row(wx),
                    pl.BlockSpec((1, SUBLANES, wx), lambda b, k: (b, jnp.maximum(cid(k) * hb - 1, 0), 0)),
                    pl.BlockSpec((1, SUBLANES, wx), lambda b, k: (b, jnp.minimum((cid(k) + 1) * hb, nc * hb - 1), 0)),
                    row(SSD_HEADS),
                    pl.BlockSpec((1, SSD_HEADS, lch), lambda b, k: (b, 0, cid(k))),
                    _const_spec((SSD_CONV, wx)), _const_spec((1, wx)),
                    _const_spec((1, SSD_HEADS)), _const_spec((SSD_HEADS, 1)),
                    _const_spec((1, SSD_HEADS)), _const_spec((SSD_HEADS, 1))]
        args = [xbc, xbc, xbc, dt, jnp.swapaxes(dt, 1, 2), conv_w, conv_b.reshape(1, wx),
                dt_bias[d].reshape(1, SSD_HEADS), dt_bias[d].reshape(SSD_HEADS, 1),
                a_log[d].reshape(1, SSD_HEADS), a_log[d].reshape(SSD_HEADS, 1)]
        scratch = [pltpu.VMEM((SSD_GROUPS, SSD_STATE, SSD_WIDTH // SSD_GROUPS), F32),
                   pltpu.VMEM((lch + 2 * SUBLANES, wx), F32)]
        if rev:
            in_specs += [row(SSD_WIDTH), row(SSD_WIDTH), _const_spec((1, SSD_WIDTH)), _const_spec((1, SSD_WIDTH))]
            args += [z, yf, dsk, norm_w.reshape(1, SSD_WIDTH)]
            scratch += [pltpu.VMEM((lch, SSD_WIDTH), F32)]
        out = pl.pallas_call(
            functools.partial(_ssd_kernel, rev=rev, nc=nc, ncc=ncc),
            grid=(bsz, nc),
            in_specs=in_specs,
            out_specs=row(SSD_WIDTH),
            out_shape=jax.ShapeDtypeStruct((bsz, t, SSD_WIDTH), BF16 if rev else F32),
            scratch_shapes=scratch,
            compiler_params=_cparams("arbitrary", "arbitrary"),
            name="ssd_bwd" if rev else "ssd_fwd",
        )(*args)
        yf = out
    return out


def _outproj_kernel(ys_ref, yd_ref, xc_ref, xl_ref, wt_ref, wb_ref, g1_ref, gate_ref, g2_ref, sc_ref, sh_ref,
                    wr_ref, xlo_ref, xco_ref, h2_ref, lg_ref):
    b, t = pl.program_id(0), pl.program_id(1)
    row = _mod_row(t, b)
    y = (jnp.dot(ys_ref[0], wt_ref[...], preferred_element_type=F32)
         + jnp.dot(yd_ref[0], wb_ref[...], preferred_element_type=F32))
    x = jnp.where(t == 0, xc_ref[0], xl_ref[0])
    xn = x + gate_ref[pl.ds(row, 1), :] * (_rms(y) * g1_ref[...])

    @pl.when(t == 0)
    def _():
        xco_ref[0] = xn

    @pl.when(t > 0)
    def _():
        xlo_ref[0] = xn

    h2 = _rms(xn) * g2_ref[...]
    h2 = h2 * (1.0 + sc_ref[pl.ds(row, 1), :]) + sh_ref[pl.ds(row, 1), :]
    h2_ref[0] = h2.reshape(h2_ref.shape[1:])
    lg_ref[0] = jnp.dot(h2.astype(BF16), wr_ref[...], preferred_element_type=F32)


def out_proj(y_s5, y_ssd, xc, xl, w_out, g1, gate1, g2, sc2, sh2, w_router):
    bsz, lc, d = xc.shape
    n = xl.shape[1]
    nt = (lc + n) // ROW_TILE
    w = w_out.astype(BF16)
    hw = S5_WIDTH
    tile = lambda wd: pl.BlockSpec((1, ROW_TILE, wd), lambda b, t: (b, t, 0))
    pair_out = [pl.BlockSpec((1, ROW_TILE, d), lambda b, t: (b, jnp.maximum(t - 1, 0), 0)),
                pl.BlockSpec((1, ROW_TILE, d), lambda b, t: (b, 0, 0))]
    return pl.pallas_call(
        _outproj_kernel,
        grid=(bsz, nt),
        in_specs=[tile(hw), tile(SSD_WIDTH)] + _pair_specs(lc, d)
        + [_const_spec((hw, d)), _const_spec((SSD_WIDTH, d)), _const_spec((1, d)), _const_spec((8, d)),
           _const_spec((1, d)), _const_spec((8, d)), _const_spec((8, d)), _const_spec((d, N_EXPERTS))],
        out_specs=pair_out + [pl.BlockSpec((1, ROW_TILE, d // LANES, LANES), lambda b, t: (b, t, 0, 0)),
                              tile(N_EXPERTS)],
        out_shape=[jax.ShapeDtypeStruct((bsz, n, d), F32), jax.ShapeDtypeStruct((bsz, lc, d), F32),
                   jax.ShapeDtypeStruct((bsz, lc + n, d // LANES, LANES), F32),
                   jax.ShapeDtypeStruct((bsz, lc + n, N_EXPERTS), F32)],
        compiler_params=_cparams("arbitrary", "arbitrary"),
        name="out_proj",
    )(y_s5, y_ssd, xc, xl, w[:hw], w[hw:], g1.reshape(1, d), gate1, g2.reshape(1, d), sc2, sh2,
      w_router.astype(BF16))


def _row_blocks(rows):
    return [(r0, min(MOE_ROW_BLOCK, rows - r0)) for r0 in range(0, rows, MOE_ROW_BLOCK)]


def _moe_kernel(idx_ref, h2_hbm, gate_ref, wg_ref, wu_ref, wd_ref, o_ref, xf_s, xb_s, h_s, sem, *, rows, nf, nsteps):
    e, s = pl.program_id(0), pl.program_id(1)
    ne = pl.num_programs(0)
    d = xb_s.shape[1]

    def row_copy(src_row, r):
        return pltpu.make_async_copy(h2_hbm.at[pl.ds(src_row, 1)], xf_s.at[pl.ds(r, 1)], sem.at[0])

    def issue(ex, lo, count):
        def body(k, c):
            r = lo + k
            row_copy(idx_ref[ex * rows + r], r).start()
            return c
        lax.fori_loop(0, count, body, 0, unroll=8)

    @pl.when(jnp.logical_and(e == 0, s == 0))
    def _():
        issue(0, 0, rows)

    @pl.when(s == 0)
    def _():
        pltpu.make_async_copy(h2_hbm.at[pl.ds(0, rows)], xf_s, sem.at[0]).wait()
        for r0, rb in _row_blocks(rows):
            xb_s[r0:r0 + rb, :] = xf_s[r0:r0 + rb].reshape(rb, d).astype(BF16)

    assert rows % MOE_GATHER_SLICES == 0 and nsteps > MOE_GATHER_SLICES
    per_step = rows // MOE_GATHER_SLICES

    @pl.when(jnp.logical_and(jnp.logical_and(s >= 1, s <= MOE_GATHER_SLICES), e + 1 < ne))
    def _():
        issue(e + 1, (s - 1) * per_step, per_step)

    @pl.when(s < nf)
    def _():
        def phase_a(wg_s, wu_s):
            wg_s[...] = wg_ref[0, 0].astype(BF16)
            wu_s[...] = wu_ref[0, 0].astype(BF16)
            for r0, rb in _row_blocks(rows):
                x = xb_s[r0:r0 + rb, :]
                g = jnp.dot(x, wg_s[...], preferred_element_type=F32)
                u = jnp.dot(x, wu_s[...], preferred_element_type=F32)
                h_s[s, r0:r0 + rb, :] = ((g * jax.nn.sigmoid(g)) * u).astype(BF16)
        pl.run_scoped(phase_a, pltpu.VMEM(wg_ref.shape[2:], BF16), pltpu.VMEM(wu_ref.shape[2:], BF16))

    @pl.when(s >= nf)
    def _():
        def phase_b(wd_s):
            wd_s[...] = wd_ref[0, 0].astype(BF16)
            for r0, rb in _row_blocks(rows):
                h = jnp.concatenate([h_s[f, r0:r0 + rb, :] for f in range(nf)], axis=1)
                y = jnp.dot(h, wd_s[...], preferred_element_type=F32) * gate_ref[0, r0:r0 + rb, :]
                o_ref[0, r0:r0 + rb] = y.reshape((rb,) + o_ref.shape[2:])
        pl.run_scoped(phase_b, pltpu.VMEM(wd_ref.shape[2:], BF16))


def moe_ffn(idx_rows, h2, gate, layer, w_gate, w_up, w_down):
    e, r, _ = gate.shape
    d, ff = w_gate.shape[2], w_gate.shape[3]
    tf, tn = MOE_F_TILE, MOE_N_TILE
    nf, nn = ff // tf, d // tn
    fcl = lambda s: jnp.minimum(s, nf - 1)
    ncl = lambda s: jnp.maximum(s - nf, 0)
    grid_spec = pltpu.PrefetchScalarGridSpec(
        num_scalar_prefetch=1,
        grid=(e, nf + nn),
        in_specs=[pl.BlockSpec(memory_space=pl.ANY),
                  pl.BlockSpec((1, r, 1), lambda i, s, idx: (i, 0, 0)),
                  pl.BlockSpec((1, 1, d, tf), lambda i, s, idx: (layer, i, 0, fcl(s))),
                  pl.BlockSpec((1, 1, d, tf), lambda i, s, idx: (layer, i, 0, fcl(s))),
                  pl.BlockSpec((1, 1, ff, tn), lambda i, s, idx: (layer, i, 0, ncl(s)))],
        out_specs=pl.BlockSpec((1, r, tn // LANES, LANES), lambda i, s, idx: (i, 0, ncl(s), 0)),
        scratch_shapes=[pltpu.VMEM((r, d // LANES, LANES), F32), pltpu.VMEM((r, d), BF16),
                        pltpu.VMEM((nf, r, tf), BF16), pltpu.SemaphoreType.DMA((1,))],
    )
    return pl.pallas_call(
        functools.partial(_moe_kernel, rows=r, nf=nf, nsteps=nf + nn),
        grid_spec=grid_spec,
        out_shape=jax.ShapeDtypeStruct((e, r, d // LANES, LANES), F32),
        compiler_params=pltpu.CompilerParams(dimension_semantics=("arbitrary", "arbitrary"),
                                             vmem_limit_bytes=MOE_VMEM_LIMIT),
        name="moe_ffn",
    )(idx_rows, h2, gate, w_gate, w_up, w_down)


def _combine_kernel(src_ref, starts_ref, tokl_ref, out_hbm, *refs, lat_tiles, tiles_per_sample, total, with_ctx):
    if with_ctx:
        xc_ref, xl_ref, gate_ref, g3_ref, xlo_ref, xco_ref, buf, acc, sem = refs
    else:
        xl_ref, gate_ref, g3_ref, xlo_ref, buf, acc, sem = refs
    j = pl.program_id(0)
    nt = pl.num_programs(0)
    ch = COMBINE_CHUNK
    first_of = lambda t: starts_ref[t] // ch
    nch_of = lambda t: jnp.where(starts_ref[t + 1] > starts_ref[t],
                                 (starts_ref[t + 1] + ch - 1) // ch - starts_ref[t] // ch, 0)

    def issue(t, c, slot):
        base = (first_of(t) + c) * ch

        def body(i, carry):
            for p in range(2):
                k = 2 * i + p
                pltpu.make_async_copy(out_hbm.at[pl.ds(src_ref[base + k], 1)], buf.at[slot, pl.ds(k, 1)],
                                      sem.at[slot]).start(priority=p)
            return carry
        lax.fori_loop(0, ch // 2, body, 0, unroll=4)

    nch = nch_of(j)

    @pl.when(jnp.logical_and(j == 0, nch > 0))
    def _():
        issue(j, 0, 0)

    acc[...] = jnp.zeros(acc.shape, F32)
    tok = lax.broadcasted_iota(jnp.int32, (ROW_TILE, ch), 0) + j * ROW_TILE

    def chunk(c, carry):
        slot = c % 2

        @pl.when(c + 1 < nch)
        def _():
            issue(j, c + 1, 1 - slot)

        pltpu.make_async_copy(out_hbm.at[pl.ds(0, ch)], buf.at[slot], sem.at[slot]).wait()
        rows = buf[slot].reshape(ch, acc.shape[1])
        onehot = (tok == tokl_ref[pl.ds(first_of(j) + c, 1), :]).astype(BF16)
        hi = rows.astype(BF16)
        lo = (rows - hi.astype(F32)).astype(BF16)
        acc[...] += (jnp.dot(onehot, hi, preferred_element_type=F32)
                     + jnp.dot(onehot, lo, preferred_element_type=F32))
        return carry
    lax.fori_loop(0, nch, chunk, 0)

    nxt = jnp.minimum(j + 1, nt - 1)

    @pl.when(jnp.logical_and(j + 1 < nt, nch_of(nxt) > 0))
    def _():
        issue(nxt, 0, 0)

    y = gate_ref[pl.ds(jnp.where(j >= lat_tiles, 2, j // tiles_per_sample), 1), :] * (_rms(acc[...]) * g3_ref[...])
    if with_ctx:
        @pl.when(j >= lat_tiles)
        def _():
            xco_ref[0] = xc_ref[0] + y

    @pl.when(j < lat_tiles)
    def _():
        xlo_ref[0] = xl_ref[0] + y


def moe_combine(src, starts, tokl, out_rows, xc, xl, gate2, g3, with_ctx):
    bsz, n, d = xl.shape
    lc = xc.shape[1]
    tps = n // ROW_TILE
    lat_tiles = bsz * tps
    nt = lat_tiles + (bsz if with_ctx else 0)
    lat_idx = lambda j: (jnp.minimum(j, lat_tiles - 1) // tps, jnp.minimum(j, lat_tiles - 1) % tps, 0)
    ctx_idx = lambda j: (jnp.clip(j - lat_tiles, 0, bsz - 1), 0, 0)
    lat_spec = pl.BlockSpec((1, ROW_TILE, d), lambda j, s0, s1: lat_idx(j))
    ctx_spec = pl.BlockSpec((1, lc, d), lambda j, s0, s1: ctx_idx(j))
    grid_spec = pltpu.PrefetchScalarGridSpec(
        num_scalar_prefetch=2,
        grid=(nt,),
        in_specs=[pl.BlockSpec(tokl.shape, lambda j, s0, s1: (0, 0)),
                  pl.BlockSpec(memory_space=pl.ANY)]
        + ([ctx_spec] if with_ctx else []) + [lat_spec,
                                              pl.BlockSpec((8, d), lambda j, s0, s1: (0, 0)),
                                              pl.BlockSpec((1, d), lambda j, s0, s1: (0, 0))],
        out_specs=[lat_spec] + ([ctx_spec] if with_ctx else []),
        scratch_shapes=[pltpu.VMEM((2, COMBINE_CHUNK, d // LANES, LANES), F32), pltpu.VMEM((ROW_TILE, d), F32),
                        pltpu.SemaphoreType.DMA((2,))],
    )
    res = pl.pallas_call(
        functools.partial(_combine_kernel, lat_tiles=lat_tiles, tiles_per_sample=tps, total=src.shape[0],
                          with_ctx=with_ctx),
        grid_spec=grid_spec,
        out_shape=[jax.ShapeDtypeStruct(xl.shape, F32)] + ([jax.ShapeDtypeStruct(xc.shape, F32)] if with_ctx else []),
        compiler_params=_cparams("arbitrary"),
        name="moe_combine",
    )(src, starts, tokl, out_rows, *([xc] if with_ctx else []), xl, gate2, g3.reshape(1, d))
    return (res[0], res[1]) if with_ctx else (res[0], xc)


def rmsnorm(x, w):
    xf = x.astype(F32)
    y = xf * lax.rsqrt(jnp.mean(xf * xf, axis=-1, keepdims=True) + EPS)
    return (y * w.astype(F32)).astype(x.dtype)


def grid_transpose(x):
    b, n = x.shape[:2]
    rows = n // GRID_W
    return x.reshape((b, rows, GRID_W) + x.shape[2:]).swapaxes(1, 2).reshape(x.shape)


def route(logits, lc, with_ctx):
    bsz, t, e = logits.shape
    n = t - lc
    idx_l, gate_l, tok_l = [], [], []
    for lo, m, off in [(lc, n, 0)] + ([(0, lc, bsz * n)] if with_ctx else []):
        cap = CAPACITY_FACTOR * m // e
        aff = jax.nn.softmax(logits[:, lo:lo + m], axis=-1)
        gate, idx = lax.top_k(jnp.swapaxes(aff, 1, 2), cap)
        b_off = jnp.arange(bsz, dtype=jnp.int32)[:, None, None]
        idx_l.append(jnp.swapaxes(b_off * t + lo + idx, 0, 1).reshape(e, bsz * cap))
        tok_l.append(jnp.swapaxes(off + b_off * m + idx, 0, 1).reshape(e, bsz * cap))
        gate_l.append(jnp.swapaxes(gate, 0, 1).reshape(e, bsz * cap))
    idx_rows = jnp.concatenate(idx_l, axis=1).reshape(-1).astype(jnp.int32)
    gate = jnp.concatenate(gate_l, axis=1)[..., None]
    tok = jnp.concatenate(tok_l, axis=1).reshape(-1).astype(jnp.int32)
    total = tok.shape[0]
    assert total % COMBINE_CHUNK == 0
    tok_sorted, src = lax.sort((tok, jnp.arange(total, dtype=jnp.int32)), num_keys=1)
    tiles = (bsz * n + (bsz * lc if with_ctx else 0)) // ROW_TILE
    bounds = jnp.arange(tiles + 1, dtype=jnp.int32) * ROW_TILE
    starts = jnp.sum((tok[None, :] < bounds[:, None]).astype(jnp.int32), axis=1)
    return idx_rows, gate, src, starts, tok_sorted.reshape(total // COMBINE_CHUNK, COMBINE_CHUNK)


def kernel(x, c, ctx, c_ctx, ada_w, ada_b, norm_g, w_in, w_out, s5_lam_re, s5_lam_im, s5_log_step,
           s5_b_re, s5_b_im, s5_c_re, s5_c_im, s5_d, s5_w_glu, s5_b_glu, ssd_conv_w, ssd_conv_b,
           ssd_dt_bias, ssd_a_log, ssd_d, ssd_norm, moe_router, moe_w_gate, moe_w_up, moe_w_down):
    bsz, n, d = x.shape
    lc = ctx.shape[1]
    nseg = SUBLANES // bsz
    cs = jnp.zeros((8, d), F32).at[:bsz].set(c).at[2].set(c_ctx)
    mods = adaln_all(cs, ada_w, ada_b)
    xl, xc = x, ctx
    for i in range(DEPTH):
        col_major = i % 2 == 1
        last = i == DEPTH - 1
        sh1, sc1, g1, sh2, sc2, g2 = [mods[i, :, k * d:(k + 1) * d] for k in range(N_MOD)]
        if col_major:
            xl = grid_transpose(xl)
        u, z, xbc, dtf, dtb = in_proj(xc, xl, norm_g[i, 0], sc1, sh1, w_in[i])
        ops = s5_compact_operators(s5_lam_re[i], s5_lam_im[i], s5_log_step[i], s5_b_re[i], s5_b_im[i],
                                   s5_c_re[i], s5_c_im[i], s5_d[i], lc // nseg, n // nseg)
        y_s5 = s5_glu(s5_mix_packed(u, bsz, lc, n, ops), bsz, lc, n, s5_w_glu[i], s5_b_glu[i])
        y_ssd = ssd_mix(z, xbc, dtf, dtb, lc, ssd_conv_w[i], ssd_conv_b[i], ssd_dt_bias[i],
                        ssd_a_log[i], ssd_d[i], ssd_norm[i])
        xl, xc, h2, logits = out_proj(y_s5, y_ssd, xc, xl, w_out[i], norm_g[i, 1], g1, norm_g[i, 2],
                                      sc2, sh2, moe_router[i])
        idx_rows, gate, src, starts, tokl = route(logits, lc, not last)
        out = moe_ffn(idx_rows, h2.reshape((bsz * (lc + n),) + h2.shape[2:]), gate, i,
                      moe_w_gate, moe_w_up, moe_w_down)
        xl, xc = moe_combine(src, starts, tokl, out.reshape((-1,) + out.shape[2:]), xc, xl, g2, norm_g[i, 3],
                             not last)
        if col_major:
            xl = grid_transpose(xl)
    return xl
```

```python
import functools
import math

import jax
import jax.numpy as jnp
from jax import lax
from jax.experimental import pallas as pl
from jax.experimental.pallas import tpu as pltpu

D_MODEL = 2048
DEPTH = 4
GRID_W = 64
EPS = 1e-6
N_MOD = 6

S5_WIDTH = 1024
S5_GROUP = 16
S5_GROUPS = S5_WIDTH // S5_GROUP
S5_STATE = 64

SSD_WIDTH = 1024
SSD_HEAD_DIM = 64
SSD_HEADS = SSD_WIDTH // SSD_HEAD_DIM
SSD_GROUPS = 2
SSD_STATE = 128
SSD_CONV = 5
SSD_XBC = SSD_WIDTH + 2 * SSD_GROUPS * SSD_STATE

N_EXPERTS = 16
CAPACITY_FACTOR = 2
D_FF = 1536

F32 = jnp.float32
BF16 = jnp.bfloat16
HIGHEST = lax.Precision.HIGHEST

LANES = 128
SUBLANES = 8
ROW_TILE = 256
MOE_F_TILE = 512
MOE_ROW_BLOCK = 544
MOE_GATHER_SLICES = 4
MOE_N_TILE = 1024
MOE_VMEM_LIMIT = 60 * 1024 * 1024
COMBINE_CHUNK = 256
S5_BLOCK = 16
S5_PAIRS = S5_GROUPS // 2
SSD_CHUNK = 128
VMEM_LIMIT = 56 * 1024 * 1024


def _cparams(*sem):
    return pltpu.CompilerParams(dimension_semantics=sem, vmem_limit_bytes=VMEM_LIMIT)


def _adaln_kernel(c_ref, w_ref, b_ref, o_ref):
    c = c_ref[...]
    a = (c * jax.nn.sigmoid(c)).astype(BF16)
    o_ref[0] = jnp.dot(a, w_ref[0].astype(BF16), preferred_element_type=F32) + b_ref[0]


def adaln_all(cs, ada_w, ada_b):
    depth, d, n = ada_w.shape
    tn = 1024
    return pl.pallas_call(
        _adaln_kernel,
        grid=(depth, n // tn),
        in_specs=[pl.BlockSpec((8, d), lambda l, j: (0, 0)),
                  pl.BlockSpec((1, d, tn), lambda l, j: (l, 0, j)),
                  pl.BlockSpec((1, 1, tn), lambda l, j: (l, 0, j))],
        out_specs=pl.BlockSpec((1, 8, tn), lambda l, j: (l, 0, j)),
        out_shape=jax.ShapeDtypeStruct((depth, 8, n), F32),
        compiler_params=_cparams("arbitrary", "arbitrary"),
        name="adaln",
    )(cs, ada_w, ada_b.reshape(depth, 1, n))


def _mod_row(t, b):
    return jnp.where(t == 0, 2, b)


def _rms(x):
    return x * lax.rsqrt(jnp.mean(x * x, axis=-1, keepdims=True) + EPS)


def _s5_block_rows(b, t, bsz, lat_tiles):
    return jnp.where(t == 0, bsz * lat_tiles + b, b * lat_tiles + t - 1)


def _inproj_kernel(xc_ref, xl_ref, g_ref, sc_ref, sh_ref, wu_ref, wz_ref, wx_ref, wf_ref, wb_ref,
                   u_ref, z_ref, xbc_ref, dtf_ref, dtb_ref, r_s):
    b, t = pl.program_id(0), pl.program_id(1)
    x = jnp.where(t == 0, xc_ref[0], xl_ref[0])
    row = _mod_row(t, b)
    h = _rms(x) * g_ref[...]
    h = (h * (1.0 + sc_ref[pl.ds(row, 1), :]) + sh_ref[pl.ds(row, 1), :]).astype(BF16)
    u = jnp.dot(h, wu_ref[...], preferred_element_type=F32)
    nk = S5_WIDTH // LANES
    blocks = ROW_TILE // S5_BLOCK
    pw = 2 * S5_GROUP
    for k in range(nk):
        r_s[k] = u[:, k * LANES:(k + 1) * LANES]
    for k in range(nk):
        xts = [r_s[k, pl.ds(tt, blocks, stride=S5_BLOCK), :] for tt in range(S5_BLOCK)]
        for j in range(LANES // pw):
            u_ref[k * (LANES // pw) + j] = jnp.concatenate([xt[:, j * pw:(j + 1) * pw] for xt in xts], axis=1)
    z_ref[0] = jnp.dot(h, wz_ref[...], preferred_element_type=F32)
    xbc_ref[0] = jnp.dot(h, wx_ref[...], preferred_element_type=F32)
    dtf_ref[0] = jnp.dot(h, wf_ref[...], preferred_element_type=F32)
    dtb_ref[0] = jnp.dot(h, wb_ref[...], preferred_element_type=F32)


def _pair_specs(lc, d):
    assert lc == ROW_TILE
    return [pl.BlockSpec((1, ROW_TILE, d), lambda b, t: (b, 0, 0)),
            pl.BlockSpec((1, ROW_TILE, d), lambda b, t: (b, jnp.maximum(t - 1, 0), 0))]


def _const_spec(shape):
    return pl.BlockSpec(shape, lambda *_: (0,) * len(shape), pipeline_mode=pl.Buffered(1))


def in_proj(xc, xl, g, sc, sh, w_in):
    bsz, lc, d = xc.shape
    n = xl.shape[1]
    nt = (lc + n) // ROW_TILE
    s1, s2, s3 = S5_WIDTH, S5_WIDTH + SSD_WIDTH, S5_WIDTH + SSD_WIDTH + SSD_XBC
    w = w_in.astype(BF16)
    ws = [w[:, :s1], w[:, s1:s2], w[:, s2:s3], w[:, s3:s3 + SSD_HEADS], w[:, s3 + SSD_HEADS:]]
    widths = [x.shape[1] for x in ws]
    blocks = ROW_TILE // S5_BLOCK
    s5_rows = bsz * (lc + n) // S5_BLOCK
    s5_lanes = 2 * S5_GROUP * S5_BLOCK
    u_spec = pl.BlockSpec((S5_PAIRS, blocks, s5_lanes),
                          lambda b, t: (0, _s5_block_rows(b, t, bsz, n // ROW_TILE), 0))
    return pl.pallas_call(
        _inproj_kernel,
        grid=(bsz, nt),
        in_specs=_pair_specs(lc, d) + [_const_spec((1, d)), _const_spec((8, d)), _const_spec((8, d))]
        + [_const_spec((d, wd)) for wd in widths],
        out_specs=[u_spec] + [pl.BlockSpec((1, ROW_TILE, wd), lambda b, t: (b, t, 0)) for wd in widths[1:]],
        out_shape=[jax.ShapeDtypeStruct((S5_PAIRS, s5_rows, s5_lanes), F32)]
        + [jax.ShapeDtypeStruct((bsz, lc + n, wd), F32) for wd in widths[1:]],
        scratch_shapes=[pltpu.VMEM((S5_WIDTH // LANES, ROW_TILE, LANES), F32)],
        compiler_params=_cparams("arbitrary", "arbitrary"),
        name="in_proj",
    )(xc, xl, g.reshape(1, d), sc, sh, *ws)


def s5_operators(lam_re, lam_im, log_step, b_re, b_im, c_re, c_im, d_skip, seg_ctx, seg_lat):
    g_, p_, h_, lk = S5_GROUPS, S5_STATE, S5_GROUP, S5_BLOCK
    j = jnp.arange(lk + 1, dtype=F32)[:, None, None]
    eye2 = jnp.eye(2, dtype=F32)
    k_dir, bst_dir, coff_dir, lam_rows = [], [], [], []
    for d in range(2):
        step = jnp.exp(log_step[d])[:, None]
        e_re, ang = lam_re[d] * step, lam_im[d] * step
        pr = jnp.exp(j * e_re) * jnp.cos(j * ang)
        pi = jnp.exp(j * e_re) * jnp.sin(j * ang)
        den = lam_re[d] * lam_re[d] + lam_im[d] * lam_im[d]
        nr = pr[1] - 1.0
        f_re = (nr * lam_re[d] + pi[1] * lam_im[d]) / den
        f_im = (pi[1] * lam_re[d] - nr * lam_im[d]) / den
        bb_re = f_re[..., None] * b_re - f_im[..., None] * b_im
        bb_im = f_re[..., None] * b_im + f_im[..., None] * b_re
        w_re = pr[:lk, :, :, None] * bb_re - pi[:lk, :, :, None] * bb_im
        w_im = pr[:lk, :, :, None] * bb_im + pi[:lk, :, :, None] * bb_re
        k_dir.append(jnp.einsum('gop,jgph->jgoh', c_re, w_re, precision=HIGHEST)
                     - jnp.einsum('gop,jgph->jgoh', c_im, w_im, precision=HIGHEST))
        order = slice(None, None, -1) if d == 0 else slice(None)
        bst_dir.append(jnp.stack([w_re[order], w_im[order]], axis=0).transpose(2, 1, 4, 0, 3))
        kk = jnp.arange(1, lk + 1) if d == 0 else jnp.arange(lk, 0, -1)
        ar, ai = pr[kk], pi[kk]
        cr = jnp.einsum('gop,tgp->gpto', c_re, ar) - jnp.einsum('gop,tgp->gpto', c_im, ai)
        ci = -(jnp.einsum('gop,tgp->gpto', c_re, ai) + jnp.einsum('gop,tgp->gpto', c_im, ar))
        coff_dir.append(jnp.stack([cr, ci], axis=1))
        for n_pow in (lk, seg_ctx, seg_lat):
            lam_rows += [jnp.exp(n_pow * e_re) * jnp.cos(n_pow * ang), jnp.exp(n_pow * e_re) * jnp.sin(n_pow * ang)]
    s_i, t_i = jnp.arange(lk)[:, None], jnp.arange(lk)[None, :]
    kf = jnp.where((t_i >= s_i)[..., None, None, None], k_dir[0][jnp.clip(t_i - s_i, 0, lk - 1)], 0.0)
    kb = jnp.where((s_i >= t_i)[..., None, None, None], k_dir[1][jnp.clip(s_i - t_i, 0, lk - 1)], 0.0)
    dk = (jnp.eye(lk, dtype=F32)[:, :, None, None, None] * jnp.eye(h_, dtype=F32)[None, None, None]
          * d_skip.reshape(g_, h_)[None, None, :, :, None])
    tz = (kf + kb + dk).transpose(2, 0, 4, 1, 3).reshape(S5_PAIRS, 2, lk * h_, lk * h_)
    bst = jnp.stack(bst_dir, axis=3).reshape(S5_PAIRS, 2, lk, h_, 2, 2, p_)
    bst = jnp.einsum('aishdqp,ij->aishdqjp', bst, eye2).reshape(S5_PAIRS, 2 * lk * h_, 8 * p_)
    coff = jnp.stack(coff_dir, axis=1).reshape(S5_PAIRS, 2, 2, 2, p_, lk, h_)
    coff = jnp.einsum('aidqpto,ij->adqipjto', coff, eye2).reshape(S5_PAIRS, 8 * p_, 2 * lk * h_)
    lam = jnp.stack([r.reshape(S5_PAIRS, 2 * p_) for r in lam_rows], axis=1)
    order12 = jnp.array([0, 1, 6, 7, 2, 3, 8, 9, 4, 5, 10, 11])
    lam = jnp.concatenate([lam[:, order12], jnp.zeros((S5_PAIRS, 4, 2 * p_), F32)], axis=1)
    return tz.astype(BF16), bst.astype(BF16), coff.astype(BF16), lam


def _s5_kernel(u_ref, tz_ref, bst_ref, coff_ref, lam_ref, y_ref, x_s, h_s, t_s, g_s, *, n_lat, n_ctx, bsz):
    nseg = SUBLANES // bsz
    u = u_ref[0]
    x_s[...] = jnp.dot(u, bst_ref[0], preferred_element_type=F32)
    lam = lam_ref[0]

    def scan(base, nsteps, rev, co, lr, li, init, store):
        def body(k, carry):
            hr, hi = carry
            q = nsteps - 1 - k if rev else k
            r0 = pl.multiple_of(base + q * SUBLANES, SUBLANES)
            if store:
                h_s[pl.ds(r0, SUBLANES), co:co + LANES] = hr
                h_s[pl.ds(r0, SUBLANES), co + LANES:co + 2 * LANES] = hi
            xr = x_s[pl.ds(r0, SUBLANES), co:co + LANES]
            xi = x_s[pl.ds(r0, SUBLANES), co + LANES:co + 2 * LANES]
            return lr * hr - li * hi + xr, lr * hi + li * hr + xi
        return lax.fori_loop(0, nsteps, body, init, unroll=4)

    def seg_carry(tot, lr, li, inits, rev):
        t_s[:, :LANES], t_s[:, LANES:] = tot
        finals = []
        for b in range(bsz):
            gr, gi = inits[b]
            for seg in (range(nseg - 1, -1, -1) if rev else range(nseg)):
                s = b * nseg + seg
                g_s[s:s + 1, :LANES], g_s[s:s + 1, LANES:] = gr, gi
                tr, ti = t_s[s:s + 1, :LANES], t_s[s:s + 1, LANES:]
                gr, gi = lr * gr - li * gi + tr, lr * gi + li * gr + ti
            finals.append((gr, gi))
        return (g_s[:, :LANES], g_s[:, LANES:]), finals

    zero8 = (jnp.zeros((SUBLANES, LANES), F32), jnp.zeros((SUBLANES, LANES), F32))
    zero1 = (jnp.zeros((1, LANES), F32), jnp.zeros((1, LANES), F32))
    for d in range(2):
        rev, co = d == 1, d * 2 * LANES
        lr = jnp.broadcast_to(lam[2 * d:2 * d + 1], (SUBLANES, LANES))
        li = jnp.broadcast_to(lam[2 * d + 1:2 * d + 2], (SUBLANES, LANES))
        init1 = [zero1] * bsz
        for base, nsteps, row in ((n_lat * SUBLANES, n_ctx, 4 + 2 * d), (0, n_lat, 8 + 2 * d)):
            tot = scan(base, nsteps, rev, co, lr, li, zero8, False)
            g0, init1 = seg_carry(tot, lam[row:row + 1], lam[row + 1:row + 2], init1, rev)
            scan(base, nsteps, rev, co, lr, li, g0, True)
    half = S5_BLOCK * S5_GROUP
    y = jnp.dot(h_s[...].astype(BF16), coff_ref[0], preferred_element_type=F32)
    y = y + jnp.concatenate([jnp.dot(u[:, :half], tz_ref[0, 0], preferred_element_type=F32),
                             jnp.dot(u[:, half:], tz_ref[0, 1], preferred_element_type=F32)], axis=1)
    y_ref[0] = jax.nn.gelu(y).astype(BF16)


def _s5_pack(part, nseg):
    bsz, n, _ = part.shape
    steps = n // (nseg * S5_BLOCK)
    p = part.reshape(bsz, nseg, steps, S5_BLOCK, S5_PAIRS, 2, S5_GROUP)
    return p.transpose(4, 2, 0, 1, 5, 3, 6).reshape(S5_PAIRS, steps * bsz * nseg, 2 * S5_BLOCK * S5_GROUP)


def _s5_unpack(rows, bsz, nseg):
    steps = rows.shape[1] // (bsz * nseg)
    p = rows.reshape(S5_PAIRS, steps, bsz, nseg, 2, S5_BLOCK, S5_GROUP)
    return p.transpose(2, 3, 1, 5, 0, 4, 6).reshape(bsz, nseg * steps * S5_BLOCK, S5_WIDTH)


def s5_mix(u, lc, ops):
    bsz, t, _ = u.shape
    assert SUBLANES % bsz == 0
    nseg = SUBLANES // bsz
    tz, bst, coff, lam = ops
    n_lat, n_ctx = (t - lc) // (nseg * S5_BLOCK), lc // (nseg * S5_BLOCK)
    rows = (n_lat + n_ctx) * SUBLANES
    width = 2 * S5_BLOCK * S5_GROUP
    up = jnp.concatenate([_s5_pack(u[:, lc:], nseg), _s5_pack(u[:, :lc], nseg)], axis=1)
    blk = lambda *shape: pl.BlockSpec((1,) + shape, lambda i: (i,) + (0,) * len(shape))
    y = pl.pallas_call(
        functools.partial(_s5_kernel, n_lat=n_lat, n_ctx=n_ctx, bsz=bsz),
        grid=(S5_PAIRS,),
        in_specs=[blk(rows, width), blk(2, width // 2, width // 2), blk(width, 4 * LANES), blk(4 * LANES, width),
                  blk(16, LANES)],
        out_specs=blk(rows, width),
        out_shape=jax.ShapeDtypeStruct((S5_PAIRS, rows, width), BF16),
        scratch_shapes=[pltpu.VMEM((rows, 4 * LANES), F32), pltpu.VMEM((rows, 4 * LANES), F32),
                        pltpu.VMEM((SUBLANES, 2 * LANES), F32), pltpu.VMEM((SUBLANES, 2 * LANES), F32)],
        compiler_params=_cparams("arbitrary"),
        name="s5_mix",
    )(up, tz, bst, coff, lam)
    nl = n_lat * SUBLANES
    return jnp.concatenate([_s5_unpack(y[:, nl:], bsz, nseg), _s5_unpack(y[:, :nl], bsz, nseg)], axis=1)


def s5_compact_operators(lam_re, lam_im, log_step, b_re, b_im, c_re, c_im, d_skip, seg_ctx, seg_lat):
    g_, p_, h_, lk = S5_GROUPS, S5_STATE, S5_GROUP, S5_BLOCK
    j = jnp.arange(lk, dtype=F32)[:, None, None]
    eye2 = jnp.eye(2, dtype=F32)
    k_dir, bb_dir, lam_rows = [], [], []
    for d in range(2):
        step = jnp.exp(log_step[d])[:, None]
        e_re, ang = lam_re[d] * step, lam_im[d] * step
        pr = jnp.exp(j * e_re) * jnp.cos(j * ang)
        pi = jnp.exp(j * e_re) * jnp.sin(j * ang)
        den = lam_re[d] * lam_re[d] + lam_im[d] * lam_im[d]
        nr = pr[1] - 1.0
        f_re = (nr * lam_re[d] + pi[1] * lam_im[d]) / den
        f_im = (pi[1] * lam_re[d] - nr * lam_im[d]) / den
        bb_re = f_re[..., None] * b_re - f_im[..., None] * b_im
        bb_im = f_re[..., None] * b_im + f_im[..., None] * b_re
        w_re = pr[:, :, :, None] * bb_re - pi[:, :, :, None] * bb_im
        w_im = pr[:, :, :, None] * bb_im + pi[:, :, :, None] * bb_re
        k_dir.append(jnp.einsum('gop,jgph->jgho', c_re, w_re, precision=HIGHEST)
                     - jnp.einsum('gop,jgph->jgho', c_im, w_im, precision=HIGHEST))
        bb_dir.append(jnp.stack([bb_re, bb_im], axis=0).transpose(1, 3, 0, 2))
        for n_pow in (lk, seg_ctx, seg_lat, 1):
            lam_rows += [jnp.exp(n_pow * e_re) * jnp.cos(n_pow * ang), jnp.exp(n_pow * e_re) * jnp.sin(n_pow * ang)]
    mid = k_dir[0][0] + k_dir[1][0] + jnp.eye(h_, dtype=F32)[None] * d_skip.reshape(g_, h_, 1)
    taps = jnp.concatenate([k_dir[1][1:][::-1], mid[None], k_dir[0][1:]], axis=0)
    taps = taps.transpose(1, 2, 0, 3).reshape(S5_PAIRS, 2, h_, 2 * lk - 1, h_)
    kfull = jnp.einsum('aihmo,ij->aihmjo', taps, eye2).reshape(S5_PAIRS, 2 * h_, (2 * lk - 1) * 2 * h_)
    kfull = jnp.pad(kfull, ((0, 0), (0, 0), (0, 2 * lk * 2 * h_ - kfull.shape[2])))
    bbp = jnp.stack(bb_dir, axis=1).reshape(S5_PAIRS, 2, 2, h_, 2, p_)
    bbp = jnp.einsum('aidhqp,ij->adihqjp', bbp, eye2).reshape(S5_PAIRS, 2, 2 * h_, 4 * p_)
    cpt = jnp.stack([c_re, -c_im], axis=2).reshape(S5_PAIRS, 2, h_, 2, p_)
    cpt = jnp.einsum('ajoqp,ij->ajoqip', cpt, eye2).reshape(S5_PAIRS, 2 * h_, 4 * p_)
    order16 = [0, 1, 8, 9, 2, 3, 10, 11, 4, 5, 12, 13, 6, 7, 14, 15]
    lam = jnp.stack([lam_rows[r].reshape(S5_PAIRS, 2 * p_) for r in order16], axis=1)
    return kfull, bbp, cpt, lam


def _s5_fused_kernel(z_ref, kf_ref, bbp_ref, cpt_ref, lam_ref, y_ref, tz_s, bst_s, cft_s, x_s, h_s, t_s, g_s,
                     *, n_lat, n_ctx, bsz):
    nseg = SUBLANES // bsz
    pw = 2 * S5_GROUP
    lam = lam_ref[0, 0]
    kf = kf_ref[0, 0]
    for tt in range(S5_BLOCK):
        off = (S5_BLOCK - 1 - tt) * pw
        tz_s[tt * pw:(tt + 1) * pw, :] = kf[:, off:off + S5_BLOCK * pw].astype(BF16)
    for d in range(2):
        co = d * 2 * LANES
        lr = jnp.broadcast_to(lam[12 + 2 * d:13 + 2 * d], (pw, LANES))
        li = jnp.broadcast_to(lam[13 + 2 * d:14 + 2 * d], (pw, LANES))
        order = range(S5_BLOCK - 1, -1, -1) if d == 0 else range(S5_BLOCK)
        wr, wi = bbp_ref[0, 0, d, :, :LANES], bbp_ref[0, 0, d, :, LANES:]
        for tt in order:
            bst_s[tt * pw:(tt + 1) * pw, co:co + LANES] = wr.astype(BF16)
            bst_s[tt * pw:(tt + 1) * pw, co + LANES:co + 2 * LANES] = wi.astype(BF16)
            wr, wi = lr * wr - li * wi, lr * wi + li * wr
        xr, xi = cpt_ref[0, 0, :, :LANES], cpt_ref[0, 0, :, LANES:]
        for tt in (range(S5_BLOCK) if d == 0 else range(S5_BLOCK - 1, -1, -1)):
            xr, xi = lr * xr + li * xi, lr * xi - li * xr
            cft_s[tt * pw:(tt + 1) * pw, co:co + LANES] = xr.astype(BF16)
            cft_s[tt * pw:(tt + 1) * pw, co + LANES:co + 2 * LANES] = xi.astype(BF16)

    u = z_ref[0].astype(BF16)
    x = jnp.dot(u, bst_s[...], preferred_element_type=F32)
    for k in range(4):
        x_s[k] = x[:, k * LANES:(k + 1) * LANES]

    def scan(base, nsteps, stride, rev, cr, lr, li, init, store):
        def body(k, carry):
            hr, hi = carry
            q = nsteps - 1 - k if rev else k
            rows = pl.ds(base + q, SUBLANES, stride=stride)
            if store:
                h_s[cr, rows, :] = hr
                h_s[cr + 1, rows, :] = hi
            return lr * hr - li * hi + x_s[cr, rows, :], lr * hi + li * hr + x_s[cr + 1, rows, :]
        return lax.fori_loop(0, nsteps, body, init, unroll=4)

    def seg_carry(tot, lr, li, inits, rev):
        t_s[:, :LANES], t_s[:, LANES:] = tot
        finals = []
        for b in range(bsz):
            gr, gi = inits[b]
            for seg in (range(nseg - 1, -1, -1) if rev else range(nseg)):
                s = b * nseg + seg
                g_s[s:s + 1, :LANES], g_s[s:s + 1, LANES:] = gr, gi
                tr, ti = t_s[s:s + 1, :LANES], t_s[s:s + 1, LANES:]
                gr, gi = lr * gr - li * gi + tr, lr * gi + li * gr + ti
            finals.append((gr, gi))
        return (g_s[:, :LANES], g_s[:, LANES:]), finals

    zero8 = (jnp.zeros((SUBLANES, LANES), F32), jnp.zeros((SUBLANES, LANES), F32))
    zero1 = (jnp.zeros((1, LANES), F32), jnp.zeros((1, LANES), F32))
    for d in range(2):
        rev = d == 1
        lr = jnp.broadcast_to(lam[2 * d:2 * d + 1], (SUBLANES, LANES))
        li = jnp.broadcast_to(lam[2 * d + 1:2 * d + 2], (SUBLANES, LANES))
        init1 = [zero1] * bsz
        for base, nsteps, row in ((n_lat * SUBLANES, n_ctx, 4 + 2 * d), (0, n_lat, 8 + 2 * d)):
            tot = scan(base, nsteps, nsteps, rev, 2 * d, lr, li, zero8, False)
            g0, init1 = seg_carry(tot, lam[row:row + 1], lam[row + 1:row + 2], init1, rev)
            scan(base, nsteps, nsteps, rev, 2 * d, lr, li, g0, True)
    hcat = jnp.concatenate([h_s[k] for k in range(4)], axis=1).astype(BF16)
    y = lax.dot_general(hcat, cft_s[...], (((1,), (1,)), ((), ())), preferred_element_type=F32)
    y = y + jnp.dot(u, tz_s[...], preferred_element_type=F32)
    y_ref[0] = jax.nn.gelu(y)


def s5_mix_packed(z, bsz, lc, n, layer, ops):
    assert SUBLANES % bsz == 0
    nseg = SUBLANES // bsz
    kfull, bbp, cpt, lam = ops
    n_lat, n_ctx = n // (nseg * S5_BLOCK), lc // (nseg * S5_BLOCK)
    rows, width = z.shape[1], z.shape[2]
    assert rows == (n_lat + n_ctx) * SUBLANES
    blk = lambda *shape: pl.BlockSpec((1,) + shape, lambda i: (i,) + (0,) * len(shape))
    lblk = lambda a: pl.BlockSpec((1, 1) + a.shape[2:], lambda i: (layer, i) + (0,) * (a.ndim - 2))
    return pl.pallas_call(
        functools.partial(_s5_fused_kernel, n_lat=n_lat, n_ctx=n_ctx, bsz=bsz),
        grid=(S5_PAIRS,),
        in_specs=[blk(rows, width), lblk(kfull), lblk(bbp), lblk(cpt), lblk(lam)],
        out_specs=blk(rows, width),
        out_shape=jax.ShapeDtypeStruct((S5_PAIRS, rows, width), F32),
        scratch_shapes=[pltpu.VMEM((width, width), BF16), pltpu.VMEM((width, 4 * LANES), BF16),
                        pltpu.VMEM((width, 4 * LANES), BF16),
                        pltpu.VMEM((4, rows, LANES), F32), pltpu.VMEM((4, rows, LANES), F32),
                        pltpu.VMEM((SUBLANES, 2 * LANES), F32), pltpu.VMEM((SUBLANES, 2 * LANES), F32)],
        compiler_params=_cparams("arbitrary"),
        name="s5_mix",
    )(z, kfull, bbp, cpt, lam)


def _glu_kernel(y_ref, wa_ref, wb_ref, ba_ref, bb_ref, o_ref, a_s):
    pw = 2 * S5_GROUP
    per = LANES // pw
    blocks = ROW_TILE // S5_BLOCK
    for tt in range(S5_BLOCK):
        for k in range(S5_WIDTH // LANES):
            a_s[k, pl.ds(tt, blocks, stride=S5_BLOCK), :] = jnp.concatenate(
                [y_ref[k * per + j, :, tt * pw:(tt + 1) * pw] for j in range(per)], axis=1)
    y = jnp.concatenate([a_s[k] for k in range(S5_WIDTH // LANES)], axis=1).astype(BF16)
    a = jnp.dot(y, wa_ref[...], preferred_element_type=F32) + ba_ref[...]
    g = jnp.dot(y, wb_ref[...], preferred_element_type=F32) + bb_ref[...]
    o_ref[0] = (a * jax.nn.sigmoid(g)).astype(BF16)


def s5_glu(y, bsz, lc, n, w_glu, b_glu):
    w = S5_WIDTH
    wb = w_glu.astype(BF16)
    blocks = ROW_TILE // S5_BLOCK
    y_spec = pl.BlockSpec((S5_PAIRS, blocks, y.shape[2]),
                          lambda b, t: (0, _s5_block_rows(b, t, bsz, n // ROW_TILE), 0))
    return pl.pallas_call(
        _glu_kernel,
        grid=(bsz, (lc + n) // ROW_TILE),
        in_specs=[y_spec, _const_spec((w, w)), _const_spec((w, w)), _const_spec((1, w)), _const_spec((1, w))],
        out_specs=pl.BlockSpec((1, ROW_TILE, w), lambda b, i: (b, i, 0)),
        out_shape=jax.ShapeDtypeStruct((bsz, lc + n, w), BF16),
        scratch_shapes=[pltpu.VMEM((w // LANES, ROW_TILE, LANES), F32)],
        compiler_params=_cparams("arbitrary", "arbitrary"),
        name="s5_glu",
    )(y, wb[:, :w], wb[:, w:], b_glu[:w].reshape(1, w), b_glu[w:].reshape(1, w))


def _softplus(x):
    return jnp.maximum(x, 0.0) + jnp.log(1.0 + jnp.exp(-jnp.abs(x)))


def _ssd_chunk_id(k, rev, nc, ncc):
    if not rev:
        return k
    return jnp.where(k < ncc, ncc - 1 - k, nc - 1 - (k - ncc))


def _ssd_kernel(*refs, rev, nc, ncc):
    if rev:
        (x_ref, xp_ref, xn_ref, dt_ref, dtt_ref, cw_ref, cb_ref, bias_ref, biast_ref, alog_ref, alogt_ref,
         z_ref, yf_ref, dsk_ref, nw_ref, o_ref, st_s, xp_s, y_s) = refs
    else:
        (x_ref, xp_ref, xn_ref, dt_ref, dtt_ref, cw_ref, cb_ref, bias_ref, biast_ref, alog_ref, alogt_ref,
         o_ref, st_s, xp_s) = refs
    lch = SSD_CHUNK
    k = pl.program_id(1)
    c = _ssd_chunk_id(k, rev, nc, ncc)

    @pl.when(k == 0)
    def _():
        st_s[...] = jnp.zeros(st_s.shape, F32)

    first = jnp.logical_or(c == 0, c == ncc)
    last = jnp.logical_or(c == ncc - 1, c == nc - 1)
    xp_s[0:SUBLANES, :] = jnp.where(first, 0.0, xp_ref[0])
    xp_s[SUBLANES:SUBLANES + lch, :] = x_ref[0]
    xp_s[SUBLANES + lch:2 * SUBLANES + lch, :] = jnp.where(last, 0.0, xn_ref[0])
    acc = cb_ref[...] + cw_ref[0:1, :] * xp_s[SUBLANES - 2:SUBLANES - 2 + lch, :]
    for tap in range(1, SSD_CONV):
        acc = acc + cw_ref[tap:tap + 1, :] * xp_s[SUBLANES - 2 + tap:SUBLANES - 2 + tap + lch, :]
    xc = acc * jax.nn.sigmoid(acc)
    gn = SSD_GROUPS * SSD_STATE
    xs = xc[:, :SSD_WIDTH]
    bm = xc[:, SSD_WIDTH:SSD_WIDTH + gn]
    cm = xc[:, SSD_WIDTH + gn:]

    dt = _softplus(dt_ref[0] + bias_ref[...])
    dtt = _softplus(dtt_ref[0] + biast_ref[...])
    da = dt * -jnp.exp(alog_ref[...])
    dat = dtt * -jnp.exp(alogt_ref[...])
    row_i = lax.broadcasted_iota(jnp.int32, (lch, lch), 0)
    col_i = lax.broadcasted_iota(jnp.int32, (lch, lch), 1)
    tri = (col_i >= row_i) if rev else (col_i <= row_i)
    trit = (row_i >= col_i) if rev else (row_i <= col_i)
    cum = jnp.dot(tri.astype(F32), da, preferred_element_type=F32, precision=HIGHEST)
    cumt = jnp.dot(dat, trit.astype(F32), preferred_element_type=F32, precision=HIGHEST)
    end = 0 if rev else lch - 1
    tot = cum[end:end + 1, :]
    wt = dtt * jnp.exp(cumt[:, end:end + 1] - cumt)
    lane = lax.broadcasted_iota(jnp.int32, (1, LANES), 1)
    low = lane < SSD_HEAD_DIM
    heads_per_group = SSD_HEADS // SSD_GROUPS
    for g in range(SSD_GROUPS):
        bg = bm[:, g * SSD_STATE:(g + 1) * SSD_STATE]
        cg = cm[:, g * SSD_STATE:(g + 1) * SSD_STATE].astype(BF16)
        scores = lax.dot_general(cg, bg.astype(BF16), (((1,), (1,)), ((), ())), preferred_element_type=F32)
        bgt = bg.T
        st = st_s[g]
        yoff = jnp.dot(cg, st.astype(BF16), preferred_element_type=F32)
        for jp in range(heads_per_group // 2):
            ha = g * heads_per_group + 2 * jp
            lo = (ha // 2) * LANES
            sl = jp * LANES
            xsp = xs[:, lo:lo + LANES]
            xblk = jnp.concatenate([jnp.where(low, xsp, 0.0), jnp.where(low, 0.0, xsp)], axis=0).astype(BF16)
            ms, ecols, lhs2 = [], [], []
            for h in (ha, ha + 1):
                col = cum[:, h:h + 1]
                seg = col - cumt[h:h + 1, :]
                dec = jnp.exp(jnp.where(tri, seg, -jnp.inf))
                ms.append((scores * dec * dtt[h:h + 1, :]).astype(BF16))
                ecols.append(jnp.exp(col))
                lhs2.append((bgt * wt[h:h + 1, :]).astype(BF16))
            ydiag = jnp.dot(jnp.concatenate(ms, axis=1), xblk, preferred_element_type=F32)
            y_pair = ydiag + jnp.where(low, ecols[0], ecols[1]) * yoff[:, sl:sl + LANES]
            upd = jnp.dot(jnp.concatenate(lhs2, axis=1), xblk, preferred_element_type=F32)
            cd = jnp.where(low, jnp.exp(tot[:, ha:ha + 1]), jnp.exp(tot[:, ha + 1:ha + 2]))
            st_s[g, :, sl:sl + LANES] = cd * st[:, sl:sl + LANES] + upd
            if rev:
                y_s[:, lo:lo + LANES] = y_pair
            else:
                o_ref[0, :, lo:lo + LANES] = y_pair
    if rev:
        y = yf_ref[0] + y_s[...] + dsk_ref[...] * xs
        z = z_ref[0]
        o_ref[0] = (_rms(y * (z * jax.nn.sigmoid(z))) * nw_ref[...]).astype(BF16)


def ssd_mix(z, xbc, dtf, dtb, lc, conv_w, conv_b, dt_bias, a_log, d_skip, norm_w):
    bsz, t, wx = xbc.shape
    lch = SSD_CHUNK
    nc, ncc = t // lch, lc // lch
    hb = lch // SUBLANES
    dsk = jnp.repeat(d_skip, SSD_HEAD_DIM).reshape(1, SSD_WIDTH)
    yf = None
    for rev in (False, True):
        cid = functools.partial(_ssd_chunk_id, rev=rev, nc=nc, ncc=ncc)
        d = int(rev)
        dt = dtb if rev else dtf
        row = lambda wd: pl.BlockSpec((1, lch, wd), lambda b, k: (b, cid(k), 0))
        in_specs = [row(wx),
                    pl.BlockSpec((1, SUBLANES, wx), lambda b, k: (b, jnp.maximum(cid(k) * hb - 1, 0), 0)),
                    pl.BlockSpec((1, SUBLANES, wx), lambda b, k: (b, jnp.minimum((cid(k) + 1) * hb, nc * hb - 1), 0)),
                    row(SSD_HEADS),
                    pl.BlockSpec((1, SSD_HEADS, lch), lambda b, k: (b, 0, cid(k))),
                    _const_spec((SSD_CONV, wx)), _const_spec((1, wx)),
                    _const_spec((1, SSD_HEADS)), _const_spec((SSD_HEADS, 1)),
                    _const_spec((1, SSD_HEADS)), _const_spec((SSD_HEADS, 1))]
        args = [xbc, xbc, xbc, dt, jnp.swapaxes(dt, 1, 2), conv_w, conv_b.reshape(1, wx),
                dt_bias[d].reshape(1, SSD_HEADS), dt_bias[d].reshape(SSD_HEADS, 1),
                a_log[d].reshape(1, SSD_HEADS), a_log[d].reshape(SSD_HEADS, 1)]
        scratch = [pltpu.VMEM((SSD_GROUPS, SSD_STATE, SSD_WIDTH // SSD_GROUPS), F32),
                   pltpu.VMEM((lch + 2 * SUBLANES, wx), F32)]
        if rev:
            in_specs += [row(SSD_WIDTH), row(SSD_WIDTH), _const_spec((1, SSD_WIDTH)), _const_spec((1, SSD_WIDTH))]
            args += [z, yf, dsk, norm_w.reshape(1, SSD_WIDTH)]
            scratch += [pltpu.VMEM((lch, SSD_WIDTH), F32)]
        out = pl.pallas_call(
            functools.partial(_ssd_kernel, rev=rev, nc=nc, ncc=ncc),
            grid=(bsz, nc),
            in_specs=in_specs,
            out_specs=row(SSD_WIDTH),
            out_shape=jax.ShapeDtypeStruct((bsz, t, SSD_WIDTH), BF16 if rev else F32),
            scratch_shapes=scratch,
            compiler_params=_cparams("arbitrary", "arbitrary"),
            name="ssd_bwd" if rev else "ssd_fwd",
        )(*args)
        yf = out
    return out


def _outproj_kernel(ys_ref, yd_ref, xc_ref, xl_ref, wt_ref, wb_ref, g1_ref, gate_ref, g2_ref, sc_ref, sh_ref,
                    wr_ref, xlo_ref, xco_ref, h2_ref, lg_ref):
    b, t = pl.program_id(0), pl.program_id(1)
    row = _mod_row(t, b)
    y = (jnp.dot(ys_ref[0], wt_ref[...], preferred_element_type=F32)
         + jnp.dot(yd_ref[0], wb_ref[...], preferred_element_type=F32))
    x = jnp.where(t == 0, xc_ref[0], xl_ref[0])
    xn = x + gate_ref[pl.ds(row, 1), :] * (_rms(y) * g1_ref[...])

    @pl.when(t == 0)
    def _():
        xco_ref[0] = xn

    @pl.when(t > 0)
    def _():
        xlo_ref[0] = xn

    h2 = _rms(xn) * g2_ref[...]
    h2 = h2 * (1.0 + sc_ref[pl.ds(row, 1), :]) + sh_ref[pl.ds(row, 1), :]
    h2_ref[0] = h2.reshape(h2_ref.shape[1:])
    lg_ref[0] = jnp.dot(h2.astype(BF16), wr_ref[...], preferred_element_type=F32)


def out_proj(y_s5, y_ssd, xc, xl, w_out, g1, gate1, g2, sc2, sh2, w_router):
    bsz, lc, d = xc.shape
    n = xl.shape[1]
    nt = (lc + n) // ROW_TILE
    w = w_out.astype(BF16)
    hw = S5_WIDTH
    tile = lambda wd: pl.BlockSpec((1, ROW_TILE, wd), lambda b, t: (b, t, 0))
    pair_out = [pl.BlockSpec((1, ROW_TILE, d), lambda b, t: (b, jnp.maximum(t - 1, 0), 0)),
                pl.BlockSpec((1, ROW_TILE, d), lambda b, t: (b, 0, 0))]
    return pl.pallas_call(
        _outproj_kernel,
        grid=(bsz, nt),
        in_specs=[tile(hw), tile(SSD_WIDTH)] + _pair_specs(lc, d)
        + [_const_spec((hw, d)), _const_spec((SSD_WIDTH, d)), _const_spec((1, d)), _const_spec((8, d)),
           _const_spec((1, d)), _const_spec((8, d)), _const_spec((8, d)), _const_spec((d, N_EXPERTS))],
        out_specs=pair_out + [pl.BlockSpec((1, ROW_TILE, d // LANES, LANES), lambda b, t: (b, t, 0, 0)),
                              tile(N_EXPERTS)],
        out_shape=[jax.ShapeDtypeStruct((bsz, n, d), F32), jax.ShapeDtypeStruct((bsz, lc, d), F32),
                   jax.ShapeDtypeStruct((bsz, lc + n, d // LANES, LANES), F32),
                   jax.ShapeDtypeStruct((bsz, lc + n, N_EXPERTS), F32)],
        compiler_params=_cparams("arbitrary", "arbitrary"),
        name="out_proj",
    )(y_s5, y_ssd, xc, xl, w[:hw], w[hw:], g1.reshape(1, d), gate1, g2.reshape(1, d), sc2, sh2,
      w_router.astype(BF16))


def _row_blocks(rows):
    return [(r0, min(MOE_ROW_BLOCK, rows - r0)) for r0 in range(0, rows, MOE_ROW_BLOCK)]


def _moe_kernel(idx_ref, h2_hbm, gate_ref, wg_ref, wu_ref, wd_ref, o_ref, xf_s, xb_s, h_s, sem, *, rows, nf, nsteps):
    e, s = pl.program_id(0), pl.program_id(1)
    ne = pl.num_programs(0)
    d = xb_s.shape[1]

    def row_copy(src_row, r):
        return pltpu.make_async_copy(h2_hbm.at[pl.ds(src_row, 1)], xf_s.at[pl.ds(r, 1)], sem.at[0])

    def issue(ex, lo, count):
        def body(k, c):
            r = lo + k
            row_copy(idx_ref[ex * rows + r], r).start()
            return c
        lax.fori_loop(0, count, body, 0, unroll=8)

    @pl.when(jnp.logical_and(e == 0, s == 0))
    def _():
        issue(0, 0, rows)

    @pl.when(s == 0)
    def _():
        pltpu.make_async_copy(h2_hbm.at[pl.ds(0, rows)], xf_s, sem.at[0]).wait()
        for r0, rb in _row_blocks(rows):
            xb_s[r0:r0 + rb, :] = xf_s[r0:r0 + rb].reshape(rb, d).astype(BF16)

    assert rows % MOE_GATHER_SLICES == 0 and nsteps > MOE_GATHER_SLICES
    per_step = rows // MOE_GATHER_SLICES

    @pl.when(jnp.logical_and(jnp.logical_and(s >= 1, s <= MOE_GATHER_SLICES), e + 1 < ne))
    def _():
        issue(e + 1, (s - 1) * per_step, per_step)

    @pl.when(s < nf)
    def _():
        def phase_a(wg_s, wu_s):
            wg_s[...] = wg_ref[0, 0].astype(BF16)
            wu_s[...] = wu_ref[0, 0].astype(BF16)
            for r0, rb in _row_blocks(rows):
                x = xb_s[r0:r0 + rb, :]
                g = jnp.dot(x, wg_s[...], preferred_element_type=F32)
                u = jnp.dot(x, wu_s[...], preferred_element_type=F32)
                h_s[s, r0:r0 + rb, :] = ((g * jax.nn.sigmoid(g)) * u).astype(BF16)
        pl.run_scoped(phase_a, pltpu.VMEM(wg_ref.shape[2:], BF16), pltpu.VMEM(wu_ref.shape[2:], BF16))

    @pl.when(s >= nf)
    def _():
        def phase_b(wd_s):
            wd_s[...] = wd_ref[0, 0].astype(BF16)
            for r0, rb in _row_blocks(rows):
                h = jnp.concatenate([h_s[f, r0:r0 + rb, :] for f in range(nf)], axis=1)
                y = jnp.dot(h, wd_s[...], preferred_element_type=F32) * gate_ref[0, r0:r0 + rb, :]
                o_ref[0, r0:r0 + rb] = y.reshape((rb,) + o_ref.shape[2:])
        pl.run_scoped(phase_b, pltpu.VMEM(wd_ref.shape[2:], BF16))


def moe_ffn(idx_rows, h2, gate, layer, w_gate, w_up, w_down):
    e, r, _ = gate.shape
    d, ff = w_gate.shape[2], w_gate.shape[3]
    tf, tn = MOE_F_TILE, MOE_N_TILE
    nf, nn = ff // tf, d // tn
    fcl = lambda s: jnp.minimum(s, nf - 1)
    ncl = lambda s: jnp.maximum(s - nf, 0)
    grid_spec = pltpu.PrefetchScalarGridSpec(
        num_scalar_prefetch=1,
        grid=(e, nf + nn),
        in_specs=[pl.BlockSpec(memory_space=pl.ANY),
                  pl.BlockSpec((1, r, 1), lambda i, s, idx: (i, 0, 0)),
                  pl.BlockSpec((1, 1, d, tf), lambda i, s, idx: (layer, i, 0, fcl(s))),
                  pl.BlockSpec((1, 1, d, tf), lambda i, s, idx: (layer, i, 0, fcl(s))),
                  pl.BlockSpec((1, 1, ff, tn), lambda i, s, idx: (layer, i, 0, ncl(s)))],
        out_specs=pl.BlockSpec((1, r, tn // LANES, LANES), lambda i, s, idx: (i, 0, ncl(s), 0)),
        scratch_shapes=[pltpu.VMEM((r, d // LANES, LANES), F32), pltpu.VMEM((r, d), BF16),
                        pltpu.VMEM((nf, r, tf), BF16), pltpu.SemaphoreType.DMA((1,))],
    )
    return pl.pallas_call(
        functools.partial(_moe_kernel, rows=r, nf=nf, nsteps=nf + nn),
        grid_spec=grid_spec,
        out_shape=jax.ShapeDtypeStruct((e, r, d // LANES, LANES), F32),
        compiler_params=pltpu.CompilerParams(dimension_semantics=("arbitrary", "arbitrary"),
                                             vmem_limit_bytes=MOE_VMEM_LIMIT),
        name="moe_ffn",
    )(idx_rows, h2, gate, w_gate, w_up, w_down)


def _combine_kernel(src_ref, starts_ref, tokl_ref, out_hbm, *refs, lat_tiles, tiles_per_sample, total, with_ctx):
    if with_ctx:
        xc_ref, xl_ref, gate_ref, g3_ref, xlo_ref, xco_ref, buf, acc, sem = refs
    else:
        xl_ref, gate_ref, g3_ref, xlo_ref, buf, acc, sem = refs
    j = pl.program_id(0)
    nt = pl.num_programs(0)
    ch = COMBINE_CHUNK
    first_of = lambda t: starts_ref[t] // ch
    nch_of = lambda t: jnp.where(starts_ref[t + 1] > starts_ref[t],
                                 (starts_ref[t + 1] + ch - 1) // ch - starts_ref[t] // ch, 0)

    def issue(t, c, slot):
        base = (first_of(t) + c) * ch

        def body(i, carry):
            for p in range(2):
                k = 2 * i + p
                pltpu.make_async_copy(out_hbm.at[pl.ds(src_ref[base + k], 1)], buf.at[slot, pl.ds(k, 1)],
                                      sem.at[slot]).start(priority=p)
            return carry
        lax.fori_loop(0, ch // 2, body, 0, unroll=4)

    nch = nch_of(j)

    @pl.when(jnp.logical_and(j == 0, nch > 0))
    def _():
        issue(j, 0, 0)

    acc[...] = jnp.zeros(acc.shape, F32)
    tok = lax.broadcasted_iota(jnp.int32, (ROW_TILE, ch), 0) + j * ROW_TILE

    def chunk(c, carry):
        slot = c % 2

        @pl.when(c + 1 < nch)
        def _():
            issue(j, c + 1, 1 - slot)

        pltpu.make_async_copy(out_hbm.at[pl.ds(0, ch)], buf.at[slot], sem.at[slot]).wait()
        rows = buf[slot].reshape(ch, acc.shape[1])
        onehot = (tok == tokl_ref[pl.ds(first_of(j) + c, 1), :]).astype(BF16)
        hi = rows.astype(BF16)
        lo = (rows - hi.astype(F32)).astype(BF16)
        acc[...] += (jnp.dot(onehot, hi, preferred_element_type=F32)
                     + jnp.dot(onehot, lo, preferred_element_type=F32))
        return carry
    lax.fori_loop(0, nch, chunk, 0)

    nxt = jnp.minimum(j + 1, nt - 1)

    @pl.when(jnp.logical_and(j + 1 < nt, nch_of(nxt) > 0))
    def _():
        issue(nxt, 0, 0)

    y = gate_ref[pl.ds(jnp.where(j >= lat_tiles, 2, j // tiles_per_sample), 1), :] * (_rms(acc[...]) * g3_ref[...])
    if with_ctx:
        @pl.when(j >= lat_tiles)
        def _():
            xco_ref[0] = xc_ref[0] + y

    @pl.when(j < lat_tiles)
    def _():
        xlo_ref[0] = xl_ref[0] + y


def moe_combine(src, starts, tokl, out_rows, xc, xl, gate2, g3, with_ctx):
    bsz, n, d = xl.shape
    lc = xc.shape[1]
    tps = n // ROW_TILE
    lat_tiles = bsz * tps
    nt = lat_tiles + (bsz if with_ctx else 0)
    lat_idx = lambda j: (jnp.minimum(j, lat_tiles - 1) // tps, jnp.minimum(j, lat_tiles - 1) % tps, 0)
    ctx_idx = lambda j: (jnp.clip(j - lat_tiles, 0, bsz - 1), 0, 0)
    lat_spec = pl.BlockSpec((1, ROW_TILE, d), lambda j, s0, s1: lat_idx(j))
    ctx_spec = pl.BlockSpec((1, lc, d), lambda j, s0, s1: ctx_idx(j))
    grid_spec = pltpu.PrefetchScalarGridSpec(
        num_scalar_prefetch=2,
        grid=(nt,),
        in_specs=[pl.BlockSpec(tokl.shape, lambda j, s0, s1: (0, 0)),
                  pl.BlockSpec(memory_space=pl.ANY)]
        + ([ctx_spec] if with_ctx else []) + [lat_spec,
                                              pl.BlockSpec((8, d), lambda j, s0, s1: (0, 0)),
                                              pl.BlockSpec((1, d), lambda j, s0, s1: (0, 0))],
        out_specs=[lat_spec] + ([ctx_spec] if with_ctx else []),
        scratch_shapes=[pltpu.VMEM((2, COMBINE_CHUNK, d // LANES, LANES), F32), pltpu.VMEM((ROW_TILE, d), F32),
                        pltpu.SemaphoreType.DMA((2,))],
    )
    res = pl.pallas_call(
        functools.partial(_combine_kernel, lat_tiles=lat_tiles, tiles_per_sample=tps, total=src.shape[0],
                          with_ctx=with_ctx),
        grid_spec=grid_spec,
        out_shape=[jax.ShapeDtypeStruct(xl.shape, F32)] + ([jax.ShapeDtypeStruct(xc.shape, F32)] if with_ctx else []),
        compiler_params=_cparams("arbitrary"),
        name="moe_combine",
    )(src, starts, tokl, out_rows, *([xc] if with_ctx else []), xl, gate2, g3.reshape(1, d))
    return (res[0], res[1]) if with_ctx else (res[0], xc)


def rmsnorm(x, w):
    xf = x.astype(F32)
    y = xf * lax.rsqrt(jnp.mean(xf * xf, axis=-1, keepdims=True) + EPS)
    return (y * w.astype(F32)).astype(x.dtype)


def grid_transpose(x):
    b, n = x.shape[:2]
    rows = n // GRID_W
    return x.reshape((b, rows, GRID_W) + x.shape[2:]).swapaxes(1, 2).reshape(x.shape)


def route(logits, lc, with_ctx):
    bsz, t, e = logits.shape
    n = t - lc
    idx_l, gate_l, tok_l = [], [], []
    for lo, m, off in [(lc, n, 0)] + ([(0, lc, bsz * n)] if with_ctx else []):
        cap = CAPACITY_FACTOR * m // e
        aff = jax.nn.softmax(logits[:, lo:lo + m], axis=-1)
        gate, idx = lax.top_k(jnp.swapaxes(aff, 1, 2), cap)
        b_off = jnp.arange(bsz, dtype=jnp.int32)[:, None, None]
        idx_l.append(jnp.swapaxes(b_off * t + lo + idx, 0, 1).reshape(e, bsz * cap))
        tok_l.append(jnp.swapaxes(off + b_off * m + idx, 0, 1).reshape(e, bsz * cap))
        gate_l.append(jnp.swapaxes(gate, 0, 1).reshape(e, bsz * cap))
    idx_rows = jnp.concatenate(idx_l, axis=1).reshape(-1).astype(jnp.int32)
    gate = jnp.concatenate(gate_l, axis=1)[..., None]
    tok = jnp.concatenate(tok_l, axis=1).reshape(-1).astype(jnp.int32)
    total = tok.shape[0]
    assert total % COMBINE_CHUNK == 0
    tok_sorted, src = lax.sort((tok, jnp.arange(total, dtype=jnp.int32)), num_keys=1)
    tiles = (bsz * n + (bsz * lc if with_ctx else 0)) // ROW_TILE
    bounds = jnp.arange(tiles + 1, dtype=jnp.int32) * ROW_TILE
    starts = jnp.sum((tok[None, :] < bounds[:, None]).astype(jnp.int32), axis=1)
    return idx_rows, gate, src, starts, tok_sorted.reshape(total // COMBINE_CHUNK, COMBINE_CHUNK)


def kernel(x, c, ctx, c_ctx, ada_w, ada_b, norm_g, w_in, w_out, s5_lam_re, s5_lam_im, s5_log_step,
           s5_b_re, s5_b_im, s5_c_re, s5_c_im, s5_d, s5_w_glu, s5_b_glu, ssd_conv_w, ssd_conv_b,
           ssd_dt_bias, ssd_a_log, ssd_d, ssd_norm, moe_router, moe_w_gate, moe_w_up, moe_w_down):
    bsz, n, d = x.shape
    lc = ctx.shape[1]
    nseg = SUBLANES // bsz
    cs = jnp.zeros((8, d), F32).at[:bsz].set(c).at[2].set(c_ctx)
    mods = adaln_all(cs, ada_w, ada_b)
    s5_ops = jax.vmap(functools.partial(s5_compact_operators, seg_ctx=lc // nseg, seg_lat=n // nseg))(
        s5_lam_re, s5_lam_im, s5_log_step, s5_b_re, s5_b_im, s5_c_re, s5_c_im, s5_d)
    xl, xc = x, ctx
    for i in range(DEPTH):
        col_major = i % 2 == 1
        last = i == DEPTH - 1
        sh1, sc1, g1, sh2, sc2, g2 = [mods[i, :, k * d:(k + 1) * d] for k in range(N_MOD)]
        if col_major:
            xl = grid_transpose(xl)
        u, z, xbc, dtf, dtb = in_proj(xc, xl, norm_g[i, 0], sc1, sh1, w_in[i])
        y_s5 = s5_glu(s5_mix_packed(u, bsz, lc, n, i, s5_ops), bsz, lc, n, s5_w_glu[i], s5_b_glu[i])
        y_ssd = ssd_mix(z, xbc, dtf, dtb, lc, ssd_conv_w[i], ssd_conv_b[i], ssd_dt_bias[i],
                        ssd_a_log[i], ssd_d[i], ssd_norm[i])
        xl, xc, h2, logits = out_proj(y_s5, y_ssd, xc, xl, w_out[i], norm_g[i, 1], g1, norm_g[i, 2],
                                      sc2, sh2, moe_router[i])
        idx_rows, gate, src, starts, tokl = route(logits, lc, not last)
        out = moe_ffn(idx_rows, h2.reshape((bsz * (lc + n),) + h2.shape[2:]), gate, i,
                      moe_w_gate, moe_w_up, moe_w_down)
        xl, xc = moe_combine(src, starts, tokl, out.reshape((-1,) + out.shape[2:]), xc, xl, g2, norm_g[i, 3],
                             not last)
        if col_major:
            xl = grid_transpose(xl)
    return xl
```

```python
import functools

import jax
import jax.numpy as jnp
from jax import lax
from jax.experimental import pallas as pl
from jax.experimental.pallas import tpu as pltpu

D_MODEL = 2048
DEPTH = 4
GRID_W = 64
EPS = 1e-6
N_MOD = 6

S5_WIDTH = 1024
S5_GROUP = 16
S5_GROUPS = S5_WIDTH // S5_GROUP
S5_STATE = 64

SSD_WIDTH = 1024
SSD_HEAD_DIM = 64
SSD_HEADS = SSD_WIDTH // SSD_HEAD_DIM
SSD_GROUPS = 2
SSD_STATE = 128
SSD_CONV = 5
SSD_XBC = SSD_WIDTH + 2 * SSD_GROUPS * SSD_STATE

N_EXPERTS = 16
CAPACITY_FACTOR = 2
D_FF = 1536

F32 = jnp.float32
BF16 = jnp.bfloat16
HIGHEST = lax.Precision.HIGHEST

LANES = 128
SUBLANES = 8
ROW_TILE = 256
MOE_F_TILE = 256
MOE_ROW_BLOCK = 544
MOE_GATHER_SLICES = 4
MOE_N_TILE = 1024
MOE_VMEM_LIMIT = 60 * 1024 * 1024
COMBINE_CHUNK = 256
S5_BLOCK = 16
S5_PAIRS = S5_GROUPS // 2
SSD_CHUNK = 128
VMEM_LIMIT = 56 * 1024 * 1024


def _cparams(*sem):
    return pltpu.CompilerParams(dimension_semantics=sem, vmem_limit_bytes=VMEM_LIMIT)


def _adaln_kernel(c_ref, w_ref, b_ref, o_ref):
    c = c_ref[...]
    a = (c * jax.nn.sigmoid(c)).astype(BF16)
    o_ref[0] = jnp.dot(a, w_ref[0].astype(BF16), preferred_element_type=F32) + b_ref[0]


def adaln_all(cs, ada_w, ada_b):
    depth, d, n = ada_w.shape
    tn = 1024
    return pl.pallas_call(
        _adaln_kernel,
        grid=(depth, n // tn),
        in_specs=[pl.BlockSpec((8, d), lambda l, j: (0, 0)),
                  pl.BlockSpec((1, d, tn), lambda l, j: (l, 0, j)),
                  pl.BlockSpec((1, 1, tn), lambda l, j: (l, 0, j))],
        out_specs=pl.BlockSpec((1, 8, tn), lambda l, j: (l, 0, j)),
        out_shape=jax.ShapeDtypeStruct((depth, 8, n), F32),
        compiler_params=_cparams("arbitrary", "arbitrary"),
        name="adaln",
    )(cs, ada_w, ada_b.reshape(depth, 1, n))


def _mod_row(t, b):
    return jnp.where(t == 0, 2, b)


def _rms(x):
    return x * lax.rsqrt(jnp.mean(x * x, axis=-1, keepdims=True) + EPS)


def _s5_block_rows(b, t, bsz, lat_tiles):
    return jnp.where(t == 0, bsz * lat_tiles + b, b * lat_tiles + t - 1)


def _inproj_kernel(xc_ref, xl_ref, g_ref, sc_ref, sh_ref, wu_ref, wz_ref, wx_ref, wf_ref, wb_ref,
                   u_ref, z_ref, xbc_ref, dtf_ref, dtb_ref, r_s):
    b, t = pl.program_id(0), pl.program_id(1)
    x = jnp.where(t == 0, xc_ref[0], xl_ref[0])
    row = _mod_row(t, b)
    h = _rms(x) * g_ref[...]
    h = (h * (1.0 + sc_ref[pl.ds(row, 1), :]) + sh_ref[pl.ds(row, 1), :]).astype(BF16)
    u = jnp.dot(h, wu_ref[...], preferred_element_type=F32)
    nk = S5_WIDTH // LANES
    blocks = ROW_TILE // S5_BLOCK
    pw = 2 * S5_GROUP
    for k in range(nk):
        r_s[k] = u[:, k * LANES:(k + 1) * LANES]
    for k in range(nk):
        xts = [r_s[k, pl.ds(tt, blocks, stride=S5_BLOCK), :] for tt in range(S5_BLOCK)]
        for j in range(LANES // pw):
            u_ref[k * (LANES // pw) + j] = jnp.concatenate([xt[:, j * pw:(j + 1) * pw] for xt in xts], axis=1)
    z_ref[0] = jnp.dot(h, wz_ref[...], preferred_element_type=F32)
    xbc_ref[0] = jnp.dot(h, wx_ref[...], preferred_element_type=F32)
    dtf_ref[0] = jnp.dot(h, wf_ref[...], preferred_element_type=F32)
    dtb_ref[0] = jnp.dot(h, wb_ref[...], preferred_element_type=F32)


def _pair_specs(lc, d):
    assert lc == ROW_TILE
    return [pl.BlockSpec((1, ROW_TILE, d), lambda b, t: (b, 0, 0)),
            pl.BlockSpec((1, ROW_TILE, d), lambda b, t: (b, jnp.maximum(t - 1, 0), 0))]


def _const_spec(shape):
    return pl.BlockSpec(shape, lambda *_: (0,) * len(shape), pipeline_mode=pl.Buffered(1))


def in_proj(xc, xl, g, sc, sh, w_in):
    bsz, lc, d = xc.shape
    n = xl.shape[1]
    nt = (lc + n) // ROW_TILE
    s1, s2, s3 = S5_WIDTH, S5_WIDTH + SSD_WIDTH, S5_WIDTH + SSD_WIDTH + SSD_XBC
    w = w_in.astype(BF16)
    ws = [w[:, :s1], w[:, s1:s2], w[:, s2:s3], w[:, s3:s3 + SSD_HEADS], w[:, s3 + SSD_HEADS:]]
    widths = [x.shape[1] for x in ws]
    blocks = ROW_TILE // S5_BLOCK
    s5_rows = bsz * (lc + n) // S5_BLOCK
    s5_lanes = 2 * S5_GROUP * S5_BLOCK
    u_spec = pl.BlockSpec((S5_PAIRS, blocks, s5_lanes),
                          lambda b, t: (0, _s5_block_rows(b, t, bsz, n // ROW_TILE), 0))
    return pl.pallas_call(
        _inproj_kernel,
        grid=(bsz, nt),
        in_specs=_pair_specs(lc, d) + [_const_spec((1, d)), _const_spec((8, d)), _const_spec((8, d))]
        + [_const_spec((d, wd)) for wd in widths],
        out_specs=[u_spec] + [pl.BlockSpec((1, ROW_TILE, wd), lambda b, t: (b, t, 0)) for wd in widths[1:]],
        out_shape=[jax.ShapeDtypeStruct((S5_PAIRS, s5_rows, s5_lanes), F32)]
        + [jax.ShapeDtypeStruct((bsz, lc + n, wd), F32) for wd in widths[1:]],
        scratch_shapes=[pltpu.VMEM((S5_WIDTH // LANES, ROW_TILE, LANES), F32)],
        compiler_params=_cparams("arbitrary", "arbitrary"),
        name="in_proj",
    )(xc, xl, g.reshape(1, d), sc, sh, *ws)


def s5_compact_operators(lam_re, lam_im, log_step, b_re, b_im, c_re, c_im, d_skip, seg_ctx, seg_lat):
    g_, p_, h_, lk = S5_GROUPS, S5_STATE, S5_GROUP, S5_BLOCK
    j = jnp.arange(lk, dtype=F32)[:, None, None]
    eye2 = jnp.eye(2, dtype=F32)
    k_dir, bb_dir, lam_rows = [], [], []
    for d in range(2):
        step = jnp.exp(log_step[d])[:, None]
        e_re, ang = lam_re[d] * step, lam_im[d] * step
        pr = jnp.exp(j * e_re) * jnp.cos(j * ang)
        pi = jnp.exp(j * e_re) * jnp.sin(j * ang)
        den = lam_re[d] * lam_re[d] + lam_im[d] * lam_im[d]
        nr = pr[1] - 1.0
        f_re = (nr * lam_re[d] + pi[1] * lam_im[d]) / den
        f_im = (pi[1] * lam_re[d] - nr * lam_im[d]) / den
        bb_re = f_re[..., None] * b_re - f_im[..., None] * b_im
        bb_im = f_re[..., None] * b_im + f_im[..., None] * b_re
        w_re = pr[:, :, :, None] * bb_re - pi[:, :, :, None] * bb_im
        w_im = pr[:, :, :, None] * bb_im + pi[:, :, :, None] * bb_re
        k_dir.append(jnp.einsum('gop,jgph->jgho', c_re, w_re, precision=HIGHEST)
                     - jnp.einsum('gop,jgph->jgho', c_im, w_im, precision=HIGHEST))
        bb_dir.append(jnp.stack([bb_re, bb_im], axis=0).transpose(1, 3, 0, 2))
        for n_pow in (lk, seg_ctx, seg_lat, 1):
            lam_rows += [jnp.exp(n_pow * e_re) * jnp.cos(n_pow * ang), jnp.exp(n_pow * e_re) * jnp.sin(n_pow * ang)]
    mid = k_dir[0][0] + k_dir[1][0] + jnp.eye(h_, dtype=F32)[None] * d_skip.reshape(g_, h_, 1)
    taps = jnp.concatenate([k_dir[1][1:][::-1], mid[None], k_dir[0][1:]], axis=0)
    taps = taps.transpose(1, 2, 0, 3).reshape(S5_PAIRS, 2, h_, 2 * lk - 1, h_)
    kfull = jnp.einsum('aihmo,ij->aihmjo', taps, eye2).reshape(S5_PAIRS, 2 * h_, (2 * lk - 1) * 2 * h_)
    kfull = jnp.pad(kfull, ((0, 0), (0, 0), (0, 2 * lk * 2 * h_ - kfull.shape[2])))
    bbp = jnp.stack(bb_dir, axis=1).reshape(S5_PAIRS, 2, 2, h_, 2, p_)
    bbp = jnp.einsum('aidhqp,ij->adihqjp', bbp, eye2).reshape(S5_PAIRS, 2, 2 * h_, 4 * p_)
    cpt = jnp.stack([c_re, -c_im], axis=2).reshape(S5_PAIRS, 2, h_, 2, p_)
    cpt = jnp.einsum('ajoqp,ij->ajoqip', cpt, eye2).reshape(S5_PAIRS, 2 * h_, 4 * p_)
    order16 = [0, 1, 8, 9, 2, 3, 10, 11, 4, 5, 12, 13, 6, 7, 14, 15]
    lam = jnp.stack([lam_rows[r].reshape(S5_PAIRS, 2 * p_) for r in order16], axis=1)
    return kfull, bbp, cpt, lam


def _s5_fused_kernel(z_ref, kf_ref, bbp_ref, cpt_ref, lam_ref, y_ref, tz_s, bst_s, cft_s, x_s, h_s, t_s, g_s,
                     *, n_lat, n_ctx, bsz):
    nseg = SUBLANES // bsz
    pw = 2 * S5_GROUP
    lam = lam_ref[0, 0]
    kf = kf_ref[0, 0]
    for tt in range(S5_BLOCK):
        off = (S5_BLOCK - 1 - tt) * pw
        tz_s[tt * pw:(tt + 1) * pw, :] = kf[:, off:off + S5_BLOCK * pw].astype(BF16)
    for d in range(2):
        co = d * 2 * LANES
        lr = jnp.broadcast_to(lam[12 + 2 * d:13 + 2 * d], (pw, LANES))
        li = jnp.broadcast_to(lam[13 + 2 * d:14 + 2 * d], (pw, LANES))
        order = range(S5_BLOCK - 1, -1, -1) if d == 0 else range(S5_BLOCK)
        wr, wi = bbp_ref[0, 0, d, :, :LANES], bbp_ref[0, 0, d, :, LANES:]
        for tt in order:
            bst_s[tt * pw:(tt + 1) * pw, co:co + LANES] = wr.astype(BF16)
            bst_s[tt * pw:(tt + 1) * pw, co + LANES:co + 2 * LANES] = wi.astype(BF16)
            wr, wi = lr * wr - li * wi, lr * wi + li * wr
        xr, xi = cpt_ref[0, 0, :, :LANES], cpt_ref[0, 0, :, LANES:]
        for tt in (range(S5_BLOCK) if d == 0 else range(S5_BLOCK - 1, -1, -1)):
            xr, xi = lr * xr + li * xi, lr * xi - li * xr
            cft_s[tt * pw:(tt + 1) * pw, co:co + LANES] = xr.astype(BF16)
            cft_s[tt * pw:(tt + 1) * pw, co + LANES:co + 2 * LANES] = xi.astype(BF16)

    u = z_ref[0].astype(BF16)
    x = jnp.dot(u, bst_s[...], preferred_element_type=F32)
    for k in range(4):
        x_s[k] = x[:, k * LANES:(k + 1) * LANES]

    def scan(base, nsteps, stride, rev, cr, lr, li, init, store):
        def body(k, carry):
            hr, hi = carry
            q = nsteps - 1 - k if rev else k
            rows = pl.ds(base + q, SUBLANES, stride=stride)
            if store:
                h_s[cr, rows, :] = hr
                h_s[cr + 1, rows, :] = hi
            return lr * hr - li * hi + x_s[cr, rows, :], lr * hi + li * hr + x_s[cr + 1, rows, :]
        return lax.fori_loop(0, nsteps, body, init, unroll=4)

    def seg_carry(tot, lr, li, inits, rev):
        t_s[:, :LANES], t_s[:, LANES:] = tot
        finals = []
        for b in range(bsz):
            gr, gi = inits[b]
            for seg in (range(nseg - 1, -1, -1) if rev else range(nseg)):
                s = b * nseg + seg
                g_s[s:s + 1, :LANES], g_s[s:s + 1, LANES:] = gr, gi
                tr, ti = t_s[s:s + 1, :LANES], t_s[s:s + 1, LANES:]
                gr, gi = lr * gr - li * gi + tr, lr * gi + li * gr + ti
            finals.append((gr, gi))
        return (g_s[:, :LANES], g_s[:, LANES:]), finals

    zero8 = (jnp.zeros((SUBLANES, LANES), F32), jnp.zeros((SUBLANES, LANES), F32))
    zero1 = (jnp.zeros((1, LANES), F32), jnp.zeros((1, LANES), F32))
    for d in range(2):
        rev = d == 1
        lr = jnp.broadcast_to(lam[2 * d:2 * d + 1], (SUBLANES, LANES))
        li = jnp.broadcast_to(lam[2 * d + 1:2 * d + 2], (SUBLANES, LANES))
        init1 = [zero1] * bsz
        for base, nsteps, row in ((n_lat * SUBLANES, n_ctx, 4 + 2 * d), (0, n_lat, 8 + 2 * d)):
            tot = scan(base, nsteps, nsteps, rev, 2 * d, lr, li, zero8, False)
            g0, init1 = seg_carry(tot, lam[row:row + 1], lam[row + 1:row + 2], init1, rev)
            scan(base, nsteps, nsteps, rev, 2 * d, lr, li, g0, True)
    hcat = jnp.concatenate([h_s[k] for k in range(4)], axis=1).astype(BF16)
    y = lax.dot_general(hcat, cft_s[...], (((1,), (1,)), ((), ())), preferred_element_type=F32)
    y = y + jnp.dot(u, tz_s[...], preferred_element_type=F32)
    y_ref[0] = jax.nn.gelu(y)


def s5_mix_packed(z, bsz, lc, n, layer, ops):
    assert SUBLANES % bsz == 0
    nseg = SUBLANES // bsz
    kfull, bbp, cpt, lam = ops
    n_lat, n_ctx = n // (nseg * S5_BLOCK), lc // (nseg * S5_BLOCK)
    rows, width = z.shape[1], z.shape[2]
    assert rows == (n_lat + n_ctx) * SUBLANES
    blk = lambda *shape: pl.BlockSpec((1,) + shape, lambda i: (i,) + (0,) * len(shape))
    lblk = lambda a: pl.BlockSpec((1, 1) + a.shape[2:], lambda i: (layer, i) + (0,) * (a.ndim - 2))
    return pl.pallas_call(
        functools.partial(_s5_fused_kernel, n_lat=n_lat, n_ctx=n_ctx, bsz=bsz),
        grid=(S5_PAIRS,),
        in_specs=[blk(rows, width), lblk(kfull), lblk(bbp), lblk(cpt), lblk(lam)],
        out_specs=blk(rows, width),
        out_shape=jax.ShapeDtypeStruct((S5_PAIRS, rows, width), F32),
        scratch_shapes=[pltpu.VMEM((width, width), BF16), pltpu.VMEM((width, 4 * LANES), BF16),
                        pltpu.VMEM((width, 4 * LANES), BF16),
                        pltpu.VMEM((4, rows, LANES), F32), pltpu.VMEM((4, rows, LANES), F32),
                        pltpu.VMEM((SUBLANES, 2 * LANES), F32), pltpu.VMEM((SUBLANES, 2 * LANES), F32)],
        compiler_params=_cparams("arbitrary"),
        name="s5_mix",
    )(z, kfull, bbp, cpt, lam)


def _glu_kernel(y_ref, wa_ref, wb_ref, ba_ref, bb_ref, o_ref, a_s):
    pw = 2 * S5_GROUP
    per = LANES // pw
    blocks = ROW_TILE // S5_BLOCK
    for tt in range(S5_BLOCK):
        for k in range(S5_WIDTH // LANES):
            a_s[k, pl.ds(tt, blocks, stride=S5_BLOCK), :] = jnp.concatenate(
                [y_ref[k * per + j, :, tt * pw:(tt + 1) * pw] for j in range(per)], axis=1)
    y = jnp.concatenate([a_s[k] for k in range(S5_WIDTH // LANES)], axis=1).astype(BF16)
    a = jnp.dot(y, wa_ref[...], preferred_element_type=F32) + ba_ref[...]
    g = jnp.dot(y, wb_ref[...], preferred_element_type=F32) + bb_ref[...]
    o_ref[0] = (a * jax.nn.sigmoid(g)).astype(BF16)


def s5_glu(y, bsz, lc, n, w_glu, b_glu):
    w = S5_WIDTH
    wb = w_glu.astype(BF16)
    blocks = ROW_TILE // S5_BLOCK
    y_spec = pl.BlockSpec((S5_PAIRS, blocks, y.shape[2]),
                          lambda b, t: (0, _s5_block_rows(b, t, bsz, n // ROW_TILE), 0))
    return pl.pallas_call(
        _glu_kernel,
        grid=(bsz, (lc + n) // ROW_TILE),
        in_specs=[y_spec, _const_spec((w, w)), _const_spec((w, w)), _const_spec((1, w)), _const_spec((1, w))],
        out_specs=pl.BlockSpec((1, ROW_TILE, w), lambda b, i: (b, i, 0)),
        out_shape=jax.ShapeDtypeStruct((bsz, lc + n, w), BF16),
        scratch_shapes=[pltpu.VMEM((w // LANES, ROW_TILE, LANES), F32)],
        compiler_params=_cparams("arbitrary", "arbitrary"),
        name="s5_glu",
    )(y, wb[:, :w], wb[:, w:], b_glu[:w].reshape(1, w), b_glu[w:].reshape(1, w))


def _softplus(x):
    return jnp.maximum(x, 0.0) + jnp.log(1.0 + jnp.exp(-jnp.abs(x)))


def _ssd_chunk_id(k, rev, nc, ncc):
    if not rev:
        return k
    return jnp.where(k < ncc, ncc - 1 - k, nc - 1 - (k - ncc))


def _ssd_kernel(*refs, rev, nc, ncc):
    if rev:
        (x_ref, xp_ref, xn_ref, dt_ref, dtt_ref, cw_ref, cb_ref, bias_ref, biast_ref, alog_ref, alogt_ref,
         z_ref, yf_ref, dsk_ref, nw_ref, o_ref, st_s, xp_s, y_s) = refs
    else:
        (x_ref, xp_ref, xn_ref, dt_ref, dtt_ref, cw_ref, cb_ref, bias_ref, biast_ref, alog_ref, alogt_ref,
         o_ref, st_s, xp_s) = refs
    lch = SSD_CHUNK
    k = pl.program_id(1)
    c = _ssd_chunk_id(k, rev, nc, ncc)

    @pl.when(k == 0)
    def _():
        st_s[...] = jnp.zeros(st_s.shape, F32)

    first = jnp.logical_or(c == 0, c == ncc)
    last = jnp.logical_or(c == ncc - 1, c == nc - 1)
    xp_s[0:SUBLANES, :] = jnp.where(first, 0.0, xp_ref[0])
    xp_s[SUBLANES:SUBLANES + lch, :] = x_ref[0]
    xp_s[SUBLANES + lch:2 * SUBLANES + lch, :] = jnp.where(last, 0.0, xn_ref[0])
    acc = cb_ref[...] + cw_ref[0:1, :] * xp_s[SUBLANES - 2:SUBLANES - 2 + lch, :]
    for tap in range(1, SSD_CONV):
        acc = acc + cw_ref[tap:tap + 1, :] * xp_s[SUBLANES - 2 + tap:SUBLANES - 2 + tap + lch, :]
    xc = acc * jax.nn.sigmoid(acc)
    gn = SSD_GROUPS * SSD_STATE
    xs = xc[:, :SSD_WIDTH]
    bm = xc[:, SSD_WIDTH:SSD_WIDTH + gn]
    cm = xc[:, SSD_WIDTH + gn:]

    dt = _softplus(dt_ref[0] + bias_ref[...])
    dtt = _softplus(dtt_ref[0] + biast_ref[...])
    da = dt * -jnp.exp(alog_ref[...])
    dat = dtt * -jnp.exp(alogt_ref[...])
    row_i = lax.broadcasted_iota(jnp.int32, (lch, lch), 0)
    col_i = lax.broadcasted_iota(jnp.int32, (lch, lch), 1)
    tri = (col_i >= row_i) if rev else (col_i <= row_i)
    trit = (row_i >= col_i) if rev else (row_i <= col_i)
    cum = jnp.dot(tri.astype(F32), da, preferred_element_type=F32, precision=HIGHEST)
    cumt = jnp.dot(dat, trit.astype(F32), preferred_element_type=F32, precision=HIGHEST)
    end = 0 if rev else lch - 1
    tot = cum[end:end + 1, :]
    wt = dtt * jnp.exp(cumt[:, end:end + 1] - cumt)
    lane = lax.broadcasted_iota(jnp.int32, (1, LANES), 1)
    low = lane < SSD_HEAD_DIM
    heads_per_group = SSD_HEADS // SSD_GROUPS
    for g in range(SSD_GROUPS):
        bg = bm[:, g * SSD_STATE:(g + 1) * SSD_STATE]
        cg = cm[:, g * SSD_STATE:(g + 1) * SSD_STATE].astype(BF16)
        scores = lax.dot_general(cg, bg.astype(BF16), (((1,), (1,)), ((), ())), preferred_element_type=F32)
        bgt = bg.T
        st = st_s[g]
        yoff = jnp.dot(cg, st.astype(BF16), preferred_element_type=F32)
        for jp in range(heads_per_group // 2):
            ha = g * heads_per_group + 2 * jp
            lo = (ha // 2) * LANES
            sl = jp * LANES
            xsp = xs[:, lo:lo + LANES]
            xblk = jnp.concatenate([jnp.where(low, xsp, 0.0), jnp.where(low, 0.0, xsp)], axis=0).astype(BF16)
            ms, ecols, lhs2 = [], [], []
            for h in (ha, ha + 1):
                col = cum[:, h:h + 1]
                seg = col - cumt[h:h + 1, :]
                dec = jnp.exp(jnp.where(tri, seg, -jnp.inf))
                ms.append((scores * dec * dtt[h:h + 1, :]).astype(BF16))
                ecols.append(jnp.exp(col))
                lhs2.append((bgt * wt[h:h + 1, :]).astype(BF16))
            ydiag = jnp.dot(jnp.concatenate(ms, axis=1), xblk, preferred_element_type=F32)
            y_pair = ydiag + jnp.where(low, ecols[0], ecols[1]) * yoff[:, sl:sl + LANES]
            upd = jnp.dot(jnp.concatenate(lhs2, axis=1), xblk, preferred_element_type=F32)
            cd = jnp.where(low, jnp.exp(tot[:, ha:ha + 1]), jnp.exp(tot[:, ha + 1:ha + 2]))
            st_s[g, :, sl:sl + LANES] = cd * st[:, sl:sl + LANES] + upd
            if rev:
                y_s[:, lo:lo + LANES] = y_pair
            else:
                o_ref[0, :, lo:lo + LANES] = y_pair
    if rev:
        y = yf_ref[0] + y_s[...] + dsk_ref[...] * xs
        z = z_ref[0]
        o_ref[0] = (_rms(y * (z * jax.nn.sigmoid(z))) * nw_ref[...]).astype(BF16)


def ssd_mix(z, xbc, dtf, dtb, lc, conv_w, conv_b, dt_bias, a_log, d_skip, norm_w):
    bsz, t, wx = xbc.shape
    lch = SSD_CHUNK
    nc, ncc = t // lch, lc // lch
    hb = lch // SUBLANES
    dsk = jnp.repeat(d_skip, SSD_HEAD_DIM).reshape(1, SSD_WIDTH)
    yf = None
    for rev in (False, True):
        cid = functools.partial(_ssd_chunk_id, rev=rev, nc=nc, ncc=ncc)
        d = int(rev)
        dt = dtb if rev else dtf
        row = lambda wd: pl.BlockSpec((1, lch, wd), lambda b, k: (b, cid(k), 0))
        in_specs = [row(wx),
                    pl.BlockSpec((1, SUBLANES, wx), lambda b, k: (b, jnp.maximum(cid(k) * hb - 1, 0), 0)),
                    pl.BlockSpec((1, SUBLANES, wx), lambda b, k: (b, jnp.minimum((cid(k) + 1) * hb, nc * hb - 1), 0)),
                    row(SSD_HEADS),
                    pl.BlockSpec((1, SSD_HEADS, lch), lambda b, k: (b, 0, cid(k))),
                    _const_spec((SSD_CONV, wx)), _const_spec((1, wx)),
                    _const_spec((1, SSD_HEADS)), _const_spec((SSD_HEADS, 1)),
                    _const_spec((1, SSD_HEADS)), _const_spec((SSD_HEADS, 1))]
        args = [xbc, xbc, xbc, dt, jnp.swapaxes(dt, 1, 2), conv_w, conv_b.reshape(1, wx),
                dt_bias[d].reshape(1, SSD_HEADS), dt_bias[d].reshape(SSD_HEADS, 1),
                a_log[d].reshape(1, SSD_HEADS), a_log[d].reshape(SSD_HEADS, 1)]
        scratch = [pltpu.VMEM((SSD_GROUPS, SSD_STATE, SSD_WIDTH // SSD_GROUPS), F32),
                   pltpu.VMEM((lch + 2 * SUBLANES, wx), F32)]
        if rev:
            in_specs += [row(SSD_WIDTH), row(SSD_WIDTH), _const_spec((1, SSD_WIDTH)), _const_spec((1, SSD_WIDTH))]
            args += [z, yf, dsk, norm_w.reshape(1, SSD_WIDTH)]
            scratch += [pltpu.VMEM((lch, SSD_WIDTH), F32)]
        out = pl.pallas_call(
            functools.partial(_ssd_kernel, rev=rev, nc=nc, ncc=ncc),
            grid=(bsz, nc),
            in_specs=in_specs,
            out_specs=row(SSD_WIDTH),
            out_shape=jax.ShapeDtypeStruct((bsz, t, SSD_WIDTH), BF16 if rev else F32),
            scratch_shapes=scratch,
            compiler_params=_cparams("arbitrary", "arbitrary"),
            name="ssd_bwd" if rev else "ssd_fwd",
        )(*args)
        yf = out
    return out


def _outproj_kernel(ys_ref, yd_ref, xc_ref, xl_ref, wt_ref, wb_ref, g1_ref, gate_ref, g2_ref, sc_ref, sh_ref,
                    wr_ref, xlo_ref, xco_ref, h2_ref, lg_ref):
    b, t = pl.program_id(0), pl.program_id(1)
    row = _mod_row(t, b)
    y = (jnp.dot(ys_ref[0], wt_ref[...], preferred_element_type=F32)
         + jnp.dot(yd_ref[0], wb_ref[...], preferred_element_type=F32))
    x = jnp.where(t == 0, xc_ref[0], xl_ref[0])
    xn = x + gate_ref[pl.ds(row, 1), :] * (_rms(y) * g1_ref[...])

    @pl.when(t == 0)
    def _():
        xco_ref[0] = xn

    @pl.when(t > 0)
    def _():
        xlo_ref[0] = xn

    h2 = _rms(xn) * g2_ref[...]
    h2 = h2 * (1.0 + sc_ref[pl.ds(row, 1), :]) + sh_ref[pl.ds(row, 1), :]
    h2_ref[0] = h2.reshape(h2_ref.shape[1:])
    lg_ref[0] = jnp.dot(h2.astype(BF16), wr_ref[...], preferred_element_type=F32)


def out_proj(y_s5, y_ssd, xc, xl, w_out, g1, gate1, g2, sc2, sh2, w_router):
    bsz, lc, d = xc.shape
    n = xl.shape[1]
    nt = (lc + n) // ROW_TILE
    w = w_out.astype(BF16)
    hw = S5_WIDTH
    tile = lambda wd: pl.BlockSpec((1, ROW_TILE, wd), lambda b, t: (b, t, 0))
    pair_out = [pl.BlockSpec((1, ROW_TILE, d), lambda b, t: (b, jnp.maximum(t - 1, 0), 0)),
                pl.BlockSpec((1, ROW_TILE, d), lambda b, t: (b, 0, 0))]
    return pl.pallas_call(
        _outproj_kernel,
        grid=(bsz, nt),
        in_specs=[tile(hw), tile(SSD_WIDTH)] + _pair_specs(lc, d)
        + [_const_spec((hw, d)), _const_spec((SSD_WIDTH, d)), _const_spec((1, d)), _const_spec((8, d)),
           _const_spec((1, d)), _const_spec((8, d)), _const_spec((8, d)), _const_spec((d, N_EXPERTS))],
        out_specs=pair_out + [pl.BlockSpec((1, ROW_TILE, d // LANES, LANES), lambda b, t: (b, t, 0, 0)),
                              tile(N_EXPERTS)],
        out_shape=[jax.ShapeDtypeStruct((bsz, n, d), F32), jax.ShapeDtypeStruct((bsz, lc, d), F32),
                   jax.ShapeDtypeStruct((bsz, lc + n, d // LANES, LANES), F32),
                   jax.ShapeDtypeStruct((bsz, lc + n, N_EXPERTS), F32)],
        compiler_params=_cparams("arbitrary", "arbitrary"),
        name="out_proj",
    )(y_s5, y_ssd, xc, xl, w[:hw], w[hw:], g1.reshape(1, d), gate1, g2.reshape(1, d), sc2, sh2,
      w_router.astype(BF16))


def _row_blocks(rows):
    return [(r0, min(MOE_ROW_BLOCK, rows - r0)) for r0 in range(0, rows, MOE_ROW_BLOCK)]


def _moe_kernel(idx_ref, h2_hbm, gate_ref, wg_ref, wu_ref, wd_ref, o_ref, xf_s, xb_s, h_s, sem, *, rows, nf, nsteps):
    e, s = pl.program_id(0), pl.program_id(1)
    ne = pl.num_programs(0)
    d = xb_s.shape[1]

    def row_copy(src_row, r):
        return pltpu.make_async_copy(h2_hbm.at[pl.ds(src_row, 1)], xf_s.at[pl.ds(r, 1)], sem.at[0])

    def issue(ex, lo, count):
        def body(k, c):
            r = lo + k
            row_copy(idx_ref[ex * rows + r], r).start()
            return c
        lax.fori_loop(0, count, body, 0, unroll=8)

    @pl.when(jnp.logical_and(e == 0, s == 0))
    def _():
        issue(0, 0, rows)

    @pl.when(s == 0)
    def _():
        pltpu.make_async_copy(h2_hbm.at[pl.ds(0, rows)], xf_s, sem.at[0]).wait()
        for r0, rb in _row_blocks(rows):
            xb_s[r0:r0 + rb, :] = xf_s[r0:r0 + rb].reshape(rb, d).astype(BF16)

    assert rows % MOE_GATHER_SLICES == 0 and nsteps > MOE_GATHER_SLICES
    per_step = rows // MOE_GATHER_SLICES

    @pl.when(jnp.logical_and(jnp.logical_and(s >= 1, s <= MOE_GATHER_SLICES), e + 1 < ne))
    def _():
        issue(e + 1, (s - 1) * per_step, per_step)

    @pl.when(s < nf)
    def _():
        def phase_a(wg_s, wu_s):
            wg_s[...] = wg_ref[0, 0].astype(BF16)
            wu_s[...] = wu_ref[0, 0].astype(BF16)
            for r0, rb in _row_blocks(rows):
                x = xb_s[r0:r0 + rb, :]
                g = jnp.dot(x, wg_s[...], preferred_element_type=F32)
                u = jnp.dot(x, wu_s[...], preferred_element_type=F32)
                h_s[s, r0:r0 + rb, :] = ((g * jax.nn.sigmoid(g)) * u).astype(BF16)
        pl.run_scoped(phase_a, pltpu.VMEM(wg_ref.shape[2:], BF16), pltpu.VMEM(wu_ref.shape[2:], BF16))

    @pl.when(s >= nf)
    def _():
        def phase_b(wd_s):
            wd_s[...] = wd_ref[0, 0].astype(BF16)
            for r0, rb in _row_blocks(rows):
                h = jnp.concatenate([h_s[f, r0:r0 + rb, :] for f in range(nf)], axis=1)
                y = jnp.dot(h, wd_s[...], preferred_element_type=F32) * gate_ref[0, r0:r0 + rb, :]
                o_ref[0, r0:r0 + rb] = y.reshape((rb,) + o_ref.shape[2:])
        pl.run_scoped(phase_b, pltpu.VMEM(wd_ref.shape[2:], BF16))


def moe_ffn(idx_rows, h2, gate, layer, w_gate, w_up, w_down):
    e, r, _ = gate.shape
    d, ff = w_gate.shape[2], w_gate.shape[3]
    tf, tn = MOE_F_TILE, MOE_N_TILE
    nf, nn = ff // tf, d // tn
    fcl = lambda s: jnp.minimum(s, nf - 1)
    ncl = lambda s: jnp.maximum(s - nf, 0)
    grid_spec = pltpu.PrefetchScalarGridSpec(
        num_scalar_prefetch=1,
        grid=(e, nf + nn),
        in_specs=[pl.BlockSpec(memory_space=pl.ANY),
                  pl.BlockSpec((1, r, 1), lambda i, s, idx: (i, 0, 0)),
                  pl.BlockSpec((1, 1, d, tf), lambda i, s, idx: (layer, i, 0, fcl(s))),
                  pl.BlockSpec((1, 1, d, tf), lambda i, s, idx: (layer, i, 0, fcl(s))),
                  pl.BlockSpec((1, 1, ff, tn), lambda i, s, idx: (layer, i, 0, ncl(s)))],
        out_specs=pl.BlockSpec((1, r, tn // LANES, LANES), lambda i, s, idx: (i, 0, ncl(s), 0)),
        scratch_shapes=[pltpu.VMEM((r, d // LANES, LANES), F32), pltpu.VMEM((r, d), BF16),
                        pltpu.VMEM((nf, r, tf), BF16), pltpu.SemaphoreType.DMA((1,))],
    )
    return pl.pallas_call(
        functools.partial(_moe_kernel, rows=r, nf=nf, nsteps=nf + nn),
        grid_spec=grid_spec,
        out_shape=jax.ShapeDtypeStruct((e, r, d // LANES, LANES), F32),
        compiler_params=pltpu.CompilerParams(dimension_semantics=("arbitrary", "arbitrary"),
                                             vmem_limit_bytes=MOE_VMEM_LIMIT),
        name="moe_ffn",
    )(idx_rows, h2, gate, w_gate, w_up, w_down)


def _combine_kernel(src_ref, starts_ref, tokl_ref, out_hbm, *refs, lat_tiles, tiles_per_sample, total, with_ctx):
    if with_ctx:
        xc_ref, xl_ref, gate_ref, g3_ref, xlo_ref, xco_ref, buf, acc, sem = refs
    else:
        xl_ref, gate_ref, g3_ref, xlo_ref, buf, acc, sem = refs
    j = pl.program_id(0)
    nt = pl.num_programs(0)
    ch = COMBINE_CHUNK
    first_of = lambda t: starts_ref[t] // ch
    nch_of = lambda t: jnp.where(starts_ref[t + 1] > starts_ref[t],
                                 (starts_ref[t + 1] + ch - 1) // ch - starts_ref[t] // ch, 0)

    def issue(t, c, slot):
        base = (first_of(t) + c) * ch

        def body(i, carry):
            for p in range(2):
                k = 2 * i + p
                pltpu.make_async_copy(out_hbm.at[pl.ds(src_ref[base + k], 1)], buf.at[slot, pl.ds(k, 1)],
                                      sem.at[slot]).start(priority=p)
            return carry
        lax.fori_loop(0, ch // 2, body, 0, unroll=4)

    nch = nch_of(j)

    @pl.when(jnp.logical_and(j == 0, nch > 0))
    def _():
        issue(j, 0, 0)

    acc[...] = jnp.zeros(acc.shape, F32)
    tok = lax.broadcasted_iota(jnp.int32, (ROW_TILE, ch), 0) + j * ROW_TILE

    def chunk(c, carry):
        slot = c % 2

        @pl.when(c + 1 < nch)
        def _():
            issue(j, c + 1, 1 - slot)

        pltpu.make_async_copy(out_hbm.at[pl.ds(0, ch)], buf.at[slot], sem.at[slot]).wait()
        rows = buf[slot].reshape(ch, acc.shape[1])
        onehot = (tok == tokl_ref[pl.ds(first_of(j) + c, 1), :]).astype(BF16)
        hi = rows.astype(BF16)
        lo = (rows - hi.astype(F32)).astype(BF16)
        acc[...] += (jnp.dot(onehot, hi, preferred_element_type=F32)
                     + jnp.dot(onehot, lo, preferred_element_type=F32))
        return carry
    lax.fori_loop(0, nch, chunk, 0)

    nxt = jnp.minimum(j + 1, nt - 1)

    @pl.when(jnp.logical_and(j + 1 < nt, nch_of(nxt) > 0))
    def _():
        issue(nxt, 0, 0)

    y = gate_ref[pl.ds(jnp.where(j >= lat_tiles, 2, j // tiles_per_sample), 1), :] * (_rms(acc[...]) * g3_ref[...])
    if with_ctx:
        @pl.when(j >= lat_tiles)
        def _():
            xco_ref[0] = xc_ref[0] + y

    @pl.when(j < lat_tiles)
    def _():
        xlo_ref[0] = xl_ref[0] + y


def moe_combine(src, starts, tokl, out_rows, xc, xl, gate2, g3, with_ctx):
    bsz, n, d = xl.shape
    lc = xc.shape[1]
    tps = n // ROW_TILE
    lat_tiles = bsz * tps
    nt = lat_tiles + (bsz if with_ctx else 0)
    lat_idx = lambda j: (jnp.minimum(j, lat_tiles - 1) // tps, jnp.minimum(j, lat_tiles - 1) % tps, 0)
    ctx_idx = lambda j: (jnp.clip(j - lat_tiles, 0, bsz - 1), 0, 0)
    lat_spec = pl.BlockSpec((1, ROW_TILE, d), lambda j, s0, s1: lat_idx(j))
    ctx_spec = pl.BlockSpec((1, lc, d), lambda j, s0, s1: ctx_idx(j))
    grid_spec = pltpu.PrefetchScalarGridSpec(
        num_scalar_prefetch=2,
        grid=(nt,),
        in_specs=[pl.BlockSpec(tokl.shape, lambda j, s0, s1: (0, 0)),
                  pl.BlockSpec(memory_space=pl.ANY)]
        + ([ctx_spec] if with_ctx else []) + [lat_spec,
                                              pl.BlockSpec((8, d), lambda j, s0, s1: (0, 0)),
                                              pl.BlockSpec((1, d), lambda j, s0, s1: (0, 0))],
        out_specs=[lat_spec] + ([ctx_spec] if with_ctx else []),
        scratch_shapes=[pltpu.VMEM((2, COMBINE_CHUNK, d // LANES, LANES), F32), pltpu.VMEM((ROW_TILE, d), F32),
                        pltpu.SemaphoreType.DMA((2,))],
    )
    res = pl.pallas_call(
        functools.partial(_combine_kernel, lat_tiles=lat_tiles, tiles_per_sample=tps, total=src.shape[0],
                          with_ctx=with_ctx),
        grid_spec=grid_spec,
        out_shape=[jax.ShapeDtypeStruct(xl.shape, F32)] + ([jax.ShapeDtypeStruct(xc.shape, F32)] if with_ctx else []),
        compiler_params=_cparams("arbitrary"),
        name="moe_combine",
    )(src, starts, tokl, out_rows, *([xc] if with_ctx else []), xl, gate2, g3.reshape(1, d))
    return (res[0], res[1]) if with_ctx else (res[0], xc)


def grid_transpose(x):
    b, n = x.shape[:2]
    rows = n // GRID_W
    return x.reshape((b, rows, GRID_W) + x.shape[2:]).swapaxes(1, 2).reshape(x.shape)


def route(logits, lc, with_ctx):
    bsz, t, e = logits.shape
    n = t - lc
    idx_l, gate_l, tok_l = [], [], []
    for lo, m, off in [(lc, n, 0)] + ([(0, lc, bsz * n)] if with_ctx else []):
        cap = CAPACITY_FACTOR * m // e
        aff = jax.nn.softmax(logits[:, lo:lo + m], axis=-1)
        gate, idx = lax.top_k(jnp.swapaxes(aff, 1, 2), cap)
        b_off = jnp.arange(bsz, dtype=jnp.int32)[:, None, None]
        idx_l.append(jnp.swapaxes(b_off * t + lo + idx, 0, 1).reshape(e, bsz * cap))
        tok_l.append(jnp.swapaxes(off + b_off * m + idx, 0, 1).reshape(e, bsz * cap))
        gate_l.append(jnp.swapaxes(gate, 0, 1).reshape(e, bsz * cap))
    idx_rows = jnp.concatenate(idx_l, axis=1).reshape(-1).astype(jnp.int32)
    gate = jnp.concatenate(gate_l, axis=1)[..., None]
    tok = jnp.concatenate(tok_l, axis=1).reshape(-1).astype(jnp.int32)
    total = tok.shape[0]
    assert total % COMBINE_CHUNK == 0
    tok_sorted, src = lax.sort((tok, jnp.arange(total, dtype=jnp.int32)), num_keys=1)
    tiles = (bsz * n + (bsz * lc if with_ctx else 0)) // ROW_TILE
    bounds = jnp.arange(tiles + 1, dtype=jnp.int32) * ROW_TILE
    starts = jnp.sum((tok[None, :] < bounds[:, None]).astype(jnp.int32), axis=1)
    return idx_rows, gate, src, starts, tok_sorted.reshape(total // COMBINE_CHUNK, COMBINE_CHUNK)


def kernel(x, c, ctx, c_ctx, ada_w, ada_b, norm_g, w_in, w_out, s5_lam_re, s5_lam_im, s5_log_step,
           s5_b_re, s5_b_im, s5_c_re, s5_c_im, s5_d, s5_w_glu, s5_b_glu, ssd_conv_w, ssd_conv_b,
           ssd_dt_bias, ssd_a_log, ssd_d, ssd_norm, moe_router, moe_w_gate, moe_w_up, moe_w_down):
    bsz, n, d = x.shape
    lc = ctx.shape[1]
    nseg = SUBLANES // bsz
    cs = jnp.zeros((8, d), F32).at[:bsz].set(c).at[2].set(c_ctx)
    mods = adaln_all(cs, ada_w, ada_b)
    s5_ops = jax.vmap(functools.partial(s5_compact_operators, seg_ctx=lc // nseg, seg_lat=n // nseg))(
        s5_lam_re, s5_lam_im, s5_log_step, s5_b_re, s5_b_im, s5_c_re, s5_c_im, s5_d)
    xl, xc = x, ctx
    for i in range(DEPTH):
        col_major = i % 2 == 1
        last = i == DEPTH - 1
        sh1, sc1, g1, sh2, sc2, g2 = [mods[i, :, k * d:(k + 1) * d] for k in range(N_MOD)]
        if col_major:
            xl = grid_transpose(xl)
        u, z, xbc, dtf, dtb = in_proj(xc, xl, norm_g[i, 0], sc1, sh1, w_in[i])
        y_s5 = s5_glu(s5_mix_packed(u, bsz, lc, n, i, s5_ops), bsz, lc, n, s5_w_glu[i], s5_b_glu[i])
        y_ssd = ssd_mix(z, xbc, dtf, dtb, lc, ssd_conv_w[i], ssd_conv_b[i], ssd_dt_bias[i],
                        ssd_a_log[i], ssd_d[i], ssd_norm[i])
        xl, xc, h2, logits = out_proj(y_s5, y_ssd, xc, xl, w_out[i], norm_g[i, 1], g1, norm_g[i, 2],
                                      sc2, sh2, moe_router[i])
        idx_rows, gate, src, starts, tokl = route(logits, lc, not last)
        out = moe_ffn(idx_rows, h2.reshape((bsz * (lc + n),) + h2.shape[2:]), gate, i,
                      moe_w_gate, moe_w_up, moe_w_down)
        xl, xc = moe_combine(src, starts, tokl, out.reshape((-1,) + out.shape[2:]), xc, xl, g2, norm_g[i, 3],
                             not last)
        if col_major:
            xl = grid_transpose(xl)
    return xl
```
